```python
import math, functools
import jax, jax.numpy as jnp
from jax import lax
import numpy as np

D_MODEL = 2048
BATCH = 1
SEQ = 8192
DEPTH = 2
DEC_BATCH = 32
DEC_SEQ = 16
PAST_LEN = 1024

CHUNK = 64
HEAD_DIM = 64
ATTN_SCALE = HEAD_DIM ** -0.5
Q_BLOCK = 128
EPS = 1e-6
H_A = 8
A_LEFT_CHUNKS = 8
A_PAST = A_LEFT_CHUNKS * CHUNK
A_REL_PAST = 128
A_REL_FUTURE = CHUNK - 1
A_N_REL = A_REL_PAST + A_REL_FUTURE + 1
H_B = 12
H_B_KV = 4
GQA = H_B // H_B_KV
H_IDX = 16
D_IDX = 64
TOPK_MAX = 256
H_C = 12
T5_BUCKETS = 32
T5_MAX_DIST = 128
N_EXPERTS = 32
N_GROUPS = 4
TOPK_EXPERTS = 2
D_EXPERT = 1024
MOE_BLOCK = 64

D_A = H_A * HEAD_DIM
D_B = H_B * HEAD_DIM
D_B_KV = H_B_KV * HEAD_DIM
D_C = H_C * HEAD_DIM
PROJ_SIZES = (D_A, D_A, D_A, D_B, D_B_KV, D_B_KV, H_IDX * D_IDX, D_IDX, H_IDX, D_C, D_C, D_C, H_C)
PROJ_TOTAL = sum(PROJ_SIZES)
PROJ_SPLITS = tuple(sum(PROJ_SIZES[:i + 1]) for i in range(len(PROJ_SIZES) - 1))

kernel_name = 'hybrid_streaming_encoder_step'


def rms_norm(x, g):
    xf = x.astype(jnp.float32)
    y = xf * lax.rsqrt(jnp.mean(xf * xf, axis=-1, keepdims=True) + EPS)
    return (y * g.astype(jnp.float32)).astype(x.dtype)


def t5_bucket(rel):
    nb = T5_BUCKETS // 2
    max_exact = nb // 2
    ret = jnp.where(rel < 0, nb, 0)
    n = jnp.abs(rel)
    nf = jnp.maximum(n, 1).astype(jnp.float32)
    large = max_exact + (jnp.log(nf / max_exact) / math.log(T5_MAX_DIST / max_exact) * (nb - max_exact)).astype(jnp.int32)
    large = jnp.minimum(large, nb - 1)
    return ret + jnp.where(n < max_exact, n, large)


def to_blocks(t):
    b, s = t.shape[:2]
    return jnp.moveaxis(t.reshape((b, s // Q_BLOCK, Q_BLOCK) + t.shape[2:]), 1, 0)


def from_blocks(t):
    nb, b, q = t.shape[:3]
    return jnp.moveaxis(t, 0, 1).reshape((b, nb * q) + t.shape[3:])


def band_attn(q, k, v, qpos, kpos, table):
    s = jnp.einsum('bnqhd,bnkhd->bnhqk', q, k).astype(jnp.float32) * ATTN_SCALE
    rel = qpos[:, :, None] - kpos[:, None, :]
    bias = jnp.moveaxis(table.astype(jnp.float32)[:, jnp.clip(rel, -A_REL_FUTURE, A_REL_PAST) + A_REL_FUTURE], 0, 1)
    qc = qpos[:, :, None] // CHUNK
    kc = kpos[:, None, :] // CHUNK
    ok = (kpos[:, None, :] >= 0) & (kc <= qc) & (kc >= qc - A_LEFT_CHUNKS)
    s = jnp.where(ok[None, :, None], s + bias[None], -jnp.inf)
    p = jax.nn.softmax(s, axis=-1).astype(v.dtype)
    return jnp.einsum('bnhqk,bnkhd->bnqhd', p, v)


def dsa_attn(q, qi, wi, qpos, k, v, kidx, kpos, t5, k_top):
    b, nq = q.shape[:2]
    isc = jnp.einsum('bqhe,ble->bqhl', qi, kidx).astype(jnp.float32) * D_IDX ** -0.5
    iscore = jnp.einsum('bqhl,bqh->bql', jax.nn.relu(isc), wi.astype(jnp.float32) * H_IDX ** -0.5)
    admissible = (kpos[None, :] // CHUNK) <= (qpos[:, None] // CHUNK)
    iscore = jnp.where(admissible[None], iscore, -jnp.inf)
    top_val, top_idx = lax.top_k(iscore, k_top)
    sel = jnp.isfinite(top_val)
    gather = jax.vmap(lambda rows, idx: rows[idx])
    kg = gather(k, top_idx)
    vg = gather(v, top_idx)
    qg = q.reshape(b, nq, H_B_KV, GQA, HEAD_DIM)
    s = jnp.einsum('bqngd,bqknd->bqngk', qg, kg).astype(jnp.float32) * ATTN_SCALE
    rel = qpos[None, :, None] - kpos[top_idx]
    bias = t5.astype(jnp.float32)[t5_bucket(rel)].reshape(b, nq, k_top, H_B_KV, GQA)
    s = jnp.where(sel[:, :, None, None, :], s + jnp.transpose(bias, (0, 1, 3, 4, 2)), -jnp.inf)
    p = jax.nn.softmax(s, axis=-1).astype(v.dtype)
    return jnp.einsum('bqngk,bqknd->bqngd', p, vg).reshape(b, nq, D_B)


def fox_attn(q, cq, qpos, k, v, ck, kpos):
    b, nq = q.shape[:2]
    decay = jnp.transpose(cq, (0, 2, 1))[:, :, :, None] - jnp.transpose(ck, (0, 2, 1))[:, :, None, :]
    s = jnp.einsum('bqhd,blhd->bhql', q, k).astype(jnp.float32) * ATTN_SCALE + decay
    s = jnp.where(kpos[None, :] <= qpos[:, None], s, -jnp.inf)
    p = jax.nn.softmax(s, axis=-1).astype(v.dtype)
    return jnp.einsum('bhql,blhd->bqhd', p, v).reshape(b, nq, D_C)


def project(h, w_in_l, b_f_l, g):
    b, s = h.shape[:2]
    qa, ka, va, qb, kb, vb, qi, ki, wi, qc, kc, vc, fc = jnp.split(h @ w_in_l, PROJ_SPLITS, axis=-1)
    hd = lambda t, n: t.reshape(b, s, n, HEAD_DIM)
    qa = rms_norm(hd(qa, H_A), g[0])
    ka = rms_norm(hd(ka, H_A), g[1])
    qb = rms_norm(hd(qb, H_B), g[2])
    kb = rms_norm(hd(kb, H_B_KV), g[3])
    qc = rms_norm(hd(qc, H_C), g[4])
    kc = rms_norm(hd(kc, H_C), g[5])
    qi = qi.reshape(b, s, H_IDX, D_IDX)
    logf = jax.nn.log_sigmoid(fc.astype(jnp.float32) + b_f_l.astype(jnp.float32))
    return (qa, ka, hd(va, H_A), qb, kb, hd(vb, H_B_KV), qi, ki, wi, qc, kc, hd(vc, H_C), logf)


def mixers_prompt(pr, rel_a, t5):
    qa, ka, va, qb, kb, vb, qi, ki, wi, qc, kc, vc, logf = pr
    b, s = qa.shape[:2]
    nc = s // CHUNK
    nband = A_LEFT_CHUNKS + 1

    def band(t):
        tc = t.reshape(b, nc, CHUNK, H_A, HEAD_DIM)
        tp = jnp.pad(tc, ((0, 0), (A_LEFT_CHUNKS, 0), (0, 0), (0, 0), (0, 0)))
        return jnp.concatenate([tp[:, j:j + nc] for j in range(nband)], axis=2)

    qpos_a = jnp.arange(s).reshape(nc, CHUNK)
    kpos_a = (jnp.arange(nc)[:, None] - A_LEFT_CHUNKS) * CHUNK + jnp.arange(nband * CHUNK)[None, :]
    oa = band_attn(qa.reshape(b, nc, CHUNK, H_A, HEAD_DIM), band(ka), band(va), qpos_a, kpos_a, rel_a).reshape(b, s, D_A)

    pos = jnp.arange(s)
    pos_blk = pos.reshape(-1, Q_BLOCK)
    k_top = min(TOPK_MAX, s // 4)
    ob = from_blocks(lax.map(lambda a: dsa_attn(a[0], a[1], a[2], a[3], kb, vb, ki, pos, t5, k_top),
                             (to_blocks(qb), to_blocks(qi), to_blocks(wi), pos_blk)))

    cum = jnp.cumsum(logf, axis=1)
    oc = from_blocks(lax.map(lambda a: fox_attn(a[0], a[1], a[2], kc, vc, cum, pos),
                             (to_blocks(qc), to_blocks(cum), pos_blk)))

    keep = min(A_PAST, s)
    state = (jnp.stack([ka, va], axis=2)[:, s - keep:], jnp.stack([kb, vb], axis=2), ki,
             jnp.stack([kc, vc], axis=2), logf)
    return jnp.concatenate([oa, ob, oc], axis=-1), state


def mixers_sample(pr, ca_kv, cb_kv, cb_kidx, cc_kv, cc_logf, rel_a, t5):
    qa, ka, va, qb, kb, vb, qi, ki, wi, qc, kc, vc, logf = pr
    b, t = qa.shape[:2]
    past = cb_kv.shape[1]
    na = ca_kv.shape[1]
    qpos = past + jnp.arange(t)
    ka_all = jnp.concatenate([ca_kv[:, :, 0], ka], axis=1)
    va_all = jnp.concatenate([ca_kv[:, :, 1], va], axis=1)
    kpos_a = jnp.concatenate([jnp.arange(past - na, past), qpos])
    oa = band_attn(qa[:, None], ka_all[:, None], va_all[:, None], qpos[None], kpos_a[None], rel_a).reshape(b, t, D_A)
    kpos = jnp.arange(past + t)
    kb_all = jnp.concatenate([cb_kv[:, :, 0], kb], axis=1)
    vb_all = jnp.concatenate([cb_kv[:, :, 1], vb], axis=1)
    ki_all = jnp.concatenate([cb_kidx, ki], axis=1)
    k_top = min(TOPK_MAX, (past + t) // 4)
    ob = dsa_attn(qb, qi, wi, qpos, kb_all, vb_all, ki_all, kpos, t5, k_top)
    kc_all = jnp.concatenate([cc_kv[:, :, 0], kc], axis=1)
    vc_all = jnp.concatenate([cc_kv[:, :, 1], vc], axis=1)
    cum = jnp.cumsum(jnp.concatenate([cc_logf.astype(jnp.float32), logf], axis=1), axis=1)
    oc = fox_attn(qc, cum[:, past:], qpos, kc_all, vc_all, cum, kpos)
    state = (jnp.stack([ka, va], axis=2), jnp.stack([kb, vb], axis=2), ki, jnp.stack([kc, vc], axis=2), logf)
    return jnp.concatenate([oa, ob, oc], axis=-1), state


def route(t, w_router, b_router):
    scores = jax.nn.sigmoid((t @ w_router).astype(jnp.float32))
    sel = scores + b_router.astype(jnp.float32)
    per_group = N_EXPERTS // N_GROUPS
    gscore = lax.top_k(sel.reshape(-1, N_GROUPS, per_group), 2)[0].sum(-1)
    gbest = jnp.argmax(gscore, axis=-1)
    in_group = (jnp.arange(N_EXPERTS) // per_group)[None, :] == gbest[:, None]
    _, eidx = lax.top_k(jnp.where(in_group, sel, -jnp.inf), TOPK_EXPERTS)
    w = jnp.take_along_axis(scores, eidx, axis=-1)
    return eidx, w / jnp.sum(w, axis=-1, keepdims=True)


def moe(h, w_router, b_router, w_g, w_u, w_d):
    b, s, d = h.shape
    t = h.reshape(b * s, d)
    n_tok = b * s
    n_asg = n_tok * TOPK_EXPERTS
    eidx, gate = route(t, w_router, b_router)
    flat_e = eidx.reshape(-1)
    flat_tok = jnp.repeat(jnp.arange(n_tok), TOPK_EXPERTS)
    flat_g = gate.reshape(-1)
    order = jnp.argsort(flat_e)
    se = flat_e[order]
    tok_sorted = flat_tok[order]
    counts = jnp.bincount(flat_e, length=N_EXPERTS)
    padded = (counts + MOE_BLOCK - 1) // MOE_BLOCK * MOE_BLOCK
    pend = jnp.cumsum(padded)
    pstart = pend - padded
    start = jnp.cumsum(counts) - counts
    dest = pstart[se] + jnp.arange(n_asg) - start[se]
    nb = -(-n_asg // MOE_BLOCK) + N_EXPERTS
    slot_tok = jnp.full((nb * MOE_BLOCK,), n_tok, jnp.int32).at[dest].set(tok_sorted)
    block_e = jnp.minimum(jnp.searchsorted(pend, jnp.arange(nb) * MOE_BLOCK, side='right'), N_EXPERTS - 1)
    tp = jnp.concatenate([t, jnp.zeros((1, d), t.dtype)], axis=0)
    xb = tp[slot_tok].reshape(nb, MOE_BLOCK, d)

    def expert_block(a):
        xblk, e = a
        return (jax.nn.silu(xblk @ w_g[e]) * (xblk @ w_u[e])) @ w_d[e]

    yb = lax.map(expert_block, (xb, block_e)).reshape(nb * MOE_BLOCK, d)
    contrib = yb[dest] * flat_g[order][:, None].astype(yb.dtype)
    out = jnp.zeros((n_tok, d), h.dtype).at[tok_sorted].add(contrib.astype(h.dtype))
    return out.reshape(b, s, d)


def layer(x, c, w_ada_l, b_ada_l, gain_l, w_in_l, b_f_l, qk_l, w_out_l, w_router, b_router, w_g, w_u, w_d, mix_fn):
    m = (jax.nn.silu(c) @ w_ada_l + b_ada_l)[:, None, :]
    sh1, sc1, g1, sh2, sc2, g2 = jnp.split(m, 6, axis=-1)
    h = rms_norm(x, gain_l[0]) * (1 + sc1) + sh1
    mix, state = mix_fn(project(h, w_in_l, b_f_l, qk_l))
    x = x + g1 * (mix @ w_out_l)
    h2 = rms_norm(x, gain_l[1]) * (1 + sc2) + sh2
    x = x + g2 * moe(h2, w_router, b_router, w_g, w_u, w_d)
    return x, state


def setup_inputs(seed: int = 0) -> dict:
    key = jax.random.key(seed)
    ks = jax.random.split(key, 26)
    nrm = lambda k, shp, sc: jax.random.normal(k, shp, jnp.float32) * sc
    a_rows = min(A_PAST, PAST_LEN)
    return {
        'x_prompt': nrm(ks[0], (BATCH, SEQ, D_MODEL), 1.0),
        'x_sample': nrm(ks[1], (DEC_BATCH, DEC_SEQ, D_MODEL), 1.0),
        'c_prompt': nrm(ks[2], (BATCH, D_MODEL), 1.0),
        'c_sample': nrm(ks[3], (DEC_BATCH, D_MODEL), 1.0),
        'cache_a_kv': nrm(ks[4], (DEPTH, DEC_BATCH, a_rows, 2, H_A, HEAD_DIM), 1.0),
        'cache_b_kv': nrm(ks[5], (DEPTH, DEC_BATCH, PAST_LEN, 2, H_B_KV, HEAD_DIM), 1.0),
        'cache_b_kidx': nrm(ks[6], (DEPTH, DEC_BATCH, PAST_LEN, D_IDX), 1.0),
        'cache_c_kv': nrm(ks[7], (DEPTH, DEC_BATCH, PAST_LEN, 2, H_C, HEAD_DIM), 1.0),
        'cache_c_logf': jax.nn.log_sigmoid(3.0 + nrm(ks[8], (DEPTH, DEC_BATCH, PAST_LEN, H_C), 1.0)),
        'w_ada': nrm(ks[9], (DEPTH, D_MODEL, 6 * D_MODEL), 0.5 * D_MODEL ** -0.5),
        'b_ada': nrm(ks[10], (DEPTH, 6 * D_MODEL), 0.02),
        'norm_gain': 1.0 + nrm(ks[11], (DEPTH, 2, D_MODEL), 0.05),
        'w_in': nrm(ks[12], (DEPTH, D_MODEL, PROJ_TOTAL), D_MODEL ** -0.5),
        'b_forget': 3.0 + nrm(ks[13], (DEPTH, H_C), 0.5),
        'qk_gain': 1.0 + nrm(ks[14], (DEPTH, 6, HEAD_DIM), 0.05),
        'rel_bias_a': nrm(ks[15], (DEPTH, H_A, A_N_REL), 0.2),
        't5_bias': nrm(ks[16], (T5_BUCKETS, H_B), 0.2),
        'w_out': nrm(ks[17], (DEPTH, D_MODEL, D_MODEL), D_MODEL ** -0.5),
        'w_router': nrm(ks[18], (D_MODEL, N_EXPERTS), D_MODEL ** -0.5),
        'b_router': nrm(ks[19], (N_EXPERTS,), 0.01),
        'w_e_gate': nrm(ks[20], (DEPTH, N_EXPERTS, D_MODEL, D_EXPERT), D_MODEL ** -0.5),
        'w_e_up': nrm(ks[21], (DEPTH, N_EXPERTS, D_MODEL, D_EXPERT), D_MODEL ** -0.5),
        'w_e_down': nrm(ks[22], (DEPTH, N_EXPERTS, D_EXPERT, D_MODEL), D_EXPERT ** -0.5),
    }


def reference(x_prompt, x_sample, c_prompt, c_sample, cache_a_kv, cache_b_kv, cache_b_kidx, cache_c_kv, cache_c_logf,
              w_ada, b_ada, norm_gain, w_in, b_forget, qk_gain, rel_bias_a, t5_bias, w_out, w_router, b_router,
              w_e_gate, w_e_up, w_e_down):
    xp, xs = x_prompt, x_sample
    sp, ss = [], []
    for l in range(DEPTH):
        shared = (w_ada[l], b_ada[l], norm_gain[l], w_in[l], b_forget[l], qk_gain[l], w_out[l], w_router, b_router,
                  w_e_gate[l], w_e_up[l], w_e_down[l])
        xp, st = layer(xp, c_prompt, *shared, functools.partial(mixers_prompt, rel_a=rel_bias_a[l], t5=t5_bias))
        sp.append(st)
        xs, st = layer(xs, c_sample, *shared,
                       functools.partial(mixers_sample, ca_kv=cache_a_kv[l], cb_kv=cache_b_kv[l], cb_kidx=cache_b_kidx[l],
                                         cc_kv=cache_c_kv[l], cc_logf=cache_c_logf[l], rel_a=rel_bias_a[l], t5=t5_bias))
        ss.append(st)
    stk = lambda sts, i: jnp.stack([st[i] for st in sts], axis=0)
    return (xp, xs, stk(sp, 0), stk(sp, 1), stk(sp, 2), stk(sp, 3), stk(sp, 4),
            stk(ss, 0), stk(ss, 1), stk(ss, 2), stk(ss, 3), stk(ss, 4))
```

```python
import functools
import math

import jax
import jax.numpy as jnp
from jax import lax
from jax.experimental import pallas as pl
from jax.experimental.pallas import tpu as pltpu

F32 = jnp.float32
BF16 = jnp.bfloat16
I32 = jnp.int32

D_MODEL = 2048
HEAD_DIM = 64
CHUNK = 64
EPS = 1e-6
ATTN_SCALE = HEAD_DIM ** -0.5
H_A = 8
A_LEFT_CHUNKS = 8
A_REL_PAST = 128
A_REL_FUTURE = CHUNK - 1
H_B = 12
H_B_KV = 4
GQA = H_B // H_B_KV
H_IDX = 16
D_IDX = 64
TOPK_MAX = 256
H_C = 12
T5_BUCKETS = 32
T5_MAX_DIST = 128
N_EXPERTS = 32
N_GROUPS = 4
D_EXPERT = 1024

D_A = H_A * HEAD_DIM
D_B = H_B * HEAD_DIM
D_B_KV = H_B_KV * HEAD_DIM
D_C = H_C * HEAD_DIM
PROJ_SIZES = (D_A, D_A, D_A, D_B, D_B_KV, D_B_KV, H_IDX * D_IDX, D_IDX, H_IDX, D_C, D_C, D_C, H_C)
PROJ_SPLITS = tuple(sum(PROJ_SIZES[:i + 1]) for i in range(len(PROJ_SIZES) - 1))

OFF_QA, OFF_KA, OFF_VA = 0, 512, 1024
OFF_KB, OFF_VB = 1536, 1792
OFF_QI = 2048
OFF_QB = 3072
OFF_QC, OFF_KC, OFF_VC = 3840, 4608, 5376
OFF_MISC = 6144
MISC_WI = 64
MISC_FC = 80
P_COLS = 6400
PROJ_TN = 256
NORM_COL_BLOCKS = (0, 1, 2, 3, 6, 12, 13, 14, 15, 16, 17, 18, 19, 20)
MISC_COL_BLOCK = OFF_MISC // PROJ_TN

TOKEN_TILE = 512
LANES = 128
VMEM_LIMIT = 56 * 1024 * 1024

NEG_INF = float("-inf")
NEG_KEY = (0xFF800000 ^ 0x7FFFFFFF) - (1 << 32)
IDX_BIG = 1 << 30


def _cparams(n_axes):
    return pltpu.CompilerParams(dimension_semantics=("arbitrary",) * n_axes, vmem_limit_bytes=VMEM_LIMIT)


def _split3(x):
    x1 = x.astype(BF16)
    r1 = x - x1.astype(F32)
    x2 = r1.astype(BF16)
    r2 = r1 - x2.astype(F32)
    return x1, x2, r2.astype(BF16)


def _dot(a, b):
    return jnp.dot(a, b, preferred_element_type=F32)


def _dot_t(a, b):
    return lax.dot_general(a, b, (((1,), (1,)), ((), ())), preferred_element_type=F32)


def _sortable(x):
    b = lax.bitcast_convert_type(x, I32)
    return jnp.where(b < 0, b ^ jnp.int32(0x7FFFFFFF), b)


def _ada_kernel(c_ref, w_ref, b_ref, o_ref):
    c = c_ref[...]
    a = (c * jax.nn.sigmoid(c)).astype(BF16)
    o_ref[...] = _dot(a, w_ref[...].astype(BF16)) + b_ref[...]


def _ada(c_all, w_ada, b_ada, layer):
    rows = c_all.shape[0]
    n_out = w_ada.shape[2]
    tn = 1024
    return pl.pallas_call(
        _ada_kernel,
        grid=(n_out // tn,),
        in_specs=[
            pl.BlockSpec((rows, D_MODEL), lambda n: (0, 0)),
            pl.BlockSpec((None, D_MODEL, tn), lambda n: (layer, 0, n)),
            pl.BlockSpec((None, 1, tn), lambda n: (layer, 0, n)),
        ],
        out_specs=pl.BlockSpec((rows, tn), lambda n: (0, n)),
        out_shape=jax.ShapeDtypeStruct((rows, n_out), F32),
        compiler_params=_cparams(1),
    )(c_all, w_ada, b_ada.reshape(b_ada.shape[0], 1, n_out))


def _mod_block_index(i, n_prompt_blocks):
    return jnp.where(i < n_prompt_blocks, 0, i - n_prompt_blocks + 1)


def _norm_mod(x, gain, sc, sh):
    ms = jnp.mean(x * x, axis=-1, keepdims=True)
    return (x * lax.rsqrt(ms + EPS) * gain) * (1.0 + sc) + sh


def _proj_kernel(x_ref, gain_ref, sc_ref, sh_ref, w_ref, gq_ref, bq_ref, bd_ref, o32_ref, o16_ref, h_scr):
    n = pl.program_id(1)

    @pl.when(n == 0)
    def _():
        h_scr[...] = _norm_mod(x_ref[...], gain_ref[...], sc_ref[...], sh_ref[...]).astype(BF16)

    y = _dot(h_scr[...], w_ref[...])

    is_norm = functools.reduce(jnp.logical_or, [n == b for b in NORM_COL_BLOCKS])
    is_misc = n == MISC_COL_BLOCK

    def emit(v):
        o32_ref[...] = v
        o16_ref[...] = v.astype(BF16)

    @pl.when(is_norm)
    def _():
        s1, s2, s3 = _split3(y * y)
        bd = bd_ref[...]
        ms = _dot(s1, bd) + _dot(s2, bd) + _dot(s3, bd)
        emit(y * lax.rsqrt(ms + EPS) * gq_ref[...])

    @pl.when(is_misc)
    def _():
        lane = lax.broadcasted_iota(I32, y.shape, 1)
        z = y + bq_ref[...]
        logsig = jnp.minimum(z, 0.0) - jnp.log(1.0 + jnp.exp(-jnp.abs(z)))
        emit(jnp.where((lane >= MISC_FC) & (lane < MISC_FC + H_C), logsig, y))

    @pl.when(jnp.logical_not(jnp.logical_or(is_norm, is_misc)))
    def _():
        emit(y)


def _project(x, gain, sc, sh, w_r, gq, bq, bd, n_prompt_blocks):
    m = x.shape[0]
    tm = TOKEN_TILE
    mod_map = lambda i, n: (_mod_block_index(i, n_prompt_blocks), 0)
    return pl.pallas_call(
        _proj_kernel,
        grid=(m // tm, P_COLS // PROJ_TN),
        in_specs=[
            pl.BlockSpec((tm, D_MODEL), lambda i, n: (i, 0)),
            pl.BlockSpec((1, D_MODEL), lambda i, n: (0, 0)),
            pl.BlockSpec((tm, D_MODEL), mod_map),
            pl.BlockSpec((tm, D_MODEL), mod_map),
            pl.BlockSpec((D_MODEL, PROJ_TN), lambda i, n: (0, n)),
            pl.BlockSpec((1, PROJ_TN), lambda i, n: (0, n)),
            pl.BlockSpec((1, PROJ_TN), lambda i, n: (0, n)),
            pl.BlockSpec((PROJ_TN, PROJ_TN), lambda i, n: (0, 0)),
        ],
        out_specs=[
            pl.BlockSpec((tm, PROJ_TN), lambda i, n: (i, n)),
            pl.BlockSpec((tm, PROJ_TN), lambda i, n: (i, n)),
        ],
        out_shape=[jax.ShapeDtypeStruct((m, P_COLS), F32), jax.ShapeDtypeStruct((m, P_COLS), BF16)],
        scratch_shapes=[pltpu.VMEM((tm, D_MODEL), BF16)],
        compiler_params=_cparams(2),
    )(x, gain, sc, sh, w_r, gq, bq, bd)


CUM_TILE = 256


def _cumsum_kernel(x_ref, tri_ref, o_ref, carry):
    @pl.when(pl.program_id(0) == 0)
    def _():
        carry[...] = jnp.zeros_like(carry)

    x1, x2, x3 = _split3(x_ref[...])
    tri = tri_ref[...]
    c = _dot(tri, x1) + _dot(tri, x2) + _dot(tri, x3) + carry[...]
    o_ref[...] = c
    carry[...] = c[CUM_TILE - 1:CUM_TILE, :]


def _cumsum_rows(p32, s, tri):
    return pl.pallas_call(
        _cumsum_kernel,
        grid=(s // CUM_TILE,),
        in_specs=[
            pl.BlockSpec((CUM_TILE, LANES), lambda i: (i, OFF_MISC // LANES)),
            pl.BlockSpec((CUM_TILE, CUM_TILE), lambda i: (0, 0)),
        ],
        out_specs=pl.BlockSpec((CUM_TILE, LANES), lambda i: (i, 0)),
        out_shape=jax.ShapeDtypeStruct((s, LANES), F32),
        scratch_shapes=[pltpu.VMEM((1, LANES), F32)],
        compiler_params=_cparams(1),
    )(p32, tri)


QB_A = 128
A_KEY_BLOCKS = 5


def _band_kernel(q_ref, *refs):
    k_refs = refs[:A_KEY_BLOCKS]
    v_refs = refs[A_KEY_BLOCKS:2 * A_KEY_BLOCKS]
    bias_ref, o_ref = refs[2 * A_KEY_BLOCKS:]
    i = pl.program_id(0)
    for h in range(H_A):
        hs = slice(h * HEAD_DIM, (h + 1) * HEAD_DIM)
        q = q_ref[:, hs]
        parts = []
        for j in range(A_KEY_BLOCKS):
            s = _dot_t(q, k_refs[j][:, hs]) * ATTN_SCALE + bias_ref[h, :, j * QB_A:(j + 1) * QB_A]
            in_range = i - (A_KEY_BLOCKS - 1) + j >= 0
            parts.append(jnp.where(in_range, s, NEG_INF))
        m = functools.reduce(jnp.maximum, [jnp.max(s, axis=1, keepdims=True) for s in parts])
        l = jnp.zeros((QB_A, 1), F32)
        o = jnp.zeros((QB_A, HEAD_DIM), F32)
        for j in range(A_KEY_BLOCKS):
            p = jnp.exp(parts[j] - m)
            l = l + jnp.sum(p, axis=1, keepdims=True)
            o = o + _dot(p.astype(BF16), v_refs[j][:, hs])
        o_ref[:, hs] = (o / l).astype(BF16)


def _band_prompt(pb, s, bias_a):
    nq = s // QB_A
    back = A_KEY_BLOCKS - 1

    def kv_spec(j, col):
        return pl.BlockSpec((QB_A, D_A), lambda i: (jnp.maximum(i - back + j, 0), col))

    in_specs = [pl.BlockSpec((QB_A, D_A), lambda i: (i, OFF_QA // D_A))]
    in_specs += [kv_spec(j, OFF_KA // D_A) for j in range(A_KEY_BLOCKS)]
    in_specs += [kv_spec(j, OFF_VA // D_A) for j in range(A_KEY_BLOCKS)]
    in_specs += [pl.BlockSpec((H_A, QB_A, A_KEY_BLOCKS * QB_A), lambda i: (0, 0, 0))]
    return pl.pallas_call(
        _band_kernel,
        grid=(nq,),
        in_specs=in_specs,
        out_specs=pl.BlockSpec((QB_A, D_A), lambda i: (i, 0)),
        out_shape=jax.ShapeDtypeStruct((s, D_A), BF16),
        compiler_params=_cparams(1),
    )(pb, *([pb] * (2 * A_KEY_BLOCKS)), bias_a)


def _kth_largest(count_ge, rows, k):
    def body(it, carry):
        lo, cnt_lo = carry
        cand = lo + lax.shift_left(jnp.int32(1), jnp.int32(31) - it)
        c = count_ge(cand)
        keep = c >= float(k)
        return jnp.where(keep, cand, lo), jnp.where(keep, c, cnt_lo)

    lo0 = jnp.full((rows, 1), -(1 << 31), I32)
    cnt0 = count_ge(lo0)
    return lax.fori_loop(0, 32, body, (lo0, cnt0))


def _tie_limit(count_eq_below, need, rows):
    def body(it, j):
        cand = j + lax.shift_left(jnp.int32(1), jnp.int32(14) - it)
        return jnp.where(count_eq_below(cand) <= need, cand, j)

    return lax.fori_loop(0, 15, body, jnp.zeros((rows, 1), I32))


QB_B = 128
KB_B = 128


def _dsa_kernel(c15_ref, qb_ref, qi_ref, misc_ref, kidx_ref, kb_ref, vb_ref, tb_ref, o_ref,
                key_scr, w_scr, thr_scr, m_scr, l_scr, acc_scr, *, k_top):
    i = pl.program_id(0)
    n_kb = i + 1
    row = lax.broadcasted_iota(I32, (QB_B, KB_B), 0)
    col = lax.broadcasted_iota(I32, (QB_B, KB_B), 1)

    wi = misc_ref[:, MISC_WI:MISC_WI + H_IDX] * (H_IDX ** -0.5 * D_IDX ** -0.5)
    for h in range(H_IDX):
        w_scr[h] = jnp.broadcast_to(wi[:, h:h + 1], (QB_B, KB_B))

    def score_block(j, _):
        kblk = kidx_ref[pl.ds(pl.multiple_of(j * KB_B, KB_B), KB_B), :][:, :D_IDX]
        acc = jnp.zeros((QB_B, KB_B), F32)
        for h in range(H_IDX):
            isc = _dot_t(qi_ref[:, h * D_IDX:(h + 1) * D_IDX], kblk)
            acc = acc + jnp.maximum(isc, 0.0) * w_scr[h]
        admissible = (j * KB_B + col) // CHUNK <= (i * QB_B + row) // CHUNK
        key_scr[:, pl.ds(pl.multiple_of(j * KB_B, KB_B), KB_B)] = jnp.where(
            admissible, _sortable(acc), jnp.int32(NEG_KEY))
        return 0

    lax.fori_loop(0, n_kb, score_block, 0)

    def count_where(pred):
        def body(j, c):
            blk = key_scr[:, pl.ds(pl.multiple_of(j * KB_B, KB_B), KB_B)]
            return c + jnp.where(pred(blk, j * KB_B + col), 1.0, 0.0)
        c = lax.fori_loop(0, n_kb, body, jnp.zeros((QB_B, KB_B), F32))
        return jnp.sum(c, axis=1, keepdims=True)

    thr, cnt_ge = _kth_largest(lambda cand: count_where(lambda blk, idx: blk >= cand), QB_B, k_top)
    live = thr > jnp.int32(NEG_KEY)
    thr = jnp.maximum(thr, jnp.int32(NEG_KEY))
    thr_scr[0] = jnp.broadcast_to(thr, (QB_B, KB_B))
    thr_scr[1] = jnp.broadcast_to(jnp.where(live, jnp.int32(IDX_BIG), 0), (QB_B, KB_B))
    has_ties = jnp.max(jnp.where(live & (cnt_ge > float(k_top)), 1.0, 0.0)) > 0.0

    @pl.when(has_ties)
    def _():
        cnt_gt = count_where(lambda blk, idx: blk > thr)
        need = float(k_top) - cnt_gt
        jstar = _tie_limit(lambda cand: count_where(lambda blk, idx: (blk == thr) & (idx < cand)), need, QB_B)
        thr_scr[1] = jnp.broadcast_to(jnp.where(live, jstar, 0), (QB_B, KB_B))

    m_scr[...] = jnp.full(m_scr.shape, -1e30, F32)
    l_scr[...] = jnp.zeros(l_scr.shape, F32)
    acc_scr[...] = jnp.zeros(acc_scr.shape, F32)

    def attend_block(j, near_off):
        start = pl.multiple_of(j * KB_B, KB_B)
        keys = key_scr[:, pl.ds(start, KB_B)]
        thr_b = thr_scr[0]
        sel = (keys > thr_b) | ((keys == thr_b) & (j * KB_B + col < thr_scr[1]))
        kblk = kb_ref[pl.ds(start, KB_B), :]
        vblk = vb_ref[pl.ds(start, KB_B), :]
        for n in range(H_B_KV):
            ns = slice(n * HEAD_DIM, (n + 1) * HEAD_DIM)
            k_n = kblk[:, ns]
            v_n = vblk[:, ns]
            for g in range(GQA):
                h = n * GQA + g
                hs = slice(h * HEAD_DIM, (h + 1) * HEAD_DIM)
                s = _dot_t(qb_ref[:, hs], k_n) * ATTN_SCALE
                if near_off is None:
                    s = s + c15_ref[h]
                else:
                    s = s + tb_ref[h, :, near_off:near_off + KB_B]
                s = jnp.where(sel, s, NEG_INF)
                m_prev = m_scr[h]
                m_new = jnp.maximum(m_prev, jnp.max(s, axis=1, keepdims=True))
                alpha = jnp.exp(m_prev - m_new)
                p = jnp.exp(s - m_new)
                l_scr[h] = alpha * l_scr[h] + jnp.sum(p, axis=1, keepdims=True)
                acc_scr[h] = acc_scr[h] * alpha[:, :HEAD_DIM] + _dot(p.astype(BF16), v_n)
                m_scr[h] = m_new

    def far_body(j, _):
        attend_block(j, None)
        return 0

    lax.fori_loop(0, jnp.maximum(i - 1, 0), far_body, 0)

    @pl.when(i >= 1)
    def _():
        attend_block(i - 1, 0)

    attend_block(i, KB_B)

    for h in range(H_B):
        hs = slice(h * HEAD_DIM, (h + 1) * HEAD_DIM)
        o_ref[:, hs] = (acc_scr[h] / l_scr[h][:, :HEAD_DIM]).astype(BF16)


def _dsa_prompt(p32, pb, s, t5_near, t5_far):
    nq = s // QB_B
    k_top = min(TOPK_MAX, s // 4)
    grid_spec = pltpu.PrefetchScalarGridSpec(
        num_scalar_prefetch=0,
        grid=(nq,),
        in_specs=[
            pl.BlockSpec(memory_space=pltpu.SMEM),
            pl.BlockSpec((QB_B, D_B), lambda i: (i, OFF_QB // D_B)),
            pl.BlockSpec((QB_B, H_IDX * D_IDX), lambda i: (i, OFF_QI // (H_IDX * D_IDX))),
            pl.BlockSpec((QB_B, LANES), lambda i: (i, OFF_MISC // LANES)),
            pl.BlockSpec((s, LANES), lambda i: (0, OFF_MISC // LANES)),
            pl.BlockSpec((s, D_B_KV), lambda i: (0, OFF_KB // D_B_KV)),
            pl.BlockSpec((s, D_B_KV), lambda i: (0, OFF_VB // D_B_KV)),
            pl.BlockSpec((H_B, QB_B, 2 * KB_B), lambda i: (0, 0, 0)),
        ],
        out_specs=pl.BlockSpec((QB_B, D_B), lambda i: (i, 0)),
        scratch_shapes=[
            pltpu.VMEM((QB_B, s), I32),
            pltpu.VMEM((H_IDX, QB_B, KB_B), F32),
            pltpu.VMEM((2, QB_B, KB_B), I32),
            pltpu.VMEM((H_B, QB_B, KB_B), F32),
            pltpu.VMEM((H_B, QB_B, KB_B), F32),
            pltpu.VMEM((H_B, QB_B, HEAD_DIM), F32),
        ],
    )
    return pl.pallas_call(
        functools.partial(_dsa_kernel, k_top=k_top),
        grid_spec=grid_spec,
        out_shape=jax.ShapeDtypeStruct((s, D_B), BF16),
        compiler_params=_cparams(1),
    )(t5_far, pb, pb, p32, pb, pb, pb, t5_near)


QB_C = 256
HEADS_PER_STEP_C = LANES // HEAD_DIM


def _fox_kernel(q_ref, k_ref, v_ref, cumt_ref, cumc_ref, o_ref):
    g = pl.program_id(0)
    i = pl.program_id(1)
    row = lax.broadcasted_iota(I32, (QB_C, QB_C), 0)
    col = lax.broadcasted_iota(I32, (QB_C, QB_C), 1)
    lane16 = lax.broadcasted_iota(I32, cumc_ref.shape, 1)
    for hh in range(HEADS_PER_STEP_C):
        h = g * HEADS_PER_STEP_C + hh
        hs = slice(hh * HEAD_DIM, (hh + 1) * HEAD_DIM)
        q = q_ref[:, hs] * ATTN_SCALE
        cq = jnp.sum(jnp.where(lane16 == h, cumc_ref[...], 0.0), axis=1, keepdims=True)

        def scores(j):
            start = pl.multiple_of(j * QB_C, QB_C)
            ck = cumt_ref[pl.ds(h, 1), pl.ds(start, QB_C)]
            s = _dot_t(q, k_ref[pl.ds(start, QB_C), hs]) + cq - ck
            return s, v_ref[pl.ds(start, QB_C), hs]

        s, v = scores(i)
        s = jnp.where(col <= row, s, NEG_INF)
        m = jnp.max(s, axis=1, keepdims=True)
        p = jnp.exp(s - m)
        l = jnp.sum(p, axis=1, keepdims=True)
        acc = _dot(p.astype(BF16), v)

        def body(j, carry):
            m, l, acc = carry
            s, v = scores(j)
            m_new = jnp.maximum(m, jnp.max(s, axis=1, keepdims=True))
            alpha = jnp.exp(m - m_new)
            p = jnp.exp(s - m_new)
            return (m_new, alpha * l + jnp.sum(p, axis=1, keepdims=True),
                    alpha * acc + _dot(p.astype(BF16), v))

        m, l, acc = lax.fori_loop(0, i, body, (m, l, acc))
        o_ref[:, hs] = (acc / l).astype(BF16)


def _fox_prompt(pb, s, cum_t, cum_c):
    nq = s // QB_C
    ng = H_C // HEADS_PER_STEP_C
    return pl.pallas_call(
        _fox_kernel,
        grid=(ng, nq),
        in_specs=[
            pl.BlockSpec((QB_C, LANES), lambda g, i: (i, OFF_QC // LANES + g)),
            pl.BlockSpec((s, LANES), lambda g, i: (0, OFF_KC // LANES + g)),
            pl.BlockSpec((s, LANES), lambda g, i: (0, OFF_VC // LANES + g)),
            pl.BlockSpec((16, s), lambda g, i: (0, 0)),
            pl.BlockSpec((QB_C, 16), lambda g, i: (i, 0)),
        ],
        out_specs=pl.BlockSpec((QB_C, LANES), lambda g, i: (i, g)),
        out_shape=jax.ShapeDtypeStruct((s, D_C), BF16),
        compiler_params=_cparams(2),
    )(pb, pb, pb, cum_t, cum_c)


def _softmax_pv(s, v_all):
    m = jnp.max(s, axis=1, keepdims=True)
    p = jnp.exp(s - m)
    l = jnp.sum(p, axis=1, keepdims=True)
    return _dot(p.astype(BF16), v_all) / l


def _with_new_rows(cache, new, pad_rows):
    parts = [cache.astype(BF16), new]
    if pad_rows:
        parts.append(jnp.zeros((pad_rows, new.shape[1]), BF16))
    return jnp.concatenate(parts, axis=0)


def _sample_kernel(qa_ref, ka_ref, va_ref, kvb_ref, qi_ref, qb_ref, qc_ref, kc_ref, vc_ref, misc16_ref,
                   misc32_ref, ca_ref, cb_ref, cbi_ref, cc_ref, clf_ref, clft_ref, lfn_ref, lfnt_ref,
                   bias_a_ref, t5_ref, triu_ref, tril_ref,
                   oa_ref, ob_ref, oc_ref, *, t, past, na, k_top):
    la = na + LANES
    lk = past + LANES
    pad = LANES - t

    ca = ca_ref[...]
    ka_all = _with_new_rows(ca[:, :D_A], ka_ref[...], pad)
    va_all = _with_new_rows(ca[:, D_A:], va_ref[...], pad)
    for h in range(H_A):
        hs = slice(h * HEAD_DIM, (h + 1) * HEAD_DIM)
        s = _dot_t(qa_ref[:, hs], ka_all[:, hs]) * ATTN_SCALE + bias_a_ref[h]
        oa_ref[:, hs] = _softmax_pv(s, va_all[:, hs]).astype(BF16)

    col = lax.broadcasted_iota(I32, (t, lk), 1)
    ki_all = _with_new_rows(cbi_ref[...], misc16_ref[:, :D_IDX], pad)
    wi = misc32_ref[:, MISC_WI:MISC_WI + H_IDX] * (H_IDX ** -0.5 * D_IDX ** -0.5)
    acc = jnp.zeros((t, lk), F32)
    for h in range(H_IDX):
        isc = _dot_t(qi_ref[:, h * D_IDX:(h + 1) * D_IDX], ki_all)
        acc = acc + jnp.maximum(isc, 0.0) * wi[:, h:h + 1]
    keys = jnp.where(col < past + t, _sortable(acc), jnp.int32(NEG_KEY))

    def count(pred):
        return jnp.sum(jnp.where(pred, 1.0, 0.0), axis=1, keepdims=True)

    thr, _ = _kth_largest(lambda cand: count(keys >= cand), t, k_top)
    live = thr > jnp.int32(NEG_KEY)
    thr = jnp.maximum(thr, jnp.int32(NEG_KEY))
    need = float(k_top) - count(keys > thr)
    jstar = _tie_limit(lambda cand: count((keys == thr) & (col < cand)), need, t)
    sel = (keys > thr) | ((keys == thr) & (col < jnp.where(live, jstar, 0)))

    cb = cb_ref[...]
    kb_all = _with_new_rows(cb[:, :D_B_KV], kvb_ref[:, :D_B_KV], pad)
    vb_all = _with_new_rows(cb[:, D_B_KV:], kvb_ref[:, D_B_KV:], pad)
    for n in range(H_B_KV):
        ns = slice(n * HEAD_DIM, (n + 1) * HEAD_DIM)
        for g in range(GQA):
            h = n * GQA + g
            hs = slice(h * HEAD_DIM, (h + 1) * HEAD_DIM)
            s = _dot_t(qb_ref[:, hs], kb_all[:, ns]) * ATTN_SCALE + t5_ref[h]
            s = jnp.where(sel, s, NEG_INF)
            ob_ref[:, hs] = _softmax_pv(s, vb_all[:, ns]).astype(BF16)

    cc = cc_ref[...]
    kc_all = _with_new_rows(cc[:, :D_C], kc_ref[...], pad)
    vc_all = _with_new_rows(cc[:, D_C:], vc_ref[...], pad)
    lft = jnp.concatenate([clft_ref[...], lfnt_ref[...]], axis=1)
    t1, t2, t3 = _split3(lft)
    triu = triu_ref[...]
    cum_t = _dot(t1, triu) + _dot(t2, triu) + _dot(t3, triu)
    total = jnp.sum(clf_ref[...], axis=0, keepdims=True)
    n1, n2, n3 = _split3(lfn_ref[...])
    tril = tril_ref[...]
    cum_q = total + _dot(tril, n1) + _dot(tril, n2) + _dot(tril, n3)
    row = lax.broadcasted_iota(I32, (t, lk), 0)
    causal = col <= past + row
    for h in range(H_C):
        hs = slice(h * HEAD_DIM, (h + 1) * HEAD_DIM)
        s = _dot_t(qc_ref[:, hs], kc_all[:, hs]) * ATTN_SCALE + cum_q[:, h:h + 1] - cum_t[h:h + 1, :]
        s = jnp.where(causal, s, NEG_INF)
        oc_ref[:, hs] = _softmax_pv(s, vc_all[:, hs]).astype(BF16)


def _sample_mixers(p32, pb, s, ca, cb, cbi, cc, clf, clft, lfn, lfnt, bias_a, t5_tab, triu, tril, layer):
    nb, na = ca.shape[1], ca.shape[2]
    past = cb.shape[2]
    t = lfn.shape[1]
    k_top = min(TOPK_MAX, (past + t) // 4)
    r0 = s // t
    lk = past + LANES
    la = na + LANES
    row = lambda width, off: pl.BlockSpec((t, width), lambda b: (r0 + b, off // width))
    cache = lambda rows, width: pl.BlockSpec((None, None, rows, width), lambda b: (layer, b, 0, 0))
    full = lambda shape: pl.BlockSpec(shape, lambda b: (0,) * len(shape))
    in_specs = [
        row(D_A, OFF_QA), row(D_A, OFF_KA), row(D_A, OFF_VA), row(2 * D_B_KV, OFF_KB),
        row(H_IDX * D_IDX, OFF_QI), row(D_B, OFF_QB), row(D_C, OFF_QC), row(D_C, OFF_KC), row(D_C, OFF_VC),
        row(LANES, OFF_MISC), row(LANES, OFF_MISC),
        cache(na, 2 * D_A), cache(past, 2 * D_B_KV), cache(past, D_IDX), cache(past, 2 * D_C),
        cache(past, 16), cache(16, past),
        pl.BlockSpec((None, t, 16), lambda b: (b, 0, 0)),
        pl.BlockSpec((None, 16, LANES), lambda b: (b, 0, 0)),
        full((H_A, t, la)), full((H_B, t, lk)), full((lk, lk)), full((t, t)),
    ]
    out = lambda width: pl.BlockSpec((t, width), lambda b: (b, 0))
    return pl.pallas_call(
        functools.partial(_sample_kernel, t=t, past=past, na=na, k_top=k_top),
        grid=(nb,),
        in_specs=in_specs,
        out_specs=[out(D_A), out(D_B), out(D_C)],
        out_shape=[jax.ShapeDtypeStruct((nb * t, w), BF16) for w in (D_A, D_B, D_C)],
        compiler_params=_cparams(1),
    )(pb, pb, pb, pb, pb, pb, pb, pb, pb, pb, p32, ca, cb, cbi, cc, clf, clft, lfn, lfnt,
      bias_a, t5_tab, triu, tril)


def _outproj_kernel(x_ref, g_ref, ap, bp, cp, as_, bs, cs, w_ref, o_ref, *, n_prompt_blocks):
    i = pl.program_id(0)

    def run(a, b, c):
        y = (_dot(a[...], w_ref[:D_A, :]) + _dot(b[...], w_ref[D_A:D_A + D_B, :])
             + _dot(c[...], w_ref[D_A + D_B:, :]))
        o_ref[...] = x_ref[...] + g_ref[...] * y

    pl.when(i < n_prompt_blocks)(lambda: run(ap, bp, cp))
    pl.when(i >= n_prompt_blocks)(lambda: run(as_, bs, cs))


def _out_project(x, gate, mix_p, mix_s, w_out_b, layer, n_prompt_blocks):
    m = x.shape[0]
    tm = TOKEN_TILE
    last_p = n_prompt_blocks - 1
    p_map = lambda i: (jnp.minimum(i, last_p), 0)
    s_map = lambda i: (jnp.maximum(i - n_prompt_blocks, 0), 0)
    widths = (D_A, D_B, D_C)
    return pl.pallas_call(
        functools.partial(_outproj_kernel, n_prompt_blocks=n_prompt_blocks),
        grid=(m // tm,),
        in_specs=[
            pl.BlockSpec((tm, D_MODEL), lambda i: (i, 0)),
            pl.BlockSpec((tm, D_MODEL), lambda i: (_mod_block_index(i, n_prompt_blocks), 0)),
            *[pl.BlockSpec((tm, w), p_map) for w in widths],
            *[pl.BlockSpec((tm, w), s_map) for w in widths],
            pl.BlockSpec((None, D_MODEL, D_MODEL), lambda i: (layer, 0, 0)),
        ],
        out_specs=pl.BlockSpec((tm, D_MODEL), lambda i: (i, 0)),
        out_shape=jax.ShapeDtypeStruct((m, D_MODEL), F32),
        compiler_params=_cparams(1),
    )(x, gate, *mix_p, *mix_s, w_out_b)


def _lane_pick(vals, lane, idx):
    return jnp.sum(jnp.where(lane == idx, vals, 0.0), axis=1, keepdims=True)


def _first_argmax(vals, lane):
    m = jnp.max(vals, axis=1, keepdims=True)
    idx = jnp.min(jnp.where(vals == m, lane, float(LANES)), axis=1, keepdims=True)
    return m, idx


def _router_kernel(x_ref, gain_ref, sc_ref, sh_ref, wr_ref, br_ref, tri_ref, h_ref, meta_ref, cnt_ref, carry):
    @pl.when(pl.program_id(0) == 0)
    def _():
        carry[...] = jnp.zeros_like(carry)

    h = _norm_mod(x_ref[...], gain_ref[...], sc_ref[...], sh_ref[...])
    hb = h.astype(BF16)
    h_ref[...] = hb
    scores = jax.nn.sigmoid(_dot(hb, wr_ref[...]))
    lane_i = lax.broadcasted_iota(I32, scores.shape, 1)
    lane = lane_i.astype(F32)
    sel = jnp.where(lane_i < N_EXPERTS, scores + br_ref[...], NEG_INF)
    group = (lane_i // (N_EXPERTS // N_GROUPS)).astype(F32)

    best = None
    for g in range(N_GROUPS):
        in_g = jnp.where(group == float(g), sel, NEG_INF)
        m1, i1 = _first_argmax(in_g, lane)
        m2 = jnp.max(jnp.where(lane == i1, NEG_INF, in_g), axis=1, keepdims=True)
        gs = m1 + m2
        if best is None:
            best, gbest = gs, jnp.zeros_like(i1)
        else:
            better = gs > best
            best = jnp.where(better, gs, best)
            gbest = jnp.where(better, float(g), gbest)

    in_best = jnp.where(group == gbest, sel, NEG_INF)
    _, e0 = _first_argmax(in_best, lane)
    _, e1 = _first_argmax(jnp.where(lane == e0, NEG_INF, in_best), lane)
    w0 = _lane_pick(scores, lane, e0)
    w1 = _lane_pick(scores, lane, e1)
    wsum = w0 + w1

    onehot = jnp.where((lane == e0) | (lane == e1), 1.0, 0.0)
    before = _dot(tri_ref[...], onehot.astype(BF16)) + carry[...]
    r0 = _lane_pick(before, lane, e0)
    r1 = _lane_pick(before, lane, e1)
    carry[...] = carry[...] + jnp.sum(onehot, axis=0, keepdims=True)
    cnt_ref[...] = carry[...]

    meta = jnp.zeros(scores.shape, F32)
    for k, v in enumerate((e0, e1, r0, r1, w0 / wsum, w1 / wsum)):
        meta = jnp.where(lane_i == k, v, meta)
    meta_ref[...] = meta


def _router(x, gain, sc, sh, w_router_p, b_router_p, tri, n_prompt_blocks):
    m = x.shape[0]
    tm = TOKEN_TILE
    mod_map = lambda i: (_mod_block_index(i, n_prompt_blocks), 0)
    return pl.pallas_call(
        _router_kernel,
        grid=(m // tm,),
        in_specs=[
            pl.BlockSpec((tm, D_MODEL), lambda i: (i, 0)),
            pl.BlockSpec((1, D_MODEL), lambda i: (0, 0)),
            pl.BlockSpec((tm, D_MODEL), mod_map),
            pl.BlockSpec((tm, D_MODEL), mod_map),
            pl.BlockSpec((D_MODEL, LANES), lambda i: (0, 0)),
            pl.BlockSpec((1, LANES), lambda i: (0, 0)),
            pl.BlockSpec((tm, tm), lambda i: (0, 0)),
        ],
        out_specs=[
            pl.BlockSpec((tm, D_MODEL), lambda i: (i, 0)),
            pl.BlockSpec((tm, LANES), lambda i: (i, 0)),
            pl.BlockSpec((1, LANES), lambda i: (0, 0)),
        ],
        out_shape=[
            jax.ShapeDtypeStruct((m, D_MODEL), BF16),
            jax.ShapeDtypeStruct((m, LANES), F32),
            jax.ShapeDtypeStruct((1, LANES), F32),
        ],
        scratch_shapes=[pltpu.VMEM((1, LANES), F32)],
        compiler_params=_cparams(1),
    )(x, gain, sc, sh, w_router_p, b_router_p, tri)


EXPERT_ROWS = 256
EXPERT_TF = 512


def _expert_kernel(be_ref, nu_ref, x_ref, wg_ref, wu_ref, wd_ref, o_ref):
    b = pl.program_id(0)
    f = pl.program_id(1)

    @pl.when(b < nu_ref[0])
    def _():
        x = x_ref[...]
        a = _dot(x, wg_ref[...].astype(BF16))
        u = _dot(x, wu_ref[...].astype(BF16))
        act = (a * jax.nn.sigmoid(a) * u).astype(BF16)
        y = _dot(act, wd_ref[...].astype(BF16))

        @pl.when(f == 0)
        def _():
            o_ref[...] = y

        @pl.when(f > 0)
        def _():
            o_ref[...] += y


def _experts(xb, block_e, n_used, w_g, w_u, w_d, layer):
    nslots = xb.shape[0]
    nb = nslots // EXPERT_ROWS
    nf = D_EXPERT // EXPERT_TF

    def blk(b, nu):
        return jnp.minimum(b, nu[0] - 1)

    def fidx(b, f, nu):
        return jnp.where(b < nu[0], f, nf - 1)

    grid_spec = pltpu.PrefetchScalarGridSpec(
        num_scalar_prefetch=2,
        grid=(nb, nf),
        in_specs=[
            pl.BlockSpec((EXPERT_ROWS, D_MODEL), lambda b, f, be, nu: (blk(b, nu), 0)),
            pl.BlockSpec((None, None, D_MODEL, EXPERT_TF),
                         lambda b, f, be, nu: (layer, be[blk(b, nu)], 0, fidx(b, f, nu))),
            pl.BlockSpec((None, None, D_MODEL, EXPERT_TF),
                         lambda b, f, be, nu: (layer, be[blk(b, nu)], 0, fidx(b, f, nu))),
            pl.BlockSpec((None, None, EXPERT_TF, D_MODEL),
                         lambda b, f, be, nu: (layer, be[blk(b, nu)], fidx(b, f, nu), 0)),
        ],
        out_specs=pl.BlockSpec((EXPERT_ROWS, D_MODEL), lambda b, f, be, nu: (blk(b, nu), 0)),
    )
    return pl.pallas_call(
        _expert_kernel,
        grid_spec=grid_spec,
        out_shape=jax.ShapeDtypeStruct((nslots, D_MODEL), F32),
        compiler_params=_cparams(2),
    )(block_e, n_used, xb, w_g, w_u, w_d)


def _t5_bucket(rel):
    nb = T5_BUCKETS // 2
    max_exact = nb // 2
    ret = jnp.where(rel < 0, nb, 0)
    n = jnp.abs(rel)
    nf = jnp.maximum(n, 1).astype(F32)
    large = max_exact + (jnp.log(nf / max_exact) / math.log(T5_MAX_DIST / max_exact) * (nb - max_exact)).astype(I32)
    large = jnp.minimum(large, nb - 1)
    return ret + jnp.where(n < max_exact, n, large)


def _band_bias(table, qpos, kpos, valid):
    rel = qpos[:, None] - kpos[None, :]
    bias = table.astype(F32)[:, jnp.clip(rel, -A_REL_FUTURE, A_REL_PAST) + A_REL_FUTURE]
    qc = qpos[:, None] // CHUNK
    kc = kpos[None, :] // CHUNK
    ok = valid[None, :] & (kc <= qc) & (kc >= qc - A_LEFT_CHUNKS)
    return jnp.where(ok[None], bias, NEG_INF)


def _t5_bias(t5, qpos, kpos):
    return jnp.moveaxis(t5.astype(F32)[_t5_bucket(qpos[:, None] - kpos[None, :])], 2, 0)


def _relayout_w_in(w_in_l):
    qa, ka, va, qb, kb, vb, qi, ki, wi, qc, kc, vc, fc = jnp.split(w_in_l, PROJ_SPLITS, axis=1)
    pad = jnp.zeros((D_MODEL, P_COLS - OFF_MISC - D_IDX - H_IDX - H_C), w_in_l.dtype)
    return jnp.concatenate([qa, ka, va, kb, vb, qi, qb, qc, kc, vc, ki, wi, fc, pad], axis=1).astype(BF16)


def _column_params(qk_gain_l, b_forget_l):
    ones = lambda n: jnp.ones((n,), F32)
    g = qk_gain_l.astype(F32)
    gq = jnp.concatenate([
        jnp.tile(g[0], H_A), jnp.tile(g[1], H_A), ones(D_A),
        jnp.tile(g[3], H_B_KV), ones(D_B_KV), ones(H_IDX * D_IDX),
        jnp.tile(g[2], H_B), jnp.tile(g[4], H_C), jnp.tile(g[5], H_C), ones(D_C), ones(P_COLS - OFF_MISC)])
    bq = jnp.zeros((P_COLS,), F32).at[OFF_MISC + MISC_FC:OFF_MISC + MISC_FC + H_C].set(b_forget_l.astype(F32))
    return gq[None, :], bq[None, :]


def _tri(n, *, strict=False, upper=False):
    r = jnp.arange(n)[:, None]
    c = jnp.arange(n)[None, :]
    m = (r < c if strict else r <= c) if upper else (c < r if strict else c <= r)
    return m.astype(BF16)


def kernel(x_prompt, x_sample, c_prompt, c_sample, cache_a_kv, cache_b_kv, cache_b_kidx, cache_c_kv, cache_c_logf,
           w_ada, b_ada, norm_gain, w_in, b_forget, qk_gain, rel_bias_a, t5_bias, w_out, w_router, b_router,
           w_e_gate, w_e_up, w_e_down):
    depth = w_in.shape[0]
    bp, s, d = x_prompt.shape
    nb, t, _ = x_sample.shape
    ns = nb * t
    tm = TOKEN_TILE
    assert bp == 1 and d == D_MODEL and s % tm == 0 and ns % tm == 0 and s % QB_C == 0
    n_pb = s // tm
    m = s + ns
    na = cache_a_kv.shape[2]
    past = cache_b_kv.shape[2]
    keep = min(A_LEFT_CHUNKS * CHUNK, s)

    x = jnp.concatenate([x_prompt.reshape(s, d), x_sample.reshape(ns, d)], axis=0)
    c_all = jnp.concatenate([c_prompt, c_sample, jnp.zeros((-(bp + nb) % 8, d), F32)], axis=0)

    q128 = jnp.arange(QB_A)
    band_k = jnp.arange(A_KEY_BLOCKS * QB_A) - (A_KEY_BLOCKS - 1) * QB_A
    qpos_s = past + jnp.arange(t)
    kpos_a = jnp.concatenate([jnp.arange(past - na, past), qpos_s, jnp.zeros((LANES - t,), I32)])
    valid_a = jnp.arange(na + LANES) < na + t
    kpos_b = jnp.concatenate([jnp.arange(past), qpos_s, jnp.zeros((LANES - t,), I32)])
    t5_near = _t5_bias(t5_bias, q128 + KB_B, jnp.arange(2 * KB_B))
    t5_far = t5_bias.astype(F32)[T5_BUCKETS // 2 - 1]
    t5_s = _t5_bias(t5_bias, qpos_s, kpos_b)
    bd = ((jnp.arange(PROJ_TN)[:, None] // HEAD_DIM == jnp.arange(PROJ_TN)[None, :] // HEAD_DIM)
          .astype(F32) / HEAD_DIM).astype(BF16)
    tri_cum = _tri(CUM_TILE)
    tri_rank = _tri(tm, strict=True)
    triu_s = _tri(past + LANES, upper=True)
    tril_s = _tri(t)
    w_out_b = w_out.astype(BF16)
    w_router_p = jnp.pad(w_router, ((0, 0), (0, LANES - N_EXPERTS))).astype(BF16)
    b_router_p = jnp.pad(b_router.astype(F32), (0, LANES - N_EXPERTS))[None, :]

    ca = cache_a_kv.reshape(depth, nb, na, 2 * D_A)
    cb = cache_b_kv.reshape(depth, nb, past, 2 * D_B_KV)
    cc = cache_c_kv.reshape(depth, nb, past, 2 * D_C)
    clf = jnp.pad(cache_c_logf.astype(F32), ((0, 0), (0, 0), (0, 0), (0, 16 - H_C)))
    clft = jnp.swapaxes(clf, 2, 3)

    n_asg = 2 * m
    n_eb = n_asg // EXPERT_ROWS + N_EXPERTS
    tok = jnp.arange(m, dtype=I32)

    states_p, states_s = [], []
    for l in range(depth):
        mod = _ada(c_all, w_ada, b_ada, l)
        mods = []
        for part in jnp.split(mod, 6, axis=1):
            mods.append(jnp.concatenate([jnp.broadcast_to(part[:1], (tm, d)), jnp.repeat(part[bp:bp + nb], t, axis=0)], 0))
        sh1, sc1, g1, sh2, sc2, g2 = mods

        gq, bq = _column_params(qk_gain[l], b_forget[l])
        p32, pb = _project(x, norm_gain[l, 0][None, :], sc1, sh1, _relayout_w_in(w_in[l]), gq, bq, bd, n_pb)

        bias_a_p = _band_bias(rel_bias_a[l], q128 + (A_KEY_BLOCKS - 1) * QB_A, band_k + (A_KEY_BLOCKS - 1) * QB_A,
                              jnp.ones((A_KEY_BLOCKS * QB_A,), bool))
        oa_p = _band_prompt(pb, s, bias_a_p)
        ob_p = _dsa_prompt(p32, pb, s, t5_near, t5_far)
        cum = _cumsum_rows(p32, s, tri_cum)[:, MISC_FC:MISC_FC + 16]
        oc_p = _fox_prompt(pb, s, cum.T, cum)

        lfn = p32[s:, OFF_MISC + MISC_FC:OFF_MISC + MISC_FC + 16].reshape(nb, t, 16)
        lfnt = jnp.pad(jnp.swapaxes(lfn, 1, 2), ((0, 0), (0, 0), (0, LANES - t)))
        bias_a_s = _band_bias(rel_bias_a[l], qpos_s, kpos_a, valid_a)
        oa_s, ob_s, oc_s = _sample_mixers(p32, pb, s, ca, cb, cache_b_kidx, cc, clf, clft, lfn, lfnt,
                                          bias_a_s, t5_s, triu_s, tril_s, l)

        x = _out_project(x, g1, (oa_p, ob_p, oc_p), (oa_s, ob_s, oc_s), w_out_b, l, n_pb)

        h2, meta, counts = _router(x, norm_gain[l, 1][None, :], sc2, sh2, w_router_p, b_router_p, tri_rank, n_pb)
        e0 = meta[:, 0].astype(I32)
        e1 = meta[:, 1].astype(I32)
        counts = counts[0, :N_EXPERTS].astype(I32)
        padded = (counts + EXPERT_ROWS - 1) // EXPERT_ROWS * EXPERT_ROWS
        pend = jnp.cumsum(padded)
        pstart = pend - padded
        dest0 = pstart[e0] + meta[:, 2].astype(I32)
        dest1 = pstart[e1] + meta[:, 3].astype(I32)
        slot_tok = jnp.full((n_eb * EXPERT_ROWS,), m, I32).at[dest0].set(tok).at[dest1].set(tok)
        block_e = jnp.minimum(jnp.searchsorted(pend, jnp.arange(n_eb, dtype=I32) * EXPERT_ROWS, side='right'),
                              N_EXPERTS - 1).astype(I32)
        n_used = (pend[-1:] // EXPERT_ROWS).astype(I32)
        xb = jnp.concatenate([h2, jnp.zeros((1, d), BF16)], axis=0)[slot_tok]
        yb = _experts(xb, block_e, n_used, w_e_gate, w_e_up, w_e_down, l)
        moe_out = yb[dest0] * meta[:, 4:5] + yb[dest1] * meta[:, 5:6]
        x = x + jnp.concatenate([jnp.broadcast_to(g2[:1], (s, d)), g2[tm:]], axis=0) * moe_out

        def states(rows, nbatch, a_rows):
            r = p32[rows]
            n = r.shape[0] // nbatch
            kv = lambda off, heads: r[:, off:off + 2 * heads * HEAD_DIM].reshape(nbatch, n, 2, heads, HEAD_DIM)
            return (kv(OFF_KA, H_A)[:, n - a_rows:], kv(OFF_KB, H_B_KV),
                    r[:, OFF_MISC:OFF_MISC + D_IDX].reshape(nbatch, n, D_IDX), kv(OFF_KC, H_C),
                    r[:, OFF_MISC + MISC_FC:OFF_MISC + MISC_FC + H_C].reshape(nbatch, n, H_C))

        states_p.append(states(slice(0, s), bp, keep))
        states_s.append(states(slice(s, m), nb, t))

    stk = lambda sts, i: jnp.stack([st[i] for st in sts], axis=0)
    return (x[:s].reshape(bp, s, d), x[s:].reshape(nb, t, d),
            *[stk(states_p, i) for i in range(5)], *[stk(states_s, i) for i in range(5)])
```

```python
import functools
import math

import jax
import jax.numpy as jnp
from jax import lax
from jax.experimental import pallas as pl
from jax.experimental.pallas import tpu as pltpu

F32 = jnp.float32
BF16 = jnp.bfloat16
I32 = jnp.int32

D_MODEL = 2048
HEAD_DIM = 64
CHUNK = 64
EPS = 1e-6
ATTN_SCALE = HEAD_DIM ** -0.5
H_A = 8
A_LEFT_CHUNKS = 8
A_REL_PAST = 128
A_REL_FUTURE = CHUNK - 1
H_B = 12
H_B_KV = 4
GQA = H_B // H_B_KV
H_IDX = 16
D_IDX = 64
TOPK_MAX = 256
H_C = 12
T5_BUCKETS = 32
T5_MAX_DIST = 128
N_EXPERTS = 32
N_GROUPS = 4
D_EXPERT = 1024

D_A = H_A * HEAD_DIM
D_B = H_B * HEAD_DIM
D_B_KV = H_B_KV * HEAD_DIM
D_C = H_C * HEAD_DIM
PROJ_SIZES = (D_A, D_A, D_A, D_B, D_B_KV, D_B_KV, H_IDX * D_IDX, D_IDX, H_IDX, D_C, D_C, D_C, H_C)
PROJ_SPLITS = tuple(sum(PROJ_SIZES[:i + 1]) for i in range(len(PROJ_SIZES) - 1))

OFF_QA, OFF_KA, OFF_VA = 0, 512, 1024
OFF_KB, OFF_VB = 1536, 1792
OFF_QI = 2048
OFF_QB = 3072
OFF_QC, OFF_KC, OFF_VC = 3840, 4608, 5376
OFF_MISC = 6144
MISC_WI = 64
MISC_FC = 80
P_COLS = 6400
PROJ_TN = 256
NORM_COL_BLOCKS = (0, 1, 2, 3, 6, 12, 13, 14, 15, 16, 17, 18, 19, 20)
MISC_COL_BLOCK = OFF_MISC // PROJ_TN

TOKEN_TILE = 512
LANES = 128
VMEM_LIMIT = 56 * 1024 * 1024

NEG_INF = float("-inf")
NEG_KEY = (0xFF800000 ^ 0x7FFFFFFF) - (1 << 32)
IDX_BIG = 1 << 30


def _cparams(n_axes):
    return pltpu.CompilerParams(dimension_semantics=("arbitrary",) * n_axes, vmem_limit_bytes=VMEM_LIMIT)


def _split3(x):
    x1 = x.astype(BF16)
    r1 = x - x1.astype(F32)
    x2 = r1.astype(BF16)
    r2 = r1 - x2.astype(F32)
    return x1, x2, r2.astype(BF16)


def _dot(a, b):
    return jnp.dot(a, b, preferred_element_type=F32)


def _dot_t(a, b):
    return lax.dot_general(a, b, (((1,), (1,)), ((), ())), preferred_element_type=F32)


def _sortable(x):
    b = lax.bitcast_convert_type(x, I32)
    return jnp.where(b < 0, b ^ jnp.int32(0x7FFFFFFF), b)


def _ada_kernel(c_ref, w_ref, b_ref, o_ref):
    c = c_ref[...]
    a = (c * jax.nn.sigmoid(c)).astype(BF16)
    o_ref[...] = _dot(a, w_ref[...].astype(BF16)) + b_ref[...]


def _ada(c_all, w_ada, b_ada, layer):
    rows = c_all.shape[0]
    n_out = w_ada.shape[2]
    tn = 1024
    return pl.pallas_call(
        _ada_kernel,
        grid=(n_out // tn,),
        in_specs=[
            pl.BlockSpec((rows, D_MODEL), lambda n: (0, 0)),
            pl.BlockSpec((None, D_MODEL, tn), lambda n: (layer, 0, n)),
            pl.BlockSpec((None, 1, tn), lambda n: (layer, 0, n)),
        ],
        out_specs=pl.BlockSpec((rows, tn), lambda n: (0, n)),
        out_shape=jax.ShapeDtypeStruct((rows, n_out), F32),
        compiler_params=_cparams(1),
    )(c_all, w_ada, b_ada.reshape(b_ada.shape[0], 1, n_out))


def _mod_block_index(i, n_prompt_blocks):
    return jnp.where(i < n_prompt_blocks, 0, i - n_prompt_blocks + 1)


def _norm_mod(x, gain, sc, sh):
    ms = jnp.mean(x * x, axis=-1, keepdims=True)
    return (x * lax.rsqrt(ms + EPS) * gain) * (1.0 + sc) + sh


def _proj_kernel(x_ref, gain_ref, sc_ref, sh_ref, w_ref, gq_ref, bq_ref, bd_ref, o32_ref, o16_ref, h_scr):
    n = pl.program_id(1)

    @pl.when(n == 0)
    def _():
        h_scr[...] = _norm_mod(x_ref[...], gain_ref[...], sc_ref[...], sh_ref[...]).astype(BF16)

    y = _dot(h_scr[...], w_ref[...])

    is_norm = functools.reduce(jnp.logical_or, [n == b for b in NORM_COL_BLOCKS])
    is_misc = n == MISC_COL_BLOCK

    def emit(v):
        o32_ref[...] = v
        o16_ref[...] = v.astype(BF16)

    @pl.when(is_norm)
    def _():
        s1, s2, s3 = _split3(y * y)
        bd = bd_ref[...]
        ms = _dot(s1, bd) + _dot(s2, bd) + _dot(s3, bd)
        emit(y * lax.rsqrt(ms + EPS) * gq_ref[...])

    @pl.when(is_misc)
    def _():
        lane = lax.broadcasted_iota(I32, y.shape, 1)
        z = y + bq_ref[...]
        logsig = jnp.minimum(z, 0.0) - jnp.log(1.0 + jnp.exp(-jnp.abs(z)))
        emit(jnp.where((lane >= MISC_FC) & (lane < MISC_FC + H_C), logsig, y))

    @pl.when(jnp.logical_not(jnp.logical_or(is_norm, is_misc)))
    def _():
        emit(y)


def _project(x, gain, sc, sh, w_r, gq, bq, bd, n_prompt_blocks):
    m = x.shape[0]
    tm = TOKEN_TILE
    mod_map = lambda i, n: (_mod_block_index(i, n_prompt_blocks), 0)
    return pl.pallas_call(
        _proj_kernel,
        grid=(m // tm, P_COLS // PROJ_TN),
        in_specs=[
            pl.BlockSpec((tm, D_MODEL), lambda i, n: (i, 0)),
            pl.BlockSpec((1, D_MODEL), lambda i, n: (0, 0)),
            pl.BlockSpec((tm, D_MODEL), mod_map),
            pl.BlockSpec((tm, D_MODEL), mod_map),
            pl.BlockSpec((D_MODEL, PROJ_TN), lambda i, n: (0, n)),
            pl.BlockSpec((1, PROJ_TN), lambda i, n: (0, n)),
            pl.BlockSpec((1, PROJ_TN), lambda i, n: (0, n)),
            pl.BlockSpec((PROJ_TN, PROJ_TN), lambda i, n: (0, 0)),
        ],
        out_specs=[
            pl.BlockSpec((tm, PROJ_TN), lambda i, n: (i, n)),
            pl.BlockSpec((tm, PROJ_TN), lambda i, n: (i, n)),
        ],
        out_shape=[jax.ShapeDtypeStruct((m, P_COLS), F32), jax.ShapeDtypeStruct((m, P_COLS), BF16)],
        scratch_shapes=[pltpu.VMEM((tm, D_MODEL), BF16)],
        compiler_params=_cparams(2),
    )(x, gain, sc, sh, w_r, gq, bq, bd)


CUM_TILE = 256


def _cumsum_kernel(x_ref, tri_ref, o_ref, carry):
    @pl.when(pl.program_id(0) == 0)
    def _():
        carry[...] = jnp.zeros_like(carry)

    x1, x2, x3 = _split3(x_ref[...])
    tri = tri_ref[...]
    c = _dot(tri, x1) + _dot(tri, x2) + _dot(tri, x3) + carry[...]
    o_ref[...] = c
    carry[...] = c[CUM_TILE - 1:CUM_TILE, :]


def _cumsum_rows(p32, s, tri):
    return pl.pallas_call(
        _cumsum_kernel,
        grid=(s // CUM_TILE,),
        in_specs=[
            pl.BlockSpec((CUM_TILE, LANES), lambda i: (i, OFF_MISC // LANES)),
            pl.BlockSpec((CUM_TILE, CUM_TILE), lambda i: (0, 0)),
        ],
        out_specs=pl.BlockSpec((CUM_TILE, LANES), lambda i: (i, 0)),
        out_shape=jax.ShapeDtypeStruct((s, LANES), F32),
        scratch_shapes=[pltpu.VMEM((1, LANES), F32)],
        compiler_params=_cparams(1),
    )(p32, tri)


QB_A = 128
A_KEY_BLOCKS = 5


def _band_kernel(q_ref, *refs):
    k_refs = refs[:A_KEY_BLOCKS]
    v_refs = refs[A_KEY_BLOCKS:2 * A_KEY_BLOCKS]
    bias_ref, o_ref = refs[2 * A_KEY_BLOCKS:]
    i = pl.program_id(0)
    for h in range(H_A):
        hs = slice(h * HEAD_DIM, (h + 1) * HEAD_DIM)
        q = q_ref[:, hs]
        parts = []
        for j in range(A_KEY_BLOCKS):
            s = _dot_t(q, k_refs[j][:, hs]) * ATTN_SCALE + bias_ref[h, :, j * QB_A:(j + 1) * QB_A]
            in_range = i - (A_KEY_BLOCKS - 1) + j >= 0
            parts.append(jnp.where(in_range, s, NEG_INF))
        m = functools.reduce(jnp.maximum, [jnp.max(s, axis=1, keepdims=True) for s in parts])
        l = jnp.zeros((QB_A, 1), F32)
        o = jnp.zeros((QB_A, HEAD_DIM), F32)
        for j in range(A_KEY_BLOCKS):
            p = jnp.exp(parts[j] - m)
            l = l + jnp.sum(p, axis=1, keepdims=True)
            o = o + _dot(p.astype(BF16), v_refs[j][:, hs])
        o_ref[:, hs] = (o / l).astype(BF16)


def _band_prompt(pb, s, bias_a):
    nq = s // QB_A
    back = A_KEY_BLOCKS - 1

    def kv_spec(j, col):
        return pl.BlockSpec((QB_A, D_A), lambda i: (jnp.maximum(i - back + j, 0), col))

    in_specs = [pl.BlockSpec((QB_A, D_A), lambda i: (i, OFF_QA // D_A))]
    in_specs += [kv_spec(j, OFF_KA // D_A) for j in range(A_KEY_BLOCKS)]
    in_specs += [kv_spec(j, OFF_VA // D_A) for j in range(A_KEY_BLOCKS)]
    in_specs += [pl.BlockSpec((H_A, QB_A, A_KEY_BLOCKS * QB_A), lambda i: (0, 0, 0))]
    return pl.pallas_call(
        _band_kernel,
        grid=(nq,),
        in_specs=in_specs,
        out_specs=pl.BlockSpec((QB_A, D_A), lambda i: (i, 0)),
        out_shape=jax.ShapeDtypeStruct((s, D_A), BF16),
        compiler_params=_cparams(1),
    )(pb, *([pb] * (2 * A_KEY_BLOCKS)), bias_a)


def _kth_largest(count_ge, rows, k):
    def body(it, carry):
        lo, cnt_lo = carry
        cand = lo + lax.shift_left(jnp.int32(1), jnp.int32(31) - it)
        c = count_ge(cand)
        keep = c >= float(k)
        return jnp.where(keep, cand, lo), jnp.where(keep, c, cnt_lo)

    lo0 = jnp.full((rows, 1), -(1 << 31), I32)
    cnt0 = count_ge(lo0)
    return lax.fori_loop(0, 32, body, (lo0, cnt0))


def _tie_limit(count_eq_below, need, rows):
    def body(it, j):
        cand = j + lax.shift_left(jnp.int32(1), jnp.int32(14) - it)
        return jnp.where(count_eq_below(cand) <= need, cand, j)

    return lax.fori_loop(0, 15, body, jnp.zeros((rows, 1), I32))


QB_B = 128
KB_B = 256


def _dsa_kernel(qb_ref, qi_ref, misc_ref, kidx_ref, kb_ref, vb_ref, tb_ref, o_ref,
                key_scr, w_scr, thr_scr, m_scr, acc_scr, *, k_top):
    i = pl.program_id(0)
    n_kb = (i * QB_B) // KB_B + 1
    row = lax.broadcasted_iota(I32, (QB_B, KB_B), 0)
    col = lax.broadcasted_iota(I32, (QB_B, KB_B), 1)

    wi = misc_ref[:, MISC_WI:MISC_WI + H_IDX] * (H_IDX ** -0.5 * D_IDX ** -0.5)
    for h in range(H_IDX):
        w_scr[h] = jnp.broadcast_to(wi[:, h:h + 1], (QB_B, KB_B))

    def score_block(j, _):
        kblk = kidx_ref[pl.ds(pl.multiple_of(j * KB_B, KB_B), KB_B), :][:, :D_IDX]
        acc = jnp.zeros((QB_B, KB_B), F32)
        for h in range(H_IDX):
            isc = _dot_t(qi_ref[:, h * D_IDX:(h + 1) * D_IDX], kblk)
            acc = acc + jnp.maximum(isc, 0.0) * w_scr[h]
        admissible = (j * KB_B + col) // CHUNK <= (i * QB_B + row) // CHUNK
        key_scr[:, pl.ds(pl.multiple_of(j * KB_B, KB_B), KB_B)] = jnp.where(
            admissible, _sortable(acc), jnp.int32(NEG_KEY))
        return 0

    lax.fori_loop(0, n_kb, score_block, 0)

    def count_where(pred):
        def body(j, c):
            blk = key_scr[:, pl.ds(pl.multiple_of(j * KB_B, KB_B), KB_B)]
            return c + jnp.where(pred(blk, j * KB_B + col), 1.0, 0.0)
        c = lax.fori_loop(0, n_kb, body, jnp.zeros((QB_B, KB_B), F32))
        return jnp.sum(c, axis=1, keepdims=True)

    thr, cnt_ge = _kth_largest(lambda cand: count_where(lambda blk, idx: blk >= cand), QB_B, k_top)
    live = thr > jnp.int32(NEG_KEY)
    thr = jnp.maximum(thr, jnp.int32(NEG_KEY))
    thr_scr[0] = jnp.broadcast_to(thr, (QB_B, KB_B))
    thr_scr[1] = jnp.broadcast_to(jnp.where(live, jnp.int32(IDX_BIG), 0), (QB_B, KB_B))
    has_ties = jnp.max(jnp.where(live & (cnt_ge > float(k_top)), 1.0, 0.0)) > 0.0

    @pl.when(has_ties)
    def _():
        cnt_gt = count_where(lambda blk, idx: blk > thr)
        need = float(k_top) - cnt_gt
        jstar = _tie_limit(lambda cand: count_where(lambda blk, idx: (blk == thr) & (idx < cand)), need, QB_B)
        thr_scr[1] = jnp.broadcast_to(jnp.where(live, jstar, 0), (QB_B, KB_B))

    def mask_block(j, _):
        start = pl.multiple_of(j * KB_B, KB_B)
        keys = key_scr[:, pl.ds(start, KB_B)]
        thr_b = thr_scr[0]
        sel = (keys > thr_b) | ((keys == thr_b) & (j * KB_B + col < thr_scr[1]))
        key_scr[:, pl.ds(start, KB_B)] = lax.bitcast_convert_type(jnp.where(sel, 0.0, NEG_INF), I32)
        return 0

    lax.fori_loop(0, n_kb, mask_block, 0)

    odd = i % 2
    table_a = jnp.where(odd == 1, 3, 1)
    table_b = jnp.where(odd == 1, 0, 2)
    n_far = jnp.maximum(n_kb - 2, 0)
    ones = jnp.ones((KB_B, HEAD_DIM), BF16)
    groups = range(H_B_KV)
    heads = [[n * GQA + g for g in range(GQA)] for n in groups]
    q3 = [jnp.concatenate([qb_ref[:, h * HEAD_DIM:(h + 1) * HEAD_DIM] for h in heads[n]], axis=0) * ATTN_SCALE
          for n in groups]

    def scores(n, j, table):
        start = pl.multiple_of(j * KB_B, KB_B)
        negm = lax.bitcast_convert_type(key_scr[:, pl.ds(start, KB_B)], F32)
        k_n = kb_ref[pl.ds(start, KB_B), n * HEAD_DIM:(n + 1) * HEAD_DIM]
        s = _dot_t(q3[n], k_n).reshape(GQA, QB_B, KB_B) + negm[None]
        if table is not None:
            s = s + jnp.stack([tb_ref[table * H_B + h] for h in heads[n]])
        return s

    def max_block(j, table):
        for n in groups:
            s = scores(n, j, table)
            m_scr[n] = jnp.maximum(m_scr[n], jnp.maximum(s[:, :, :LANES], s[:, :, LANES:]))

    m_scr[...] = jnp.full(m_scr.shape, -1e30, F32)
    lax.fori_loop(0, n_far, lambda j, c: (max_block(j, None), c)[1], 0)
    pl.when(n_kb >= 2)(lambda: max_block(n_kb - 2, table_a))
    max_block(n_kb - 1, table_b)
    m_b = [jnp.broadcast_to(jnp.max(m_scr[n], axis=2, keepdims=True), (GQA, QB_B, KB_B)) for n in groups]

    def pv_block(j, table):
        start = pl.multiple_of(j * KB_B, KB_B)
        for n in groups:
            p = jnp.exp(scores(n, j, table) - m_b[n]).astype(BF16).reshape(GQA * QB_B, KB_B)
            v_n = vb_ref[pl.ds(start, KB_B), n * HEAD_DIM:(n + 1) * HEAD_DIM]
            acc_scr[n] += _dot(p, jnp.concatenate([v_n, ones], axis=1))

    acc_scr[...] = jnp.zeros(acc_scr.shape, F32)
    lax.fori_loop(0, n_far, lambda j, c: (pv_block(j, None), c)[1], 0)
    pl.when(n_kb >= 2)(lambda: pv_block(n_kb - 2, table_a))
    pv_block(n_kb - 1, table_b)
    for n in groups:
        for g, h in enumerate(heads[n]):
            a = acc_scr[n, g * QB_B:(g + 1) * QB_B, :]
            o_ref[:, h * HEAD_DIM:(h + 1) * HEAD_DIM] = (a[:, :HEAD_DIM] / a[:, HEAD_DIM:]).astype(BF16)


def _dsa_prompt(p32, pb, s, t5_tables):
    nq = s // QB_B
    k_top = min(TOPK_MAX, s // 4)
    once = pl.Buffered(1)
    return pl.pallas_call(
        functools.partial(_dsa_kernel, k_top=k_top),
        grid=(nq,),
        in_specs=[
            pl.BlockSpec((QB_B, D_B), lambda i: (i, OFF_QB // D_B)),
            pl.BlockSpec((QB_B, H_IDX * D_IDX), lambda i: (i, OFF_QI // (H_IDX * D_IDX))),
            pl.BlockSpec((QB_B, LANES), lambda i: (i, OFF_MISC // LANES)),
            pl.BlockSpec((s, LANES), lambda i: (0, OFF_MISC // LANES), pipeline_mode=once),
            pl.BlockSpec((s, D_B_KV), lambda i: (0, OFF_KB // D_B_KV), pipeline_mode=once),
            pl.BlockSpec((s, D_B_KV), lambda i: (0, OFF_VB // D_B_KV), pipeline_mode=once),
            pl.BlockSpec((4 * H_B, QB_B, KB_B), lambda i: (0, 0, 0), pipeline_mode=once),
        ],
        out_specs=pl.BlockSpec((QB_B, D_B), lambda i: (i, 0)),
        out_shape=jax.ShapeDtypeStruct((s, D_B), BF16),
        scratch_shapes=[
            pltpu.VMEM((QB_B, s), I32),
            pltpu.VMEM((H_IDX, QB_B, KB_B), F32),
            pltpu.VMEM((2, QB_B, KB_B), I32),
            pltpu.VMEM((H_B_KV, GQA, QB_B, LANES), F32),
            pltpu.VMEM((H_B_KV, GQA * QB_B, LANES), F32),
        ],
        compiler_params=_cparams(1),
    )(pb, pb, p32, pb, pb, pb, t5_tables)


QB_C = 512
HEADS_PER_STEP_C = LANES // HEAD_DIM


PREP_TILE_C = 512


def _fox_prep_kernel(q_ref, k_ref, v_ref, cum_ref, qa_ref, ka_ref, va_ref):
    rows = q_ref.shape[0]
    lane = lax.broadcasted_iota(I32, (rows, HEAD_DIM), 1)
    ones = jnp.ones((rows, HEAD_DIM), BF16)
    for h in range(H_C):
        hs = slice(h * HEAD_DIM, (h + 1) * HEAD_DIM)
        c1, c2, c3 = [c.astype(F32) for c in _split3(cum_ref[:, h:h + 1])]
        unit = jnp.where(lane < 6, 1.0, 0.0)
        q_extra = jnp.where(lane < 3, jnp.where(lane == 0, c1, jnp.where(lane == 1, c2, c3)), unit)
        k_extra = jnp.where(lane < 3, unit, -jnp.where(lane == 3, c1, jnp.where(lane == 4, c2, c3)) * unit)
        qa_ref[h] = jnp.concatenate([q_ref[:, hs] * ATTN_SCALE, q_extra.astype(BF16)], axis=1)
        ka_ref[h] = jnp.concatenate([k_ref[:, hs], k_extra.astype(BF16)], axis=1)
        va_ref[h] = jnp.concatenate([v_ref[:, hs], ones], axis=1)


def _fox_prepare(pb, s, cum):
    tm = PREP_TILE_C
    row = lambda off: pl.BlockSpec((tm, D_C), lambda i: (i, off // D_C))
    out = pl.BlockSpec((H_C, tm, LANES), lambda i: (0, i, 0))
    return pl.pallas_call(
        _fox_prep_kernel,
        grid=(s // tm,),
        in_specs=[row(OFF_QC), row(OFF_KC), row(OFF_VC), pl.BlockSpec((tm, 16), lambda i: (i, 0))],
        out_specs=[out, out, out],
        out_shape=[jax.ShapeDtypeStruct((H_C, s, LANES), BF16)] * 3,
        compiler_params=_cparams(1),
    )(pb, pb, pb, cum)


def _fox_kernel(q_ref, k_ref, v_ref, o_ref):
    i = pl.program_id(1)
    row = lax.broadcasted_iota(I32, (QB_C, QB_C), 0)
    col = lax.broadcasted_iota(I32, (QB_C, QB_C), 1)
    heads = range(HEADS_PER_STEP_C)
    q = [q_ref[hh] for hh in heads]

    def scores(hh, j):
        return _dot_t(q[hh], k_ref[hh, pl.ds(pl.multiple_of(j * QB_C, QB_C), QB_C), :])

    def halves_max(s):
        return functools.reduce(jnp.maximum, [s[:, c:c + LANES] for c in range(0, QB_C, LANES)])

    diag = [jnp.where(col <= row, scores(hh, i), NEG_INF) for hh in heads]

    def max_body(j, ms):
        return tuple(jnp.maximum(ms[hh], halves_max(scores(hh, j))) for hh in heads)

    ms = lax.fori_loop(0, i, max_body, tuple(halves_max(diag[hh]) for hh in heads))
    m_b = [jnp.broadcast_to(jnp.max(ms[hh], axis=1, keepdims=True), (QB_C, QB_C)) for hh in heads]

    def pv(hh, s, j):
        p = jnp.exp(s - m_b[hh]).astype(BF16)
        return _dot(p, v_ref[hh, pl.ds(pl.multiple_of(j * QB_C, QB_C), QB_C), :])

    def pv_body(j, accs):
        return tuple(accs[hh] + pv(hh, scores(hh, j), j) for hh in heads)

    accs = lax.fori_loop(0, i, pv_body, tuple(pv(hh, diag[hh], i) for hh in heads))
    o_ref[...] = jnp.concatenate(
        [accs[hh][:, :HEAD_DIM] / accs[hh][:, HEAD_DIM:] for hh in heads], axis=1).astype(BF16)


def _fox_prompt(qa, ka, va, s):
    nq = s // QB_C
    ng = H_C // HEADS_PER_STEP_C
    hp = HEADS_PER_STEP_C
    return pl.pallas_call(
        _fox_kernel,
        grid=(ng, nq),
        in_specs=[
            pl.BlockSpec((hp, QB_C, LANES), lambda g, i: (g, i, 0)),
            pl.BlockSpec((hp, s, LANES), lambda g, i: (g, 0, 0)),
            pl.BlockSpec((hp, s, LANES), lambda g, i: (g, 0, 0)),
        ],
        out_specs=pl.BlockSpec((QB_C, LANES), lambda g, i: (i, g)),
        out_shape=jax.ShapeDtypeStruct((s, D_C), BF16),
        compiler_params=_cparams(2),
    )(qa, ka, va)


def _softmax_pv(s, v_all):
    m = jnp.max(s, axis=1, keepdims=True)
    p = jnp.exp(s - m)
    l = jnp.sum(p, axis=1, keepdims=True)
    return _dot(p.astype(BF16), v_all) / l


def _with_new_rows(cache, new, pad_rows):
    parts = [cache.astype(BF16), new]
    if pad_rows:
        parts.append(jnp.zeros((pad_rows, new.shape[1]), BF16))
    return jnp.concatenate(parts, axis=0)


def _sample_kernel(qa_ref, ka_ref, va_ref, kvb_ref, qi_ref, qb_ref, qc_ref, kc_ref, vc_ref, misc16_ref,
                   misc32_ref, ca_ref, cb_ref, cbi_ref, cc_ref, clf_ref, clft_ref, lfn_ref, lfnt_ref,
                   bias_a_ref, t5_ref, triu_ref, tril_ref,
                   oa_ref, ob_ref, oc_ref, *, t, past, na, k_top):
    la = na + LANES
    lk = past + LANES
    pad = LANES - t

    ca = ca_ref[...]
    ka_all = _with_new_rows(ca[:, :D_A], ka_ref[...], pad)
    va_all = _with_new_rows(ca[:, D_A:], va_ref[...], pad)
    for h in range(H_A):
        hs = slice(h * HEAD_DIM, (h + 1) * HEAD_DIM)
        s = _dot_t(qa_ref[:, hs], ka_all[:, hs]) * ATTN_SCALE + bias_a_ref[h]
        oa_ref[:, hs] = _softmax_pv(s, va_all[:, hs]).astype(BF16)

    col = lax.broadcasted_iota(I32, (t, lk), 1)
    ki_all = _with_new_rows(cbi_ref[...], misc16_ref[:, :D_IDX], pad)
    wi = misc32_ref[:, MISC_WI:MISC_WI + H_IDX] * (H_IDX ** -0.5 * D_IDX ** -0.5)
    acc = jnp.zeros((t, lk), F32)
    for h in range(H_IDX):
        isc = _dot_t(qi_ref[:, h * D_IDX:(h + 1) * D_IDX], ki_all)
        acc = acc + jnp.maximum(isc, 0.0) * wi[:, h:h + 1]
    keys = jnp.where(col < past + t, _sortable(acc), jnp.int32(NEG_KEY))

    def count(pred):
        return jnp.sum(jnp.where(pred, 1.0, 0.0), axis=1, keepdims=True)

    thr, _ = _kth_largest(lambda cand: count(keys >= cand), t, k_top)
    live = thr > jnp.int32(NEG_KEY)
    thr = jnp.maximum(thr, jnp.int32(NEG_KEY))
    need = float(k_top) - count(keys > thr)
    jstar = _tie_limit(lambda cand: count((keys == thr) & (col < cand)), need, t)
    sel = (keys > thr) | ((keys == thr) & (col < jnp.where(live, jstar, 0)))

    cb = cb_ref[...]
    kb_all = _with_new_rows(cb[:, :D_B_KV], kvb_ref[:, :D_B_KV], pad)
    vb_all = _with_new_rows(cb[:, D_B_KV:], kvb_ref[:, D_B_KV:], pad)
    for n in range(H_B_KV):
        ns = slice(n * HEAD_DIM, (n + 1) * HEAD_DIM)
        for g in range(GQA):
            h = n * GQA + g
            hs = slice(h * HEAD_DIM, (h + 1) * HEAD_DIM)
            s = _dot_t(qb_ref[:, hs], kb_all[:, ns]) * ATTN_SCALE + t5_ref[h]
            s = jnp.where(sel, s, NEG_INF)
            ob_ref[:, hs] = _softmax_pv(s, vb_all[:, ns]).astype(BF16)

    cc = cc_ref[...]
    kc_all = _with_new_rows(cc[:, :D_C], kc_ref[...], pad)
    vc_all = _with_new_rows(cc[:, D_C:], vc_ref[...], pad)
    lft = jnp.concatenate([clft_ref[...], lfnt_ref[...]], axis=1)
    t1, t2, t3 = _split3(lft)
    triu = triu_ref[...]
    cum_t = _dot(t1, triu) + _dot(t2, triu) + _dot(t3, triu)
    total = jnp.sum(clf_ref[...], axis=0, keepdims=True)
    n1, n2, n3 = _split3(lfn_ref[...])
    tril = tril_ref[...]
    cum_q = total + _dot(tril, n1) + _dot(tril, n2) + _dot(tril, n3)
    row = lax.broadcasted_iota(I32, (t, lk), 0)
    causal = col <= past + row
    for h in range(H_C):
        hs = slice(h * HEAD_DIM, (h + 1) * HEAD_DIM)
        s = _dot_t(qc_ref[:, hs], kc_all[:, hs]) * ATTN_SCALE + cum_q[:, h:h + 1] - cum_t[h:h + 1, :]
        s = jnp.where(causal, s, NEG_INF)
        oc_ref[:, hs] = _softmax_pv(s, vc_all[:, hs]).astype(BF16)


def _sample_mixers(p32, pb, s, ca, cb, cbi, cc, clf, clft, lfn, lfnt, bias_a, t5_tab, triu, tril, layer):
    nb, na = ca.shape[1], ca.shape[2]
    past = cb.shape[2]
    t = lfn.shape[1]
    k_top = min(TOPK_MAX, (past + t) // 4)
    r0 = s // t
    lk = past + LANES
    la = na + LANES
    row = lambda width, off: pl.BlockSpec((t, width), lambda b: (r0 + b, off // width))
    cache = lambda rows, width: pl.BlockSpec((None, None, rows, width), lambda b: (layer, b, 0, 0))
    full = lambda shape: pl.BlockSpec(shape, lambda b: (0,) * len(shape))
    in_specs = [
        row(D_A, OFF_QA), row(D_A, OFF_KA), row(D_A, OFF_VA), row(2 * D_B_KV, OFF_KB),
        row(H_IDX * D_IDX, OFF_QI), row(D_B, OFF_QB), row(D_C, OFF_QC), row(D_C, OFF_KC), row(D_C, OFF_VC),
        row(LANES, OFF_MISC), row(LANES, OFF_MISC),
        cache(na, 2 * D_A), cache(past, 2 * D_B_KV), cache(past, D_IDX), cache(past, 2 * D_C),
        cache(past, 16), cache(16, past),
        pl.BlockSpec((None, t, 16), lambda b: (b, 0, 0)),
        pl.BlockSpec((None, 16, LANES), lambda b: (b, 0, 0)),
        full((H_A, t, la)), full((H_B, t, lk)), full((lk, lk)), full((t, t)),
    ]
    out = lambda width: pl.BlockSpec((t, width), lambda b: (b, 0))
    return pl.pallas_call(
        functools.partial(_sample_kernel, t=t, past=past, na=na, k_top=k_top),
        grid=(nb,),
        in_specs=in_specs,
        out_specs=[out(D_A), out(D_B), out(D_C)],
        out_shape=[jax.ShapeDtypeStruct((nb * t, w), BF16) for w in (D_A, D_B, D_C)],
        compiler_params=_cparams(1),
    )(pb, pb, pb, pb, pb, pb, pb, pb, pb, pb, p32, ca, cb, cbi, cc, clf, clft, lfn, lfnt,
      bias_a, t5_tab, triu, tril)


def _outproj_kernel(x_ref, g_ref, ap, bp, cp, as_, bs, cs, w_ref, o_ref, *, n_prompt_blocks):
    i = pl.program_id(0)

    def run(a, b, c):
        y = (_dot(a[...], w_ref[:D_A, :]) + _dot(b[...], w_ref[D_A:D_A + D_B, :])
             + _dot(c[...], w_ref[D_A + D_B:, :]))
        o_ref[...] = x_ref[...] + g_ref[...] * y

    pl.when(i < n_prompt_blocks)(lambda: run(ap, bp, cp))
    pl.when(i >= n_prompt_blocks)(lambda: run(as_, bs, cs))


def _out_project(x, gate, mix_p, mix_s, w_out_b, layer, n_prompt_blocks):
    m = x.shape[0]
    tm = TOKEN_TILE
    last_p = n_prompt_blocks - 1
    p_map = lambda i: (jnp.minimum(i, last_p), 0)
    s_map = lambda i: (jnp.maximum(i - n_prompt_blocks, 0), 0)
    widths = (D_A, D_B, D_C)
    return pl.pallas_call(
        functools.partial(_outproj_kernel, n_prompt_blocks=n_prompt_blocks),
        grid=(m // tm,),
        in_specs=[
            pl.BlockSpec((tm, D_MODEL), lambda i: (i, 0)),
            pl.BlockSpec((tm, D_MODEL), lambda i: (_mod_block_index(i, n_prompt_blocks), 0)),
            *[pl.BlockSpec((tm, w), p_map) for w in widths],
            *[pl.BlockSpec((tm, w), s_map) for w in widths],
            pl.BlockSpec((None, D_MODEL, D_MODEL), lambda i: (layer, 0, 0)),
        ],
        out_specs=pl.BlockSpec((tm, D_MODEL), lambda i: (i, 0)),
        out_shape=jax.ShapeDtypeStruct((m, D_MODEL), F32),
        compiler_params=_cparams(1),
    )(x, gate, *mix_p, *mix_s, w_out_b)


def _lane_pick(vals, lane, idx):
    return jnp.sum(jnp.where(lane == idx, vals, 0.0), axis=1, keepdims=True)


def _first_argmax(vals, lane):
    m = jnp.max(vals, axis=1, keepdims=True)
    idx = jnp.min(jnp.where(vals == m, lane, float(LANES)), axis=1, keepdims=True)
    return m, idx


def _router_kernel(x_ref, gain_ref, sc_ref, sh_ref, wr_ref, br_ref, tri_ref, h_ref, meta_ref, cnt_ref, carry):
    @pl.when(pl.program_id(0) == 0)
    def _():
        carry[...] = jnp.zeros_like(carry)

    h = _norm_mod(x_ref[...], gain_ref[...], sc_ref[...], sh_ref[...])
    hb = h.astype(BF16)
    h_ref[...] = hb
    scores = jax.nn.sigmoid(_dot(hb, wr_ref[...]))
    lane_i = lax.broadcasted_iota(I32, scores.shape, 1)
    lane = lane_i.astype(F32)
    sel = jnp.where(lane_i < N_EXPERTS, scores + br_ref[...], NEG_INF)
    group = (lane_i // (N_EXPERTS // N_GROUPS)).astype(F32)

    best = None
    for g in range(N_GROUPS):
        in_g = jnp.where(group == float(g), sel, NEG_INF)
        m1, i1 = _first_argmax(in_g, lane)
        m2 = jnp.max(jnp.where(lane == i1, NEG_INF, in_g), axis=1, keepdims=True)
        gs = m1 + m2
        if best is None:
            best, gbest = gs, jnp.zeros_like(i1)
        else:
            better = gs > best
            best = jnp.where(better, gs, best)
            gbest = jnp.where(better, float(g), gbest)

    in_best = jnp.where(group == gbest, sel, NEG_INF)
    _, e0 = _first_argmax(in_best, lane)
    _, e1 = _first_argmax(jnp.where(lane == e0, NEG_INF, in_best), lane)
    w0 = _lane_pick(scores, lane, e0)
    w1 = _lane_pick(scores, lane, e1)
    wsum = w0 + w1

    onehot = jnp.where((lane == e0) | (lane == e1), 1.0, 0.0)
    before = _dot(tri_ref[...], onehot.astype(BF16)) + carry[...]
    r0 = _lane_pick(before, lane, e0)
    r1 = _lane_pick(before, lane, e1)
    carry[...] = carry[...] + jnp.sum(onehot, axis=0, keepdims=True)
    cnt_ref[...] = carry[...]

    meta = jnp.zeros(scores.shape, F32)
    for k, v in enumerate((e0, e1, r0, r1, w0 / wsum, w1 / wsum)):
        meta = jnp.where(lane_i == k, v, meta)
    meta_ref[...] = meta


def _router(x, gain, sc, sh, w_router_p, b_router_p, tri, n_prompt_blocks):
    m = x.shape[0]
    tm = TOKEN_TILE
    mod_map = lambda i: (_mod_block_index(i, n_prompt_blocks), 0)
    return pl.pallas_call(
        _router_kernel,
        grid=(m // tm,),
        in_specs=[
            pl.BlockSpec((tm, D_MODEL), lambda i: (i, 0)),
            pl.BlockSpec((1, D_MODEL), lambda i: (0, 0)),
            pl.BlockSpec((tm, D_MODEL), mod_map),
            pl.BlockSpec((tm, D_MODEL), mod_map),
            pl.BlockSpec((D_MODEL, LANES), lambda i: (0, 0)),
            pl.BlockSpec((1, LANES), lambda i: (0, 0)),
            pl.BlockSpec((tm, tm), lambda i: (0, 0)),
        ],
        out_specs=[
            pl.BlockSpec((tm, D_MODEL), lambda i: (i, 0)),
            pl.BlockSpec((tm, LANES), lambda i: (i, 0)),
            pl.BlockSpec((1, LANES), lambda i: (0, 0)),
        ],
        out_shape=[
            jax.ShapeDtypeStruct((m, D_MODEL), BF16),
            jax.ShapeDtypeStruct((m, LANES), F32),
            jax.ShapeDtypeStruct((1, LANES), F32),
        ],
        scratch_shapes=[pltpu.VMEM((1, LANES), F32)],
        compiler_params=_cparams(1),
    )(x, gain, sc, sh, w_router_p, b_router_p, tri)


EXPERT_ROWS = 256
EXPERT_TF = 512


def _expert_kernel(be_ref, nu_ref, x_ref, wg_ref, wu_ref, wd_ref, o_ref):
    b = pl.program_id(0)
    f = pl.program_id(1)

    @pl.when(b < nu_ref[0])
    def _():
        x = x_ref[...]
        a = _dot(x, wg_ref[...].astype(BF16))
        u = _dot(x, wu_ref[...].astype(BF16))
        act = (a * jax.nn.sigmoid(a) * u).astype(BF16)
        y = _dot(act, wd_ref[...].astype(BF16))

        @pl.when(f == 0)
        def _():
            o_ref[...] = y

        @pl.when(f > 0)
        def _():
            o_ref[...] += y


def _experts(xb, block_e, n_used, w_g, w_u, w_d, layer):
    nslots = xb.shape[0]
    nb = nslots // EXPERT_ROWS
    nf = D_EXPERT // EXPERT_TF

    def blk(b, nu):
        return jnp.minimum(b, nu[0] - 1)

    def fidx(b, f, nu):
        return jnp.where(b < nu[0], f, nf - 1)

    grid_spec = pltpu.PrefetchScalarGridSpec(
        num_scalar_prefetch=2,
        grid=(nb, nf),
        in_specs=[
            pl.BlockSpec((EXPERT_ROWS, D_MODEL), lambda b, f, be, nu: (blk(b, nu), 0)),
            pl.BlockSpec((None, None, D_MODEL, EXPERT_TF),
                         lambda b, f, be, nu: (layer, be[blk(b, nu)], 0, fidx(b, f, nu))),
            pl.BlockSpec((None, None, D_MODEL, EXPERT_TF),
                         lambda b, f, be, nu: (layer, be[blk(b, nu)], 0, fidx(b, f, nu))),
            pl.BlockSpec((None, None, EXPERT_TF, D_MODEL),
                         lambda b, f, be, nu: (layer, be[blk(b, nu)], fidx(b, f, nu), 0)),
        ],
        out_specs=pl.BlockSpec((EXPERT_ROWS, D_MODEL), lambda b, f, be, nu: (blk(b, nu), 0)),
    )
    return pl.pallas_call(
        _expert_kernel,
        grid_spec=grid_spec,
        out_shape=jax.ShapeDtypeStruct((nslots, D_MODEL), F32),
        compiler_params=_cparams(2),
    )(block_e, n_used, xb, w_g, w_u, w_d)


def _t5_bucket(rel):
    nb = T5_BUCKETS // 2
    max_exact = nb // 2
    ret = jnp.where(rel < 0, nb, 0)
    n = jnp.abs(rel)
    nf = jnp.maximum(n, 1).astype(F32)
    large = max_exact + (jnp.log(nf / max_exact) / math.log(T5_MAX_DIST / max_exact) * (nb - max_exact)).astype(I32)
    large = jnp.minimum(large, nb - 1)
    return ret + jnp.where(n < max_exact, n, large)


def _band_bias(table, qpos, kpos, valid):
    rel = qpos[:, None] - kpos[None, :]
    bias = table.astype(F32)[:, jnp.clip(rel, -A_REL_FUTURE, A_REL_PAST) + A_REL_FUTURE]
    qc = qpos[:, None] // CHUNK
    kc = kpos[None, :] // CHUNK
    ok = valid[None, :] & (kc <= qc) & (kc >= qc - A_LEFT_CHUNKS)
    return jnp.where(ok[None], bias, NEG_INF)


def _t5_bias(t5, qpos, kpos):
    return jnp.moveaxis(t5.astype(F32)[_t5_bucket(qpos[:, None] - kpos[None, :])], 2, 0)


def _relayout_w_in(w_in_l):
    qa, ka, va, qb, kb, vb, qi, ki, wi, qc, kc, vc, fc = jnp.split(w_in_l, PROJ_SPLITS, axis=1)
    pad = jnp.zeros((D_MODEL, P_COLS - OFF_MISC - D_IDX - H_IDX - H_C), w_in_l.dtype)
    return jnp.concatenate([qa, ka, va, kb, vb, qi, qb, qc, kc, vc, ki, wi, fc, pad], axis=1).astype(BF16)


def _column_params(qk_gain_l, b_forget_l):
    ones = lambda n: jnp.ones((n,), F32)
    g = qk_gain_l.astype(F32)
    gq = jnp.concatenate([
        jnp.tile(g[0], H_A), jnp.tile(g[1], H_A), ones(D_A),
        jnp.tile(g[3], H_B_KV), ones(D_B_KV), ones(H_IDX * D_IDX),
        jnp.tile(g[2], H_B), jnp.tile(g[4], H_C), jnp.tile(g[5], H_C), ones(D_C), ones(P_COLS - OFF_MISC)])
    bq = jnp.zeros((P_COLS,), F32).at[OFF_MISC + MISC_FC:OFF_MISC + MISC_FC + H_C].set(b_forget_l.astype(F32))
    return gq[None, :], bq[None, :]


def _tri(n, *, strict=False, upper=False):
    r = jnp.arange(n)[:, None]
    c = jnp.arange(n)[None, :]
    m = (r < c if strict else r <= c) if upper else (c < r if strict else c <= r)
    return m.astype(BF16)


def kernel(x_prompt, x_sample, c_prompt, c_sample, cache_a_kv, cache_b_kv, cache_b_kidx, cache_c_kv, cache_c_logf,
           w_ada, b_ada, norm_gain, w_in, b_forget, qk_gain, rel_bias_a, t5_bias, w_out, w_router, b_router,
           w_e_gate, w_e_up, w_e_down):
    depth = w_in.shape[0]
    bp, s, d = x_prompt.shape
    nb, t, _ = x_sample.shape
    ns = nb * t
    tm = TOKEN_TILE
    assert bp == 1 and d == D_MODEL and s % tm == 0 and ns % tm == 0 and s % QB_C == 0
    n_pb = s // tm
    m = s + ns
    na = cache_a_kv.shape[2]
    past = cache_b_kv.shape[2]
    keep = min(A_LEFT_CHUNKS * CHUNK, s)

    x = jnp.concatenate([x_prompt.reshape(s, d), x_sample.reshape(ns, d)], axis=0)
    c_all = jnp.concatenate([c_prompt, c_sample, jnp.zeros((-(bp + nb) % 8, d), F32)], axis=0)

    q128 = jnp.arange(QB_A)
    band_k = jnp.arange(A_KEY_BLOCKS * QB_A) - (A_KEY_BLOCKS - 1) * QB_A
    qpos_s = past + jnp.arange(t)
    kpos_a = jnp.concatenate([jnp.arange(past - na, past), qpos_s, jnp.zeros((LANES - t,), I32)])
    valid_a = jnp.arange(na + LANES) < na + t
    kpos_b = jnp.concatenate([jnp.arange(past), qpos_s, jnp.zeros((LANES - t,), I32)])
    t5_far = t5_bias.astype(F32)[T5_BUCKETS // 2 - 1]
    kcol = jnp.arange(KB_B)
    t5_tables = jnp.concatenate(
        [_t5_bias(t5_bias, q128 + off, kcol) - t5_far[:, None, None] for off in (QB_B, 2 * QB_B, 0)]
        + [jnp.zeros((H_B, QB_B, KB_B), F32)], axis=0)
    t5_s = _t5_bias(t5_bias, qpos_s, kpos_b)
    bd = ((jnp.arange(PROJ_TN)[:, None] // HEAD_DIM == jnp.arange(PROJ_TN)[None, :] // HEAD_DIM)
          .astype(F32) / HEAD_DIM).astype(BF16)
    tri_cum = _tri(CUM_TILE)
    tri_rank = _tri(tm, strict=True)
    triu_s = _tri(past + LANES, upper=True)
    tril_s = _tri(t)
    w_out_b = w_out.astype(BF16)
    w_router_p = jnp.pad(w_router, ((0, 0), (0, LANES - N_EXPERTS))).astype(BF16)
    b_router_p = jnp.pad(b_router.astype(F32), (0, LANES - N_EXPERTS))[None, :]

    ca = cache_a_kv.reshape(depth, nb, na, 2 * D_A)
    cb = cache_b_kv.reshape(depth, nb, past, 2 * D_B_KV)
    cc = cache_c_kv.reshape(depth, nb, past, 2 * D_C)
    clf = jnp.pad(cache_c_logf.astype(F32), ((0, 0), (0, 0), (0, 0), (0, 16 - H_C)))
    clft = jnp.swapaxes(clf, 2, 3)

    n_asg = 2 * m
    n_eb = n_asg // EXPERT_ROWS + N_EXPERTS
    tok = jnp.arange(m, dtype=I32)

    states_p, states_s = [], []
    for l in range(depth):
        mod = _ada(c_all, w_ada, b_ada, l)
        mods = []
        for part in jnp.split(mod, 6, axis=1):
            mods.append(jnp.concatenate([jnp.broadcast_to(part[:1], (tm, d)), jnp.repeat(part[bp:bp + nb], t, axis=0)], 0))
        sh1, sc1, g1, sh2, sc2, g2 = mods

        gq, bq = _column_params(qk_gain[l], b_forget[l])
        p32, pb = _project(x, norm_gain[l, 0][None, :], sc1, sh1, _relayout_w_in(w_in[l]), gq, bq, bd, n_pb)

        bias_a_p = _band_bias(rel_bias_a[l], q128 + (A_KEY_BLOCKS - 1) * QB_A, band_k + (A_KEY_BLOCKS - 1) * QB_A,
                              jnp.ones((A_KEY_BLOCKS * QB_A,), bool))
        oa_p = _band_prompt(pb, s, bias_a_p)
        ob_p = _dsa_prompt(p32, pb, s, t5_tables)
        cum = _cumsum_rows(p32, s, tri_cum)[:, MISC_FC:MISC_FC + 16]
        oc_p = _fox_prompt(*_fox_prepare(pb, s, cum), s)

        lfn = p32[s:, OFF_MISC + MISC_FC:OFF_MISC + MISC_FC + 16].reshape(nb, t, 16)
        lfnt = jnp.pad(jnp.swapaxes(lfn, 1, 2), ((0, 0), (0, 0), (0, LANES - t)))
        bias_a_s = _band_bias(rel_bias_a[l], qpos_s, kpos_a, valid_a)
        oa_s, ob_s, oc_s = _sample_mixers(p32, pb, s, ca, cb, cache_b_kidx, cc, clf, clft, lfn, lfnt,
                                          bias_a_s, t5_s, triu_s, tril_s, l)

        x = _out_project(x, g1, (oa_p, ob_p, oc_p), (oa_s, ob_s, oc_s), w_out_b, l, n_pb)

        h2, meta, counts = _router(x, norm_gain[l, 1][None, :], sc2, sh2, w_router_p, b_router_p, tri_rank, n_pb)
        e0 = meta[:, 0].astype(I32)
        e1 = meta[:, 1].astype(I32)
        counts = counts[0, :N_EXPERTS].astype(I32)
        padded = (counts + EXPERT_ROWS - 1) // EXPERT_ROWS * EXPERT_ROWS
        pend = jnp.cumsum(padded)
        pstart = pend - padded
        dest0 = pstart[e0] + meta[:, 2].astype(I32)
        dest1 = pstart[e1] + meta[:, 3].astype(I32)
        slot_tok = jnp.full((n_eb * EXPERT_ROWS,), m, I32).at[dest0].set(tok).at[dest1].set(tok)
        block_e = jnp.minimum(jnp.searchsorted(pend, jnp.arange(n_eb, dtype=I32) * EXPERT_ROWS, side='right'),
                              N_EXPERTS - 1).astype(I32)
        n_used = (pend[-1:] // EXPERT_ROWS).astype(I32)
        xb = jnp.concatenate([h2, jnp.zeros((1, d), BF16)], axis=0)[slot_tok]
        yb = _experts(xb, block_e, n_used, w_e_gate, w_e_up, w_e_down, l)
        moe_out = yb[dest0] * meta[:, 4:5] + yb[dest1] * meta[:, 5:6]
        x = x + jnp.concatenate([jnp.broadcast_to(g2[:1], (s, d)), g2[tm:]], axis=0) * moe_out

        def states(rows, nbatch, a_rows):
            r = p32[rows]
            n = r.shape[0] // nbatch
            kv = lambda off, heads: r[:, off:off + 2 * heads * HEAD_DIM].reshape(nbatch, n, 2, heads, HEAD_DIM)
            return (kv(OFF_KA, H_A)[:, n - a_rows:], kv(OFF_KB, H_B_KV),
                    r[:, OFF_MISC:OFF_MISC + D_IDX].reshape(nbatch, n, D_IDX), kv(OFF_KC, H_C),
                    r[:, OFF_MISC + MISC_FC:OFF_MISC + MISC_FC + H_C].reshape(nbatch, n, H_C))

        states_p.append(states(slice(0, s), bp, keep))
        states_s.append(states(slice(s, m), nb, t))

    stk = lambda sts, i: jnp.stack([st[i] for st in sts], axis=0)
    return (x[:s].reshape(bp, s, d), x[s:].reshape(nb, t, d),
            *[stk(states_p, i) for i in range(5)], *[stk(states_s, i) for i in range(5)])
```

```python
import functools
import math

import jax
import jax.numpy as jnp
from jax import lax
from jax.experimental import pallas as pl
from jax.experimental.pallas import tpu as pltpu

F32 = jnp.float32
BF16 = jnp.bfloat16
I32 = jnp.int32

D_MODEL = 2048
HEAD_DIM = 64
CHUNK = 64
EPS = 1e-6
ATTN_SCALE = HEAD_DIM ** -0.5
H_A = 8
A_LEFT_CHUNKS = 8
A_REL_PAST = 128
A_REL_FUTURE = CHUNK - 1
H_B = 12
H_B_KV = 4
GQA = H_B // H_B_KV
H_IDX = 16
D_IDX = 64
TOPK_MAX = 256
H_C = 12
T5_BUCKETS = 32
T5_MAX_DIST = 128
N_EXPERTS = 32
N_GROUPS = 4
D_EXPERT = 1024

D_A = H_A * HEAD_DIM
D_B = H_B * HEAD_DIM
D_B_KV = H_B_KV * HEAD_DIM
D_C = H_C * HEAD_DIM
PROJ_SIZES = (D_A, D_A, D_A, D_B, D_B_KV, D_B_KV, H_IDX * D_IDX, D_IDX, H_IDX, D_C, D_C, D_C, H_C)
PROJ_SPLITS = tuple(sum(PROJ_SIZES[:i + 1]) for i in range(len(PROJ_SIZES) - 1))

OFF_QA, OFF_KA, OFF_VA = 0, 512, 1024
OFF_KB, OFF_VB = 1536, 1792
OFF_QI = 2048
OFF_QB = 3072
OFF_QC, OFF_KC, OFF_VC = 3840, 4608, 5376
OFF_MISC = 6144
MISC_WI = 64
MISC_FC = 80
P_COLS = 6400
PROJ_TN = 256
NORM_COL_BLOCKS = (0, 1, 2, 3, 6, 12, 13, 14, 15, 16, 17, 18, 19, 20)
MISC_COL_BLOCK = OFF_MISC // PROJ_TN

TOKEN_TILE = 512
LANES = 128
VMEM_LIMIT = 56 * 1024 * 1024

NEG_INF = float("-inf")
NEG_KEY = (0xFF800000 ^ 0x7FFFFFFF) - (1 << 32)
IDX_BIG = 1 << 30


def _cparams(n_axes):
    return pltpu.CompilerParams(dimension_semantics=("arbitrary",) * n_axes, vmem_limit_bytes=VMEM_LIMIT)


def _split3(x):
    x1 = x.astype(BF16)
    r1 = x - x1.astype(F32)
    x2 = r1.astype(BF16)
    r2 = r1 - x2.astype(F32)
    return x1, x2, r2.astype(BF16)


def _dot(a, b):
    return jnp.dot(a, b, preferred_element_type=F32)


def _dot_t(a, b):
    return lax.dot_general(a, b, (((1,), (1,)), ((), ())), preferred_element_type=F32)


def _sortable(x):
    b = lax.bitcast_convert_type(x, I32)
    return jnp.where(b < 0, b ^ jnp.int32(0x7FFFFFFF), b)


def _ada_kernel(c_ref, w_ref, b_ref, o_ref):
    c = c_ref[...]
    a = (c * jax.nn.sigmoid(c)).astype(BF16)
    o_ref[...] = _dot(a, w_ref[...].astype(BF16)) + b_ref[...]


def _ada(c_all, w_ada, b_ada, layer):
    rows = c_all.shape[0]
    n_out = w_ada.shape[2]
    tn = 1024
    return pl.pallas_call(
        _ada_kernel,
        grid=(n_out // tn,),
        in_specs=[
            pl.BlockSpec((rows, D_MODEL), lambda n: (0, 0)),
            pl.BlockSpec((None, D_MODEL, tn), lambda n: (layer, 0, n)),
            pl.BlockSpec((None, 1, tn), lambda n: (layer, 0, n)),
        ],
        out_specs=pl.BlockSpec((rows, tn), lambda n: (0, n)),
        out_shape=jax.ShapeDtypeStruct((rows, n_out), F32),
        compiler_params=_cparams(1),
    )(c_all, w_ada, b_ada.reshape(b_ada.shape[0], 1, n_out))


def _mod_block_index(i, n_prompt_blocks):
    return jnp.where(i < n_prompt_blocks, 0, i - n_prompt_blocks + 1)


def _norm_mod(x, gain, sc, sh):
    ms = jnp.mean(x * x, axis=-1, keepdims=True)
    return (x * lax.rsqrt(ms + EPS) * gain) * (1.0 + sc) + sh


def _proj_kernel(x_ref, gain_ref, sc_ref, sh_ref, w_ref, gq_ref, bq_ref, bd_ref, o32_ref, o16_ref, h_scr):
    n = pl.program_id(1)

    @pl.when(n == 0)
    def _():
        h_scr[...] = _norm_mod(x_ref[...], gain_ref[...], sc_ref[...], sh_ref[...]).astype(BF16)

    y = _dot(h_scr[...], w_ref[...])

    is_norm = functools.reduce(jnp.logical_or, [n == b for b in NORM_COL_BLOCKS])
    is_misc = n == MISC_COL_BLOCK

    def emit(v):
        o32_ref[...] = v
        o16_ref[...] = v.astype(BF16)

    @pl.when(is_norm)
    def _():
        s1, s2, s3 = _split3(y * y)
        bd = bd_ref[...]
        ms = _dot(s1, bd) + _dot(s2, bd) + _dot(s3, bd)
        emit(y * lax.rsqrt(ms + EPS) * gq_ref[...])

    @pl.when(is_misc)
    def _():
        lane = lax.broadcasted_iota(I32, y.shape, 1)
        z = y + bq_ref[...]
        logsig = jnp.minimum(z, 0.0) - jnp.log(1.0 + jnp.exp(-jnp.abs(z)))
        emit(jnp.where((lane >= MISC_FC) & (lane < MISC_FC + H_C), logsig, y))

    @pl.when(jnp.logical_not(jnp.logical_or(is_norm, is_misc)))
    def _():
        emit(y)


def _project(x, gain, sc, sh, w_r, gq, bq, bd, n_prompt_blocks):
    m = x.shape[0]
    tm = TOKEN_TILE
    mod_map = lambda i, n: (_mod_block_index(i, n_prompt_blocks), 0)
    return pl.pallas_call(
        _proj_kernel,
        grid=(m // tm, P_COLS // PROJ_TN),
        in_specs=[
            pl.BlockSpec((tm, D_MODEL), lambda i, n: (i, 0)),
            pl.BlockSpec((1, D_MODEL), lambda i, n: (0, 0)),
            pl.BlockSpec((tm, D_MODEL), mod_map),
            pl.BlockSpec((tm, D_MODEL), mod_map),
            pl.BlockSpec((None, D_MODEL, PROJ_TN), lambda i, n: (n, 0, 0)),
            pl.BlockSpec((1, PROJ_TN), lambda i, n: (0, n)),
            pl.BlockSpec((1, PROJ_TN), lambda i, n: (0, n)),
            pl.BlockSpec((PROJ_TN, PROJ_TN), lambda i, n: (0, 0)),
        ],
        out_specs=[
            pl.BlockSpec((tm, PROJ_TN), lambda i, n: (i, n)),
            pl.BlockSpec((tm, PROJ_TN), lambda i, n: (i, n)),
        ],
        out_shape=[jax.ShapeDtypeStruct((m, P_COLS), F32), jax.ShapeDtypeStruct((m, P_COLS), BF16)],
        scratch_shapes=[pltpu.VMEM((tm, D_MODEL), BF16)],
        compiler_params=_cparams(2),
    )(x, gain, sc, sh, w_r, gq, bq, bd)


CUM_TILE = 256


def _cumsum_kernel(x_ref, tri_ref, o_ref, carry):
    @pl.when(pl.program_id(0) == 0)
    def _():
        carry[...] = jnp.zeros_like(carry)

    x1, x2, x3 = _split3(x_ref[...])
    tri = tri_ref[...]
    c = _dot(tri, x1) + _dot(tri, x2) + _dot(tri, x3) + carry[...]
    o_ref[...] = c
    carry[...] = c[CUM_TILE - 1:CUM_TILE, :]


def _cumsum_rows(p32, s, tri):
    return pl.pallas_call(
        _cumsum_kernel,
        grid=(s // CUM_TILE,),
        in_specs=[
            pl.BlockSpec((CUM_TILE, LANES), lambda i: (i, OFF_MISC // LANES)),
            pl.BlockSpec((CUM_TILE, CUM_TILE), lambda i: (0, 0)),
        ],
        out_specs=pl.BlockSpec((CUM_TILE, LANES), lambda i: (i, 0)),
        out_shape=jax.ShapeDtypeStruct((s, LANES), F32),
        scratch_shapes=[pltpu.VMEM((1, LANES), F32)],
        compiler_params=_cparams(1),
    )(p32, tri)


QB_A = 128
A_KEY_BLOCKS = 5


def _band_kernel(q_ref, *refs):
    k_refs = refs[:A_KEY_BLOCKS]
    v_refs = refs[A_KEY_BLOCKS:2 * A_KEY_BLOCKS]
    bias_ref, o_ref = refs[2 * A_KEY_BLOCKS:]
    i = pl.program_id(0)
    for h in range(H_A):
        hs = slice(h * HEAD_DIM, (h + 1) * HEAD_DIM)
        q = q_ref[:, hs]
        parts = []
        for j in range(A_KEY_BLOCKS):
            s = _dot_t(q, k_refs[j][:, hs]) * ATTN_SCALE + bias_ref[h, :, j * QB_A:(j + 1) * QB_A]
            in_range = i - (A_KEY_BLOCKS - 1) + j >= 0
            parts.append(jnp.where(in_range, s, NEG_INF))
        m = functools.reduce(jnp.maximum, [jnp.max(s, axis=1, keepdims=True) for s in parts])
        l = jnp.zeros((QB_A, 1), F32)
        o = jnp.zeros((QB_A, HEAD_DIM), F32)
        for j in range(A_KEY_BLOCKS):
            p = jnp.exp(parts[j] - m)
            l = l + jnp.sum(p, axis=1, keepdims=True)
            o = o + _dot(p.astype(BF16), v_refs[j][:, hs])
        o_ref[:, hs] = (o / l).astype(BF16)


def _band_prompt(pb, s, bias_a):
    nq = s // QB_A
    back = A_KEY_BLOCKS - 1

    def kv_spec(j, col):
        return pl.BlockSpec((QB_A, D_A), lambda i: (jnp.maximum(i - back + j, 0), col))

    in_specs = [pl.BlockSpec((QB_A, D_A), lambda i: (i, OFF_QA // D_A))]
    in_specs += [kv_spec(j, OFF_KA // D_A) for j in range(A_KEY_BLOCKS)]
    in_specs += [kv_spec(j, OFF_VA // D_A) for j in range(A_KEY_BLOCKS)]
    in_specs += [pl.BlockSpec((H_A, QB_A, A_KEY_BLOCKS * QB_A), lambda i: (0, 0, 0))]
    return pl.pallas_call(
        _band_kernel,
        grid=(nq,),
        in_specs=in_specs,
        out_specs=pl.BlockSpec((QB_A, D_A), lambda i: (i, 0)),
        out_shape=jax.ShapeDtypeStruct((s, D_A), BF16),
        compiler_params=_cparams(1),
    )(pb, *([pb] * (2 * A_KEY_BLOCKS)), bias_a)


def _kth_largest(count_ge, rows, k):
    def body(it, carry):
        lo, cnt_lo = carry
        cand = lo + lax.shift_left(jnp.int32(1), jnp.int32(31) - it)
        c = count_ge(cand)
        keep = c >= float(k)
        return jnp.where(keep, cand, lo), jnp.where(keep, c, cnt_lo)

    lo0 = jnp.full((rows, 1), -(1 << 31), I32)
    cnt0 = count_ge(lo0)
    return lax.fori_loop(0, 32, body, (lo0, cnt0))


def _tie_limit(count_eq_below, need, rows):
    def body(it, j):
        cand = j + lax.shift_left(jnp.int32(1), jnp.int32(14) - it)
        return jnp.where(count_eq_below(cand) <= need, cand, j)

    return lax.fori_loop(0, 15, body, jnp.zeros((rows, 1), I32))


QB_B = 128
KB_B = 256


def _dsa_kernel(qb_ref, qi_ref, misc_ref, kidx_ref, kb_ref, vb_ref, tb_ref, o_ref,
                key_scr, w_scr, thr_scr, m_scr, acc_scr, *, k_top):
    i = pl.program_id(0)
    n_kb = (i * QB_B) // KB_B + 1
    row = lax.broadcasted_iota(I32, (QB_B, KB_B), 0)
    col = lax.broadcasted_iota(I32, (QB_B, KB_B), 1)

    wi = misc_ref[:, MISC_WI:MISC_WI + H_IDX] * (H_IDX ** -0.5 * D_IDX ** -0.5)
    for h in range(H_IDX):
        w_scr[h] = jnp.broadcast_to(wi[:, h:h + 1], (QB_B, KB_B))

    def score_block(j, _):
        kblk = kidx_ref[pl.ds(pl.multiple_of(j * KB_B, KB_B), KB_B), :][:, :D_IDX]
        acc = jnp.zeros((QB_B, KB_B), F32)
        for h in range(H_IDX):
            isc = _dot_t(qi_ref[:, h * D_IDX:(h + 1) * D_IDX], kblk)
            acc = acc + jnp.maximum(isc, 0.0) * w_scr[h]
        admissible = (j * KB_B + col) // CHUNK <= (i * QB_B + row) // CHUNK
        key_scr[:, pl.ds(pl.multiple_of(j * KB_B, KB_B), KB_B)] = jnp.where(
            admissible, _sortable(acc), jnp.int32(NEG_KEY))
        return 0

    lax.fori_loop(0, n_kb, score_block, 0)

    def count_where(pred):
        def body(j, c):
            blk = key_scr[:, pl.ds(pl.multiple_of(j * KB_B, KB_B), KB_B)]
            return c + jnp.where(pred(blk, j * KB_B + col), 1.0, 0.0)
        c = lax.fori_loop(0, n_kb, body, jnp.zeros((QB_B, KB_B), F32))
        return jnp.sum(c, axis=1, keepdims=True)

    thr, cnt_ge = _kth_largest(lambda cand: count_where(lambda blk, idx: blk >= cand), QB_B, k_top)
    live = thr > jnp.int32(NEG_KEY)
    thr = jnp.maximum(thr, jnp.int32(NEG_KEY))
    thr_scr[0] = jnp.broadcast_to(thr, (QB_B, KB_B))
    thr_scr[1] = jnp.broadcast_to(jnp.where(live, jnp.int32(IDX_BIG), 0), (QB_B, KB_B))
    has_ties = jnp.max(jnp.where(live & (cnt_ge > float(k_top)), 1.0, 0.0)) > 0.0

    @pl.when(has_ties)
    def _():
        cnt_gt = count_where(lambda blk, idx: blk > thr)
        need = float(k_top) - cnt_gt
        jstar = _tie_limit(lambda cand: count_where(lambda blk, idx: (blk == thr) & (idx < cand)), need, QB_B)
        thr_scr[1] = jnp.broadcast_to(jnp.where(live, jstar, 0), (QB_B, KB_B))

    def mask_block(j, _):
        start = pl.multiple_of(j * KB_B, KB_B)
        keys = key_scr[:, pl.ds(start, KB_B)]
        thr_b = thr_scr[0]
        sel = (keys > thr_b) | ((keys == thr_b) & (j * KB_B + col < thr_scr[1]))
        key_scr[:, pl.ds(start, KB_B)] = lax.bitcast_convert_type(jnp.where(sel, 0.0, NEG_INF), I32)
        return 0

    lax.fori_loop(0, n_kb, mask_block, 0)

    odd = i % 2
    table_a = jnp.where(odd == 1, 3, 1)
    table_b = jnp.where(odd == 1, 0, 2)
    n_far = jnp.maximum(n_kb - 2, 0)
    ones = jnp.ones((KB_B, HEAD_DIM), BF16)
    groups = range(H_B_KV)
    heads = [[n * GQA + g for g in range(GQA)] for n in groups]
    q3 = [jnp.concatenate([qb_ref[:, h * HEAD_DIM:(h + 1) * HEAD_DIM] for h in heads[n]], axis=0) * ATTN_SCALE
          for n in groups]

    def scores(n, j, table):
        start = pl.multiple_of(j * KB_B, KB_B)
        negm = lax.bitcast_convert_type(key_scr[:, pl.ds(start, KB_B)], F32)
        k_n = kb_ref[pl.ds(start, KB_B), n * HEAD_DIM:(n + 1) * HEAD_DIM]
        s = _dot_t(q3[n], k_n).reshape(GQA, QB_B, KB_B) + negm[None]
        if table is not None:
            s = s + jnp.stack([tb_ref[table * H_B + h] for h in heads[n]])
        return s

    def max_block(j, table):
        for n in groups:
            s = scores(n, j, table)
            m_scr[n] = jnp.maximum(m_scr[n], jnp.maximum(s[:, :, :LANES], s[:, :, LANES:]))

    m_scr[...] = jnp.full(m_scr.shape, -1e30, F32)
    lax.fori_loop(0, n_far, lambda j, c: (max_block(j, None), c)[1], 0)
    pl.when(n_kb >= 2)(lambda: max_block(n_kb - 2, table_a))
    max_block(n_kb - 1, table_b)
    m_b = [jnp.broadcast_to(jnp.max(m_scr[n], axis=2, keepdims=True), (GQA, QB_B, KB_B)) for n in groups]

    def pv_block(j, table):
        start = pl.multiple_of(j * KB_B, KB_B)
        for n in groups:
            p = jnp.exp(scores(n, j, table) - m_b[n]).astype(BF16).reshape(GQA * QB_B, KB_B)
            v_n = vb_ref[pl.ds(start, KB_B), n * HEAD_DIM:(n + 1) * HEAD_DIM]
            acc_scr[n] += _dot(p, jnp.concatenate([v_n, ones], axis=1))

    acc_scr[...] = jnp.zeros(acc_scr.shape, F32)
    lax.fori_loop(0, n_far, lambda j, c: (pv_block(j, None), c)[1], 0)
    pl.when(n_kb >= 2)(lambda: pv_block(n_kb - 2, table_a))
    pv_block(n_kb - 1, table_b)
    for n in groups:
        for g, h in enumerate(heads[n]):
            a = acc_scr[n, g * QB_B:(g + 1) * QB_B, :]
            o_ref[:, h * HEAD_DIM:(h + 1) * HEAD_DIM] = (a[:, :HEAD_DIM] / a[:, HEAD_DIM:]).astype(BF16)


def _dsa_prompt(p32, pb, s, t5_tables):
    nq = s // QB_B
    k_top = min(TOPK_MAX, s // 4)
    once = pl.Buffered(1)
    return pl.pallas_call(
        functools.partial(_dsa_kernel, k_top=k_top),
        grid=(nq,),
        in_specs=[
            pl.BlockSpec((QB_B, D_B), lambda i: (i, OFF_QB // D_B)),
            pl.BlockSpec((QB_B, H_IDX * D_IDX), lambda i: (i, OFF_QI // (H_IDX * D_IDX))),
            pl.BlockSpec((QB_B, LANES), lambda i: (i, OFF_MISC // LANES)),
            pl.BlockSpec((s, LANES), lambda i: (0, OFF_MISC // LANES), pipeline_mode=once),
            pl.BlockSpec((s, D_B_KV), lambda i: (0, OFF_KB // D_B_KV), pipeline_mode=once),
            pl.BlockSpec((s, D_B_KV), lambda i: (0, OFF_VB // D_B_KV), pipeline_mode=once),
            pl.BlockSpec((4 * H_B, QB_B, KB_B), lambda i: (0, 0, 0), pipeline_mode=once),
        ],
        out_specs=pl.BlockSpec((QB_B, D_B), lambda i: (i, 0)),
        out_shape=jax.ShapeDtypeStruct((s, D_B), BF16),
        scratch_shapes=[
            pltpu.VMEM((QB_B, s), I32),
            pltpu.VMEM((H_IDX, QB_B, KB_B), F32),
            pltpu.VMEM((2, QB_B, KB_B), I32),
            pltpu.VMEM((H_B_KV, GQA, QB_B, LANES), F32),
            pltpu.VMEM((H_B_KV, GQA * QB_B, LANES), F32),
        ],
        compiler_params=_cparams(1),
    )(pb, pb, p32, pb, pb, pb, t5_tables)


QB_C = 512
HEADS_PER_STEP_C = LANES // HEAD_DIM


PREP_TILE_C = 512


def _fox_prep_kernel(q_ref, k_ref, v_ref, cum_ref, qa_ref, ka_ref, va_ref):
    rows = q_ref.shape[0]
    lane = lax.broadcasted_iota(I32, (rows, HEAD_DIM), 1)
    ones = jnp.ones((rows, HEAD_DIM), BF16)
    for h in range(H_C):
        hs = slice(h * HEAD_DIM, (h + 1) * HEAD_DIM)
        c1, c2, c3 = [c.astype(F32) for c in _split3(cum_ref[:, h:h + 1])]
        unit = jnp.where(lane < 6, 1.0, 0.0)
        q_extra = jnp.where(lane < 3, jnp.where(lane == 0, c1, jnp.where(lane == 1, c2, c3)), unit)
        k_extra = jnp.where(lane < 3, unit, -jnp.where(lane == 3, c1, jnp.where(lane == 4, c2, c3)) * unit)
        qa_ref[h] = jnp.concatenate([q_ref[:, hs] * ATTN_SCALE, q_extra.astype(BF16)], axis=1)
        ka_ref[h] = jnp.concatenate([k_ref[:, hs], k_extra.astype(BF16)], axis=1)
        va_ref[h] = jnp.concatenate([v_ref[:, hs], ones], axis=1)


def _fox_prepare(pb, s, cum):
    tm = PREP_TILE_C
    row = lambda off: pl.BlockSpec((tm, D_C), lambda i: (i, off // D_C))
    out = pl.BlockSpec((H_C, tm, LANES), lambda i: (0, i, 0))
    return pl.pallas_call(
        _fox_prep_kernel,
        grid=(s // tm,),
        in_specs=[row(OFF_QC), row(OFF_KC), row(OFF_VC), pl.BlockSpec((tm, 16), lambda i: (i, 0))],
        out_specs=[out, out, out],
        out_shape=[jax.ShapeDtypeStruct((H_C, s, LANES), BF16)] * 3,
        compiler_params=_cparams(1),
    )(pb, pb, pb, cum)


def _fox_kernel(q_ref, k_ref, v_ref, o_ref):
    i = pl.program_id(1)
    row = lax.broadcasted_iota(I32, (QB_C, QB_C), 0)
    col = lax.broadcasted_iota(I32, (QB_C, QB_C), 1)
    heads = range(HEADS_PER_STEP_C)
    q = [q_ref[hh] for hh in heads]

    def scores(hh, j):
        return _dot_t(q[hh], k_ref[hh, pl.ds(pl.multiple_of(j * QB_C, QB_C), QB_C), :])

    def halves_max(s):
        return functools.reduce(jnp.maximum, [s[:, c:c + LANES] for c in range(0, QB_C, LANES)])

    diag = [jnp.where(col <= row, scores(hh, i), NEG_INF) for hh in heads]

    def max_body(j, ms):
        return tuple(jnp.maximum(ms[hh], halves_max(scores(hh, j))) for hh in heads)

    ms = lax.fori_loop(0, i, max_body, tuple(halves_max(diag[hh]) for hh in heads))
    m_b = [jnp.broadcast_to(jnp.max(ms[hh], axis=1, keepdims=True), (QB_C, QB_C)) for hh in heads]

    def pv(hh, s, j):
        p = jnp.exp(s - m_b[hh]).astype(BF16)
        return _dot(p, v_ref[hh, pl.ds(pl.multiple_of(j * QB_C, QB_C), QB_C), :])

    def pv_body(j, accs):
        return tuple(accs[hh] + pv(hh, scores(hh, j), j) for hh in heads)

    accs = lax.fori_loop(0, i, pv_body, tuple(pv(hh, diag[hh], i) for hh in heads))
    o_ref[...] = jnp.concatenate(
        [accs[hh][:, :HEAD_DIM] / accs[hh][:, HEAD_DIM:] for hh in heads], axis=1).astype(BF16)


def _fox_prompt(qa, ka, va, s):
    nq = s // QB_C
    ng = H_C // HEADS_PER_STEP_C
    hp = HEADS_PER_STEP_C
    return pl.pallas_call(
        _fox_kernel,
        grid=(ng, nq),
        in_specs=[
            pl.BlockSpec((hp, QB_C, LANES), lambda g, i: (g, i, 0)),
            pl.BlockSpec((hp, s, LANES), lambda g, i: (g, 0, 0)),
            pl.BlockSpec((hp, s, LANES), lambda g, i: (g, 0, 0)),
        ],
        out_specs=pl.BlockSpec((QB_C, LANES), lambda g, i: (i, g)),
        out_shape=jax.ShapeDtypeStruct((s, D_C), BF16),
        compiler_params=_cparams(2),
    )(qa, ka, va)


def _softmax_pv(s, v_all):
    m = jnp.max(s, axis=1, keepdims=True)
    p = jnp.exp(s - m)
    l = jnp.sum(p, axis=1, keepdims=True)
    return _dot(p.astype(BF16), v_all) / l


def _with_new_rows(cache, new, pad_rows):
    parts = [cache.astype(BF16), new]
    if pad_rows:
        parts.append(jnp.zeros((pad_rows, new.shape[1]), BF16))
    return jnp.concatenate(parts, axis=0)


def _sample_kernel(qa_ref, ka_ref, va_ref, kvb_ref, qi_ref, qb_ref, qc_ref, kc_ref, vc_ref, misc16_ref,
                   misc32_ref, ca_ref, cb_ref, cbi_ref, cc_ref, clf_ref, clft_ref, lfn_ref, lfnt_ref,
                   bias_a_ref, t5_ref, triu_ref, tril_ref,
                   oa_ref, ob_ref, oc_ref, *, t, past, na, k_top):
    la = na + LANES
    lk = past + LANES
    pad = LANES - t

    ca = ca_ref[...]
    ka_all = _with_new_rows(ca[:, :D_A], ka_ref[...], pad)
    va_all = _with_new_rows(ca[:, D_A:], va_ref[...], pad)
    for h in range(H_A):
        hs = slice(h * HEAD_DIM, (h + 1) * HEAD_DIM)
        s = _dot_t(qa_ref[:, hs], ka_all[:, hs]) * ATTN_SCALE + bias_a_ref[h]
        oa_ref[:, hs] = _softmax_pv(s, va_all[:, hs]).astype(BF16)

    col = lax.broadcasted_iota(I32, (t, lk), 1)
    ki_all = _with_new_rows(cbi_ref[...], misc16_ref[:, :D_IDX], pad)
    wi = misc32_ref[:, MISC_WI:MISC_WI + H_IDX] * (H_IDX ** -0.5 * D_IDX ** -0.5)
    acc = jnp.zeros((t, lk), F32)
    for h in range(H_IDX):
        isc = _dot_t(qi_ref[:, h * D_IDX:(h + 1) * D_IDX], ki_all)
        acc = acc + jnp.maximum(isc, 0.0) * wi[:, h:h + 1]
    keys = jnp.where(col < past + t, _sortable(acc), jnp.int32(NEG_KEY))

    def count(pred):
        return jnp.sum(jnp.where(pred, 1.0, 0.0), axis=1, keepdims=True)

    thr, _ = _kth_largest(lambda cand: count(keys >= cand), t, k_top)
    live = thr > jnp.int32(NEG_KEY)
    thr = jnp.maximum(thr, jnp.int32(NEG_KEY))
    need = float(k_top) - count(keys > thr)
    jstar = _tie_limit(lambda cand: count((keys == thr) & (col < cand)), need, t)
    sel = (keys > thr) | ((keys == thr) & (col < jnp.where(live, jstar, 0)))

    cb = cb_ref[...]
    kb_all = _with_new_rows(cb[:, :D_B_KV], kvb_ref[:, :D_B_KV], pad)
    vb_all = _with_new_rows(cb[:, D_B_KV:], kvb_ref[:, D_B_KV:], pad)
    for n in range(H_B_KV):
        ns = slice(n * HEAD_DIM, (n + 1) * HEAD_DIM)
        for g in range(GQA):
            h = n * GQA + g
            hs = slice(h * HEAD_DIM, (h + 1) * HEAD_DIM)
            s = _dot_t(qb_ref[:, hs], kb_all[:, ns]) * ATTN_SCALE + t5_ref[h]
            s = jnp.where(sel, s, NEG_INF)
            ob_ref[:, hs] = _softmax_pv(s, vb_all[:, ns]).astype(BF16)

    cc = cc_ref[...]
    kc_all = _with_new_rows(cc[:, :D_C], kc_ref[...], pad)
    vc_all = _with_new_rows(cc[:, D_C:], vc_ref[...], pad)
    lft = jnp.concatenate([clft_ref[...], lfnt_ref[...]], axis=1)
    t1, t2, t3 = _split3(lft)
    triu = triu_ref[...]
    cum_t = _dot(t1, triu) + _dot(t2, triu) + _dot(t3, triu)
    total = jnp.sum(clf_ref[...], axis=0, keepdims=True)
    n1, n2, n3 = _split3(lfn_ref[...])
    tril = tril_ref[...]
    cum_q = total + _dot(tril, n1) + _dot(tril, n2) + _dot(tril, n3)
    row = lax.broadcasted_iota(I32, (t, lk), 0)
    causal = col <= past + row
    for h in range(H_C):
        hs = slice(h * HEAD_DIM, (h + 1) * HEAD_DIM)
        s = _dot_t(qc_ref[:, hs], kc_all[:, hs]) * ATTN_SCALE + cum_q[:, h:h + 1] - cum_t[h:h + 1, :]
        s = jnp.where(causal, s, NEG_INF)
        oc_ref[:, hs] = _softmax_pv(s, vc_all[:, hs]).astype(BF16)


def _sample_mixers(p32, pb, s, ca, cb, cbi, cc, clf, clft, lfn, lfnt, bias_a, t5_tab, triu, tril, layer):
    nb, na = ca.shape[1], ca.shape[2]
    past = cb.shape[2]
    t = lfn.shape[1]
    k_top = min(TOPK_MAX, (past + t) // 4)
    r0 = s // t
    lk = past + LANES
    la = na + LANES
    row = lambda width, off: pl.BlockSpec((t, width), lambda b: (r0 + b, off // width))
    cache = lambda rows, width: pl.BlockSpec((None, None, rows, width), lambda b: (layer, b, 0, 0))
    full = lambda shape: pl.BlockSpec(shape, lambda b: (0,) * len(shape))
    in_specs = [
        row(D_A, OFF_QA), row(D_A, OFF_KA), row(D_A, OFF_VA), row(2 * D_B_KV, OFF_KB),
        row(H_IDX * D_IDX, OFF_QI), row(D_B, OFF_QB), row(D_C, OFF_QC), row(D_C, OFF_KC), row(D_C, OFF_VC),
        row(LANES, OFF_MISC), row(LANES, OFF_MISC),
        cache(na, 2 * D_A), cache(past, 2 * D_B_KV), cache(past, D_IDX), cache(past, 2 * D_C),
        cache(past, 16), cache(16, past),
        pl.BlockSpec((None, t, 16), lambda b: (b, 0, 0)),
        pl.BlockSpec((None, 16, LANES), lambda b: (b, 0, 0)),
        full((H_A, t, la)), full((H_B, t, lk)), full((lk, lk)), full((t, t)),
    ]
    out = lambda width: pl.BlockSpec((t, width), lambda b: (b, 0))
    return pl.pallas_call(
        functools.partial(_sample_kernel, t=t, past=past, na=na, k_top=k_top),
        grid=(nb,),
        in_specs=in_specs,
        out_specs=[out(D_A), out(D_B), out(D_C)],
        out_shape=[jax.ShapeDtypeStruct((nb * t, w), BF16) for w in (D_A, D_B, D_C)],
        compiler_params=_cparams(1),
    )(pb, pb, pb, pb, pb, pb, pb, pb, pb, pb, p32, ca, cb, cbi, cc, clf, clft, lfn, lfnt,
      bias_a, t5_tab, triu, tril)


def _outproj_kernel(x_ref, g_ref, ap, bp, cp, as_, bs, cs, w_ref, o_ref, *, n_prompt_blocks):
    i = pl.program_id(0)

    def run(a, b, c):
        y = (_dot(a[...], w_ref[:D_A, :]) + _dot(b[...], w_ref[D_A:D_A + D_B, :])
             + _dot(c[...], w_ref[D_A + D_B:, :]))
        o_ref[...] = x_ref[...] + g_ref[...] * y

    pl.when(i < n_prompt_blocks)(lambda: run(ap, bp, cp))
    pl.when(i >= n_prompt_blocks)(lambda: run(as_, bs, cs))


def _out_project(x, gate, mix_p, mix_s, w_out_b, layer, n_prompt_blocks):
    m = x.shape[0]
    tm = TOKEN_TILE
    last_p = n_prompt_blocks - 1
    p_map = lambda i: (jnp.minimum(i, last_p), 0)
    s_map = lambda i: (jnp.maximum(i - n_prompt_blocks, 0), 0)
    widths = (D_A, D_B, D_C)
    return pl.pallas_call(
        functools.partial(_outproj_kernel, n_prompt_blocks=n_prompt_blocks),
        grid=(m // tm,),
        in_specs=[
            pl.BlockSpec((tm, D_MODEL), lambda i: (i, 0)),
            pl.BlockSpec((tm, D_MODEL), lambda i: (_mod_block_index(i, n_prompt_blocks), 0)),
            *[pl.BlockSpec((tm, w), p_map) for w in widths],
            *[pl.BlockSpec((tm, w), s_map) for w in widths],
            pl.BlockSpec((None, D_MODEL, D_MODEL), lambda i: (layer, 0, 0)),
        ],
        out_specs=pl.BlockSpec((tm, D_MODEL), lambda i: (i, 0)),
        out_shape=jax.ShapeDtypeStruct((m, D_MODEL), F32),
        compiler_params=_cparams(1),
    )(x, gate, *mix_p, *mix_s, w_out_b)


def _lane_pick(vals, lane, idx):
    return jnp.sum(jnp.where(lane == idx, vals, 0.0), axis=1, keepdims=True)


def _first_argmax(vals, lane):
    m = jnp.max(vals, axis=1, keepdims=True)
    idx = jnp.min(jnp.where(vals == m, lane, float(LANES)), axis=1, keepdims=True)
    return m, idx


def _router_kernel(x_ref, gain_ref, sc_ref, sh_ref, wr_ref, br_ref, tri_ref, h_ref, meta_ref, cnt_ref, carry):
    @pl.when(pl.program_id(0) == 0)
    def _():
        carry[...] = jnp.zeros_like(carry)

    h = _norm_mod(x_ref[...], gain_ref[...], sc_ref[...], sh_ref[...])
    hb = h.astype(BF16)
    h_ref[...] = h
    scores = jax.nn.sigmoid(_dot(hb, wr_ref[...]))
    lane_i = lax.broadcasted_iota(I32, scores.shape, 1)
    lane = lane_i.astype(F32)
    sel = jnp.where(lane_i < N_EXPERTS, scores + br_ref[...], NEG_INF)
    group = (lane_i // (N_EXPERTS // N_GROUPS)).astype(F32)

    best = None
    for g in range(N_GROUPS):
        in_g = jnp.where(group == float(g), sel, NEG_INF)
        m1, i1 = _first_argmax(in_g, lane)
        m2 = jnp.max(jnp.where(lane == i1, NEG_INF, in_g), axis=1, keepdims=True)
        gs = m1 + m2
        if best is None:
            best, gbest = gs, jnp.zeros_like(i1)
        else:
            better = gs > best
            best = jnp.where(better, gs, best)
            gbest = jnp.where(better, float(g), gbest)

    in_best = jnp.where(group == gbest, sel, NEG_INF)
    _, e0 = _first_argmax(in_best, lane)
    _, e1 = _first_argmax(jnp.where(lane == e0, NEG_INF, in_best), lane)
    w0 = _lane_pick(scores, lane, e0)
    w1 = _lane_pick(scores, lane, e1)
    wsum = w0 + w1

    onehot = jnp.where((lane == e0) | (lane == e1), 1.0, 0.0)
    before = _dot(tri_ref[...], onehot.astype(BF16)) + carry[...]
    r0 = _lane_pick(before, lane, e0)
    r1 = _lane_pick(before, lane, e1)
    carry[...] = carry[...] + jnp.sum(onehot, axis=0, keepdims=True)
    cnt_ref[...] = carry[...]

    meta = jnp.zeros(scores.shape, F32)
    for k, v in enumerate((e0, e1, r0, r1, w0 / wsum, w1 / wsum)):
        meta = jnp.where(lane_i == k, v, meta)
    meta_ref[...] = meta


def _router(x, gain, sc, sh, w_router_p, b_router_p, tri, n_prompt_blocks):
    m = x.shape[0]
    tm = TOKEN_TILE
    mod_map = lambda i: (_mod_block_index(i, n_prompt_blocks), 0)
    return pl.pallas_call(
        _router_kernel,
        grid=(m // tm,),
        in_specs=[
            pl.BlockSpec((tm, D_MODEL), lambda i: (i, 0)),
            pl.BlockSpec((1, D_MODEL), lambda i: (0, 0)),
            pl.BlockSpec((tm, D_MODEL), mod_map),
            pl.BlockSpec((tm, D_MODEL), mod_map),
            pl.BlockSpec((D_MODEL, LANES), lambda i: (0, 0)),
            pl.BlockSpec((1, LANES), lambda i: (0, 0)),
            pl.BlockSpec((tm, tm), lambda i: (0, 0)),
        ],
        out_specs=[
            pl.BlockSpec((tm, D_MODEL), lambda i: (i, 0)),
            pl.BlockSpec((tm, LANES), lambda i: (i, 0)),
            pl.BlockSpec((1, LANES), lambda i: (0, 0)),
        ],
        out_shape=[
            jax.ShapeDtypeStruct((m, D_MODEL), F32),
            jax.ShapeDtypeStruct((m, LANES), F32),
            jax.ShapeDtypeStruct((1, LANES), F32),
        ],
        scratch_shapes=[pltpu.VMEM((1, LANES), F32)],
        compiler_params=_cparams(1),
    )(x, gain, sc, sh, w_router_p, b_router_p, tri)


EXPERT_ROWS = 256
EXPERT_TF = 512


def _row_copy(src_hbm, src_row, dst_buf, dst_row, sem):
    return pltpu.make_async_copy(src_hbm.at[pl.ds(src_row, 1), :], dst_buf.at[pl.ds(dst_row, 1), :], sem)


def _expert_kernel(be_ref, nu_ref, st_ref, h_hbm, wg_ref, wu_ref, wd_ref, o_ref, xrows, x16, sems, *, n_tok):
    b = pl.program_id(0)
    f = pl.program_id(1)
    n_used = nu_ref[0]

    def for_rows(blk, fn):
        slot = blk % 2

        def body(r, _):
            tok = st_ref[blk * EXPERT_ROWS + r]

            @pl.when(tok < n_tok)
            def _():
                fn(_row_copy(h_hbm, tok, xrows.at[slot], r, sems.at[slot]))
            return 0

        lax.fori_loop(0, EXPERT_ROWS, body, 0)

    @pl.when((f == 0) & (b < n_used))
    def _():
        @pl.when(b == 0)
        def _():
            xrows[...] = jnp.zeros(xrows.shape, F32)
            for_rows(b, lambda cp: cp.start())

        for_rows(b, lambda cp: cp.wait())

        @pl.when(b + 1 < n_used)
        def _():
            for_rows(b + 1, lambda cp: cp.start())

        x16[...] = xrows[b % 2].astype(BF16)

    @pl.when((f == 0) & (b >= n_used))
    def _():
        o_ref[...] = jnp.zeros(o_ref.shape, F32)

    @pl.when(b < n_used)
    def _():
        x = x16[...]
        a = _dot(x, wg_ref[...].astype(BF16))
        u = _dot(x, wu_ref[...].astype(BF16))
        act = (a * jax.nn.sigmoid(a) * u).astype(BF16)
        y = _dot(act, wd_ref[...].astype(BF16))

        @pl.when(f == 0)
        def _():
            o_ref[...] = y

        @pl.when(f > 0)
        def _():
            o_ref[...] += y


def _experts(h2, slot_tok, block_e, n_used, w_g, w_u, w_d, layer):
    nslots = slot_tok.shape[0]
    nb = nslots // EXPERT_ROWS
    nf = D_EXPERT // EXPERT_TF

    def blk(b, nu):
        return jnp.minimum(b, nu[0] - 1)

    def fidx(b, f, nu):
        return jnp.where(b < nu[0], f, nf - 1)

    grid_spec = pltpu.PrefetchScalarGridSpec(
        num_scalar_prefetch=3,
        grid=(nb, nf),
        in_specs=[
            pl.BlockSpec(memory_space=pl.ANY),
            pl.BlockSpec((None, None, D_MODEL, EXPERT_TF),
                         lambda b, f, be, nu, st: (layer, be[blk(b, nu)], 0, fidx(b, f, nu))),
            pl.BlockSpec((None, None, D_MODEL, EXPERT_TF),
                         lambda b, f, be, nu, st: (layer, be[blk(b, nu)], 0, fidx(b, f, nu))),
            pl.BlockSpec((None, None, EXPERT_TF, D_MODEL),
                         lambda b, f, be, nu, st: (layer, be[blk(b, nu)], fidx(b, f, nu), 0)),
        ],
        out_specs=pl.BlockSpec((EXPERT_ROWS, D_MODEL), lambda b, f, be, nu, st: (b, 0)),
        scratch_shapes=[
            pltpu.VMEM((2, EXPERT_ROWS, D_MODEL), F32),
            pltpu.VMEM((EXPERT_ROWS, D_MODEL), BF16),
            pltpu.SemaphoreType.DMA((2,)),
        ],
    )
    return pl.pallas_call(
        functools.partial(_expert_kernel, n_tok=h2.shape[0]),
        grid_spec=grid_spec,
        out_shape=jax.ShapeDtypeStruct((nslots, D_MODEL), F32),
        compiler_params=_cparams(2),
    )(block_e, n_used, slot_tok, h2, w_g, w_u, w_d)


COMBINE_TILE = 256


def _combine_kernel(d0_ref, d1_ref, x_ref, g_ref, meta_ref, y_hbm, o_ref, ybuf, sems):
    i = pl.program_id(0)
    n = pl.num_programs(0)

    def for_rows(blk, fn):
        slot = blk % 2

        def body(r, _):
            t = blk * COMBINE_TILE + r
            fn(_row_copy(y_hbm, d0_ref[t], ybuf.at[slot, 0], r, sems.at[slot]))
            fn(_row_copy(y_hbm, d1_ref[t], ybuf.at[slot, 1], r, sems.at[slot]))
            return 0

        lax.fori_loop(0, COMBINE_TILE, body, 0)

    @pl.when(i == 0)
    def _():
        for_rows(i, lambda cp: cp.start())

    for_rows(i, lambda cp: cp.wait())

    @pl.when(i + 1 < n)
    def _():
        for_rows(i + 1, lambda cp: cp.start())

    slot = i % 2
    moe = ybuf[slot, 0] * meta_ref[:, 4:5] + ybuf[slot, 1] * meta_ref[:, 5:6]
    o_ref[...] = x_ref[...] + g_ref[...] * moe


def _combine(x, gate, meta, yb, dest0, dest1, n_prompt_blocks):
    m = x.shape[0]
    tm = COMBINE_TILE
    ratio = TOKEN_TILE // tm
    mod_map = lambda i, d0, d1: (jnp.where(i < n_prompt_blocks * ratio, 0, i - (n_prompt_blocks - 1) * ratio), 0)
    row = lambda width: pl.BlockSpec((tm, width), lambda i, d0, d1: (i, 0))
    grid_spec = pltpu.PrefetchScalarGridSpec(
        num_scalar_prefetch=2,
        grid=(m // tm,),
        in_specs=[row(D_MODEL), pl.BlockSpec((tm, D_MODEL), mod_map), row(LANES), pl.BlockSpec(memory_space=pl.ANY)],
        out_specs=row(D_MODEL),
        scratch_shapes=[pltpu.VMEM((2, 2, tm, D_MODEL), F32), pltpu.SemaphoreType.DMA((2,))],
    )
    return pl.pallas_call(
        _combine_kernel,
        grid_spec=grid_spec,
        out_shape=jax.ShapeDtypeStruct((m, D_MODEL), F32),
        compiler_params=_cparams(1),
    )(dest0, dest1, x, gate, meta, yb)


def _t5_bucket(rel):
    nb = T5_BUCKETS // 2
    max_exact = nb // 2
    ret = jnp.where(rel < 0, nb, 0)
    n = jnp.abs(rel)
    nf = jnp.maximum(n, 1).astype(F32)
    large = max_exact + (jnp.log(nf / max_exact) / math.log(T5_MAX_DIST / max_exact) * (nb - max_exact)).astype(I32)
    large = jnp.minimum(large, nb - 1)
    return ret + jnp.where(n < max_exact, n, large)


def _band_bias(table, qpos, kpos, valid):
    rel = qpos[:, None] - kpos[None, :]
    bias = table.astype(F32)[:, jnp.clip(rel, -A_REL_FUTURE, A_REL_PAST) + A_REL_FUTURE]
    qc = qpos[:, None] // CHUNK
    kc = kpos[None, :] // CHUNK
    ok = valid[None, :] & (kc <= qc) & (kc >= qc - A_LEFT_CHUNKS)
    return jnp.where(ok[None], bias, NEG_INF)


def _t5_bias(t5, qpos, kpos):
    return jnp.moveaxis(t5.astype(F32)[_t5_bucket(qpos[:, None] - kpos[None, :])], 2, 0)


def _relayout_w_in(w_in_l):
    qa, ka, va, qb, kb, vb, qi, ki, wi, qc, kc, vc, fc = jnp.split(w_in_l, PROJ_SPLITS, axis=1)
    pad = jnp.zeros((D_MODEL, P_COLS - OFF_MISC - D_IDX - H_IDX - H_C), w_in_l.dtype)
    w = jnp.concatenate([qa, ka, va, kb, vb, qi, qb, qc, kc, vc, ki, wi, fc, pad], axis=1).astype(BF16)
    return jnp.swapaxes(w.reshape(D_MODEL, P_COLS // PROJ_TN, PROJ_TN), 0, 1)


def _column_params(qk_gain_l, b_forget_l):
    ones = lambda n: jnp.ones((n,), F32)
    g = qk_gain_l.astype(F32)
    gq = jnp.concatenate([
        jnp.tile(g[0], H_A), jnp.tile(g[1], H_A), ones(D_A),
        jnp.tile(g[3], H_B_KV), ones(D_B_KV), ones(H_IDX * D_IDX),
        jnp.tile(g[2], H_B), jnp.tile(g[4], H_C), jnp.tile(g[5], H_C), ones(D_C), ones(P_COLS - OFF_MISC)])
    bq = jnp.zeros((P_COLS,), F32).at[OFF_MISC + MISC_FC:OFF_MISC + MISC_FC + H_C].set(b_forget_l.astype(F32))
    return gq[None, :], bq[None, :]


def _tri(n, *, strict=False, upper=False):
    r = jnp.arange(n)[:, None]
    c = jnp.arange(n)[None, :]
    m = (r < c if strict else r <= c) if upper else (c < r if strict else c <= r)
    return m.astype(BF16)


def kernel(x_prompt, x_sample, c_prompt, c_sample, cache_a_kv, cache_b_kv, cache_b_kidx, cache_c_kv, cache_c_logf,
           w_ada, b_ada, norm_gain, w_in, b_forget, qk_gain, rel_bias_a, t5_bias, w_out, w_router, b_router,
           w_e_gate, w_e_up, w_e_down):
    depth = w_in.shape[0]
    bp, s, d = x_prompt.shape
    nb, t, _ = x_sample.shape
    ns = nb * t
    tm = TOKEN_TILE
    assert bp == 1 and d == D_MODEL and s % tm == 0 and ns % tm == 0 and s % QB_C == 0
    n_pb = s // tm
    m = s + ns
    na = cache_a_kv.shape[2]
    past = cache_b_kv.shape[2]
    keep = min(A_LEFT_CHUNKS * CHUNK, s)

    x = jnp.concatenate([x_prompt.reshape(s, d), x_sample.reshape(ns, d)], axis=0)
    c_all = jnp.concatenate([c_prompt, c_sample, jnp.zeros((-(bp + nb) % 8, d), F32)], axis=0)

    q128 = jnp.arange(QB_A)
    band_k = jnp.arange(A_KEY_BLOCKS * QB_A) - (A_KEY_BLOCKS - 1) * QB_A
    qpos_s = past + jnp.arange(t)
    kpos_a = jnp.concatenate([jnp.arange(past - na, past), qpos_s, jnp.zeros((LANES - t,), I32)])
    valid_a = jnp.arange(na + LANES) < na + t
    kpos_b = jnp.concatenate([jnp.arange(past), qpos_s, jnp.zeros((LANES - t,), I32)])
    t5_far = t5_bias.astype(F32)[T5_BUCKETS // 2 - 1]
    kcol = jnp.arange(KB_B)
    t5_tables = jnp.concatenate(
        [_t5_bias(t5_bias, q128 + off, kcol) - t5_far[:, None, None] for off in (QB_B, 2 * QB_B, 0)]
        + [jnp.zeros((H_B, QB_B, KB_B), F32)], axis=0)
    t5_s = _t5_bias(t5_bias, qpos_s, kpos_b)
    bd = ((jnp.arange(PROJ_TN)[:, None] // HEAD_DIM == jnp.arange(PROJ_TN)[None, :] // HEAD_DIM)
          .astype(F32) / HEAD_DIM).astype(BF16)
    tri_cum = _tri(CUM_TILE)
    tri_rank = _tri(tm, strict=True)
    triu_s = _tri(past + LANES, upper=True)
    tril_s = _tri(t)
    w_out_b = w_out.astype(BF16)
    w_router_p = jnp.pad(w_router, ((0, 0), (0, LANES - N_EXPERTS))).astype(BF16)
    b_router_p = jnp.pad(b_router.astype(F32), (0, LANES - N_EXPERTS))[None, :]

    ca = cache_a_kv.reshape(depth, nb, na, 2 * D_A)
    cb = cache_b_kv.reshape(depth, nb, past, 2 * D_B_KV)
    cc = cache_c_kv.reshape(depth, nb, past, 2 * D_C)
    clf = jnp.pad(cache_c_logf.astype(F32), ((0, 0), (0, 0), (0, 0), (0, 16 - H_C)))
    clft = jnp.swapaxes(clf, 2, 3)

    n_asg = 2 * m
    n_eb = n_asg // EXPERT_ROWS + N_EXPERTS
    tok = jnp.arange(m, dtype=I32)

    states_p, states_s = [], []
    for l in range(depth):
        mod = _ada(c_all, w_ada, b_ada, l)
        mods = []
        for part in jnp.split(mod, 6, axis=1):
            mods.append(jnp.concatenate([jnp.broadcast_to(part[:1], (tm, d)), jnp.repeat(part[bp:bp + nb], t, axis=0)], 0))
        sh1, sc1, g1, sh2, sc2, g2 = mods

        gq, bq = _column_params(qk_gain[l], b_forget[l])
        p32, pb = _project(x, norm_gain[l, 0][None, :], sc1, sh1, _relayout_w_in(w_in[l]), gq, bq, bd, n_pb)

        bias_a_p = _band_bias(rel_bias_a[l], q128 + (A_KEY_BLOCKS - 1) * QB_A, band_k + (A_KEY_BLOCKS - 1) * QB_A,
                              jnp.ones((A_KEY_BLOCKS * QB_A,), bool))
        oa_p = _band_prompt(pb, s, bias_a_p)
        ob_p = _dsa_prompt(p32, pb, s, t5_tables)
        cum = _cumsum_rows(p32, s, tri_cum)[:, MISC_FC:MISC_FC + 16]
        oc_p = _fox_prompt(*_fox_prepare(pb, s, cum), s)

        lfn = p32[s:, OFF_MISC + MISC_FC:OFF_MISC + MISC_FC + 16].reshape(nb, t, 16)
        lfnt = jnp.pad(jnp.swapaxes(lfn, 1, 2), ((0, 0), (0, 0), (0, LANES - t)))
        bias_a_s = _band_bias(rel_bias_a[l], qpos_s, kpos_a, valid_a)
        oa_s, ob_s, oc_s = _sample_mixers(p32, pb, s, ca, cb, cache_b_kidx, cc, clf, clft, lfn, lfnt,
                                          bias_a_s, t5_s, triu_s, tril_s, l)

        x = _out_project(x, g1, (oa_p, ob_p, oc_p), (oa_s, ob_s, oc_s), w_out_b, l, n_pb)

        h2, meta, counts = _router(x, norm_gain[l, 1][None, :], sc2, sh2, w_router_p, b_router_p, tri_rank, n_pb)
        e0 = meta[:, 0].astype(I32)
        e1 = meta[:, 1].astype(I32)
        counts = counts[0, :N_EXPERTS].astype(I32)
        padded = (counts + EXPERT_ROWS - 1) // EXPERT_ROWS * EXPERT_ROWS
        pend = jnp.cumsum(padded)
        pstart = pend - padded
        dest0 = pstart[e0] + meta[:, 2].astype(I32)
        dest1 = pstart[e1] + meta[:, 3].astype(I32)
        slot_tok = jnp.full((n_eb * EXPERT_ROWS,), m, I32).at[dest0].set(tok).at[dest1].set(tok)
        block_e = jnp.minimum(jnp.searchsorted(pend, jnp.arange(n_eb, dtype=I32) * EXPERT_ROWS, side='right'),
                              N_EXPERTS - 1).astype(I32)
        n_used = (pend[-1:] // EXPERT_ROWS).astype(I32)
        yb = _experts(h2, slot_tok, block_e, n_used, w_e_gate, w_e_up, w_e_down, l)
        x = _combine(x, g2, meta, yb, dest0, dest1, n_pb)

        def states(rows, nbatch, a_rows):
            r = p32[rows]
            n = r.shape[0] // nbatch
            kv = lambda off, heads: r[:, off:off + 2 * heads * HEAD_DIM].reshape(nbatch, n, 2, heads, HEAD_DIM)
            return (kv(OFF_KA, H_A)[:, n - a_rows:], kv(OFF_KB, H_B_KV),
                    r[:, OFF_MISC:OFF_MISC + D_IDX].reshape(nbatch, n, D_IDX), kv(OFF_KC, H_C),
                    r[:, OFF_MISC + MISC_FC:OFF_MISC + MISC_FC + H_C].reshape(nbatch, n, H_C))

        states_p.append(states(slice(0, s), bp, keep))
        states_s.append(states(slice(s, m), nb, t))

    stk = lambda sts, i: jnp.stack([st[i] for st in sts], axis=0)
    return (x[:s].reshape(bp, s, d), x[s:].reshape(nb, t, d),
            *[stk(states_p, i) for i in range(5)], *[stk(states_s, i) for i in range(5)])
```

```python
import functools
import math

import jax
import jax.numpy as jnp
from jax import lax
from jax.experimental import pallas as pl
from jax.experimental.pallas import tpu as pltpu

F32 = jnp.float32
BF16 = jnp.bfloat16
I32 = jnp.int32

D_MODEL = 2048
HEAD_DIM = 64
CHUNK = 64
EPS = 1e-6
ATTN_SCALE = HEAD_DIM ** -0.5
H_A = 8
A_LEFT_CHUNKS = 8
A_REL_PAST = 128
A_REL_FUTURE = CHUNK - 1
H_B = 12
H_B_KV = 4
GQA = H_B // H_B_KV
H_IDX = 16
D_IDX = 64
TOPK_MAX = 256
H_C = 12
T5_BUCKETS = 32
T5_MAX_DIST = 128
N_EXPERTS = 32
N_GROUPS = 4
D_EXPERT = 1024

D_A = H_A * HEAD_DIM
D_B = H_B * HEAD_DIM
D_B_KV = H_B_KV * HEAD_DIM
D_C = H_C * HEAD_DIM
PROJ_SIZES = (D_A, D_A, D_A, D_B, D_B_KV, D_B_KV, H_IDX * D_IDX, D_IDX, H_IDX, D_C, D_C, D_C, H_C)
PROJ_SPLITS = tuple(sum(PROJ_SIZES[:i + 1]) for i in range(len(PROJ_SIZES) - 1))

OFF_QA, OFF_KA, OFF_VA = 0, 512, 1024
OFF_KB, OFF_VB = 1536, 1792
OFF_QI = 2048
OFF_QB = 3072
OFF_QC, OFF_KC, OFF_VC = 3840, 4608, 5376
OFF_MISC = 6144
MISC_WI = 64
MISC_FC = 80
P_COLS = 6400
PROJ_TN = 256
NORM_COL_BLOCKS = (0, 1, 2, 3, 6, 12, 13, 14, 15, 16, 17, 18, 19, 20)
MISC_COL_BLOCK = OFF_MISC // PROJ_TN

TOKEN_TILE = 512
LANES = 128
VMEM_LIMIT = 56 * 1024 * 1024

NEG_INF = float("-inf")
NEG_KEY = (0xFF800000 ^ 0x7FFFFFFF) - (1 << 32)
IDX_BIG = 1 << 30


def _cparams(n_axes):
    return pltpu.CompilerParams(dimension_semantics=("arbitrary",) * n_axes, vmem_limit_bytes=VMEM_LIMIT)


def _split3(x):
    x1 = x.astype(BF16)
    r1 = x - x1.astype(F32)
    x2 = r1.astype(BF16)
    r2 = r1 - x2.astype(F32)
    return x1, x2, r2.astype(BF16)


def _dot(a, b):
    return jnp.dot(a, b, preferred_element_type=F32)


def _dot_t(a, b):
    return lax.dot_general(a, b, (((1,), (1,)), ((), ())), preferred_element_type=F32)


def _sortable(x):
    b = lax.bitcast_convert_type(x, I32)
    return jnp.where(b < 0, b ^ jnp.int32(0x7FFFFFFF), b)


def _ada_kernel(c_ref, w_ref, b_ref, o_ref):
    c = c_ref[...]
    a = (c * jax.nn.sigmoid(c)).astype(BF16)
    o_ref[...] = _dot(a, w_ref[...].astype(BF16)) + b_ref[...]


def _ada(c_all, w_ada, b_ada, layer):
    rows = c_all.shape[0]
    n_out = w_ada.shape[2]
    tn = 1024
    return pl.pallas_call(
        _ada_kernel,
        grid=(n_out // tn,),
        in_specs=[
            pl.BlockSpec((rows, D_MODEL), lambda n: (0, 0)),
            pl.BlockSpec((None, D_MODEL, tn), lambda n: (layer, 0, n)),
            pl.BlockSpec((None, 1, tn), lambda n: (layer, 0, n)),
        ],
        out_specs=pl.BlockSpec((rows, tn), lambda n: (0, n)),
        out_shape=jax.ShapeDtypeStruct((rows, n_out), F32),
        compiler_params=_cparams(1),
    )(c_all, w_ada, b_ada.reshape(b_ada.shape[0], 1, n_out))


def _mod_block_index(i, n_prompt_blocks):
    return jnp.where(i < n_prompt_blocks, 0, i - n_prompt_blocks + 1)


def _norm_mod(x, gain, sc, sh):
    ms = jnp.mean(x * x, axis=-1, keepdims=True)
    return (x * lax.rsqrt(ms + EPS) * gain) * (1.0 + sc) + sh


def _proj_kernel(x_ref, gain_ref, sc_ref, sh_ref, w_ref, gq_ref, bq_ref, bd_ref, o32_ref, o16_ref, h_scr):
    n = pl.program_id(1)

    @pl.when(n == 0)
    def _():
        h_scr[...] = _norm_mod(x_ref[...], gain_ref[...], sc_ref[...], sh_ref[...]).astype(BF16)

    y = _dot(h_scr[...], w_ref[...])

    is_norm = functools.reduce(jnp.logical_or, [n == b for b in NORM_COL_BLOCKS])
    is_misc = n == MISC_COL_BLOCK

    def emit(v):
        o32_ref[...] = v
        o16_ref[...] = v.astype(BF16)

    @pl.when(is_norm)
    def _():
        s1, s2, s3 = _split3(y * y)
        bd = bd_ref[...]
        ms = _dot(s1, bd) + _dot(s2, bd) + _dot(s3, bd)
        emit(y * lax.rsqrt(ms + EPS) * gq_ref[...])

    @pl.when(is_misc)
    def _():
        lane = lax.broadcasted_iota(I32, y.shape, 1)
        z = y + bq_ref[...]
        logsig = jnp.minimum(z, 0.0) - jnp.log(1.0 + jnp.exp(-jnp.abs(z)))
        emit(jnp.where((lane >= MISC_FC) & (lane < MISC_FC + H_C), logsig, y))

    @pl.when(jnp.logical_not(jnp.logical_or(is_norm, is_misc)))
    def _():
        emit(y)


def _project(x, gain, sc, sh, w_r, gq, bq, bd, n_prompt_blocks):
    m = x.shape[0]
    tm = TOKEN_TILE
    mod_map = lambda i, n: (_mod_block_index(i, n_prompt_blocks), 0)
    return pl.pallas_call(
        _proj_kernel,
        grid=(m // tm, P_COLS // PROJ_TN),
        in_specs=[
            pl.BlockSpec((tm, D_MODEL), lambda i, n: (i, 0)),
            pl.BlockSpec((1, D_MODEL), lambda i, n: (0, 0)),
            pl.BlockSpec((tm, D_MODEL), mod_map),
            pl.BlockSpec((tm, D_MODEL), mod_map),
            pl.BlockSpec((None, D_MODEL, PROJ_TN), lambda i, n: (n, 0, 0)),
            pl.BlockSpec((1, PROJ_TN), lambda i, n: (0, n)),
            pl.BlockSpec((1, PROJ_TN), lambda i, n: (0, n)),
            pl.BlockSpec((PROJ_TN, PROJ_TN), lambda i, n: (0, 0)),
        ],
        out_specs=[
            pl.BlockSpec((tm, PROJ_TN), lambda i, n: (i, n)),
            pl.BlockSpec((tm, PROJ_TN), lambda i, n: (i, n)),
        ],
        out_shape=[jax.ShapeDtypeStruct((m, P_COLS), F32), jax.ShapeDtypeStruct((m, P_COLS), BF16)],
        scratch_shapes=[pltpu.VMEM((tm, D_MODEL), BF16)],
        compiler_params=_cparams(2),
    )(x, gain, sc, sh, w_r, gq, bq, bd)


CUM_TILE = 256


def _cumsum_kernel(x_ref, tri_ref, o_ref, carry):
    @pl.when(pl.program_id(0) == 0)
    def _():
        carry[...] = jnp.zeros_like(carry)

    x1, x2, x3 = _split3(x_ref[...])
    tri = tri_ref[...]
    c = _dot(tri, x1) + _dot(tri, x2) + _dot(tri, x3) + carry[...]
    o_ref[...] = c
    carry[...] = c[CUM_TILE - 1:CUM_TILE, :]


def _cumsum_rows(p32, s, tri):
    return pl.pallas_call(
        _cumsum_kernel,
        grid=(s // CUM_TILE,),
        in_specs=[
            pl.BlockSpec((CUM_TILE, LANES), lambda i: (i, OFF_MISC // LANES)),
            pl.BlockSpec((CUM_TILE, CUM_TILE), lambda i: (0, 0)),
        ],
        out_specs=pl.BlockSpec((CUM_TILE, LANES), lambda i: (i, 0)),
        out_shape=jax.ShapeDtypeStruct((s, LANES), F32),
        scratch_shapes=[pltpu.VMEM((1, LANES), F32)],
        compiler_params=_cparams(1),
    )(p32, tri)


QB_A = 128
A_KEY_BLOCKS = 5


def _band_kernel(q_ref, *refs):
    k_refs = refs[:A_KEY_BLOCKS]
    v_refs = refs[A_KEY_BLOCKS:2 * A_KEY_BLOCKS]
    bias_ref, o_ref = refs[2 * A_KEY_BLOCKS:]
    i = pl.program_id(0)
    for h in range(H_A):
        hs = slice(h * HEAD_DIM, (h + 1) * HEAD_DIM)
        q = q_ref[:, hs]
        parts = []
        for j in range(A_KEY_BLOCKS):
            s = _dot_t(q, k_refs[j][:, hs]) * ATTN_SCALE + bias_ref[h, :, j * QB_A:(j + 1) * QB_A]
            in_range = i - (A_KEY_BLOCKS - 1) + j >= 0
            parts.append(jnp.where(in_range, s, NEG_INF))
        m = functools.reduce(jnp.maximum, [jnp.max(s, axis=1, keepdims=True) for s in parts])
        l = jnp.zeros((QB_A, 1), F32)
        o = jnp.zeros((QB_A, HEAD_DIM), F32)
        for j in range(A_KEY_BLOCKS):
            p = jnp.exp(parts[j] - m)
            l = l + jnp.sum(p, axis=1, keepdims=True)
            o = o + _dot(p.astype(BF16), v_refs[j][:, hs])
        o_ref[:, hs] = (o / l).astype(BF16)


def _band_prompt(pb, s, bias_a):
    nq = s // QB_A
    back = A_KEY_BLOCKS - 1

    def kv_spec(j, col):
        return pl.BlockSpec((QB_A, D_A), lambda i: (jnp.maximum(i - back + j, 0), col))

    in_specs = [pl.BlockSpec((QB_A, D_A), lambda i: (i, OFF_QA // D_A))]
    in_specs += [kv_spec(j, OFF_KA // D_A) for j in range(A_KEY_BLOCKS)]
    in_specs += [kv_spec(j, OFF_VA // D_A) for j in range(A_KEY_BLOCKS)]
    in_specs += [pl.BlockSpec((H_A, QB_A, A_KEY_BLOCKS * QB_A), lambda i: (0, 0, 0))]
    return pl.pallas_call(
        _band_kernel,
        grid=(nq,),
        in_specs=in_specs,
        out_specs=pl.BlockSpec((QB_A, D_A), lambda i: (i, 0)),
        out_shape=jax.ShapeDtypeStruct((s, D_A), BF16),
        compiler_params=_cparams(1),
    )(pb, *([pb] * (2 * A_KEY_BLOCKS)), bias_a)


def _kth_largest(count_ge, rows, k):
    def body(carry):
        it, lo, cnt_lo = carry
        cand = lo + lax.shift_left(jnp.int32(1), jnp.int32(31) - it)
        c = count_ge(cand)
        keep = c >= float(k)
        return it + 1, jnp.where(keep, cand, lo), jnp.where(keep, c, cnt_lo)

    def unsettled(carry):
        it, _, cnt_lo = carry
        return (it < 32) & (jnp.max(jnp.abs(cnt_lo - float(k))) > 0.0)

    lo0 = jnp.full((rows, 1), -(1 << 31), I32)
    _, lo, cnt = lax.while_loop(unsettled, body, (jnp.int32(0), lo0, count_ge(lo0)))
    return lo, cnt


def _tie_limit(count_eq_below, need, rows):
    def body(it, j):
        cand = j + lax.shift_left(jnp.int32(1), jnp.int32(14) - it)
        return jnp.where(count_eq_below(cand) <= need, cand, j)

    return lax.fori_loop(0, 15, body, jnp.zeros((rows, 1), I32))


QB_B = 128
KB_B = 256


def _dsa_kernel(qb_ref, qi_ref, misc_ref, kidx_ref, kb_ref, vb_ref, tb_ref, o_ref,
                key_scr, w_scr, thr_scr, m_scr, acc_scr, *, k_top):
    i = pl.program_id(0)
    n_kb = (i * QB_B) // KB_B + 1
    row = lax.broadcasted_iota(I32, (QB_B, KB_B), 0)
    col = lax.broadcasted_iota(I32, (QB_B, KB_B), 1)

    wi = misc_ref[:, MISC_WI:MISC_WI + H_IDX] * (H_IDX ** -0.5 * D_IDX ** -0.5)
    for h in range(H_IDX):
        w_scr[h] = jnp.broadcast_to(wi[:, h:h + 1], (QB_B, KB_B))

    def score_block(j, _):
        kblk = kidx_ref[pl.ds(pl.multiple_of(j * KB_B, KB_B), KB_B), :][:, :D_IDX]
        acc = jnp.zeros((QB_B, KB_B), F32)
        for h in range(H_IDX):
            isc = _dot_t(qi_ref[:, h * D_IDX:(h + 1) * D_IDX], kblk)
            acc = acc + jnp.maximum(isc, 0.0) * w_scr[h]
        admissible = (j * KB_B + col) // CHUNK <= (i * QB_B + row) // CHUNK
        key_scr[:, pl.ds(pl.multiple_of(j * KB_B, KB_B), KB_B)] = jnp.where(
            admissible, _sortable(acc), jnp.int32(NEG_KEY))
        return 0

    lax.fori_loop(0, n_kb, score_block, 0)

    def count_where(pred):
        def body(j, c):
            blk = key_scr[:, pl.ds(pl.multiple_of(j * KB_B, KB_B), KB_B)]
            return c + jnp.where(pred(blk, j * KB_B + col), 1.0, 0.0)
        c = lax.fori_loop(0, n_kb, body, jnp.zeros((QB_B, KB_B), F32))
        return jnp.sum(c, axis=1, keepdims=True)

    thr, cnt_ge = _kth_largest(lambda cand: count_where(lambda blk, idx: blk >= cand), QB_B, k_top)
    live = thr > jnp.int32(NEG_KEY)
    thr = jnp.maximum(thr, jnp.int32(NEG_KEY))
    thr_scr[0] = jnp.broadcast_to(thr, (QB_B, KB_B))
    thr_scr[1] = jnp.broadcast_to(jnp.where(live, jnp.int32(IDX_BIG), 0), (QB_B, KB_B))
    has_ties = jnp.max(jnp.where(live & (cnt_ge > float(k_top)), 1.0, 0.0)) > 0.0

    @pl.when(has_ties)
    def _():
        cnt_gt = count_where(lambda blk, idx: blk > thr)
        need = float(k_top) - cnt_gt
        jstar = _tie_limit(lambda cand: count_where(lambda blk, idx: (blk == thr) & (idx < cand)), need, QB_B)
        thr_scr[1] = jnp.broadcast_to(jnp.where(live, jstar, 0), (QB_B, KB_B))

    def mask_block(j, _):
        start = pl.multiple_of(j * KB_B, KB_B)
        keys = key_scr[:, pl.ds(start, KB_B)]
        thr_b = thr_scr[0]
        sel = (keys > thr_b) | ((keys == thr_b) & (j * KB_B + col < thr_scr[1]))
        key_scr[:, pl.ds(start, KB_B)] = lax.bitcast_convert_type(jnp.where(sel, 0.0, NEG_INF), I32)
        return 0

    lax.fori_loop(0, n_kb, mask_block, 0)

    odd = i % 2
    table_a = jnp.where(odd == 1, 3, 1)
    table_b = jnp.where(odd == 1, 0, 2)
    n_far = jnp.maximum(n_kb - 2, 0)
    ones = jnp.ones((KB_B, HEAD_DIM), BF16)
    groups = range(H_B_KV)
    heads = [[n * GQA + g for g in range(GQA)] for n in groups]
    q3 = [jnp.concatenate([qb_ref[:, h * HEAD_DIM:(h + 1) * HEAD_DIM] for h in heads[n]], axis=0) * ATTN_SCALE
          for n in groups]

    def scores(n, j, table):
        start = pl.multiple_of(j * KB_B, KB_B)
        negm = lax.bitcast_convert_type(key_scr[:, pl.ds(start, KB_B)], F32)
        k_n = kb_ref[pl.ds(start, KB_B), n * HEAD_DIM:(n + 1) * HEAD_DIM]
        s = _dot_t(q3[n], k_n).reshape(GQA, QB_B, KB_B) + negm[None]
        if table is not None:
            s = s + jnp.stack([tb_ref[table * H_B + h] for h in heads[n]])
        return s

    def max_block(j, table):
        for n in groups:
            s = scores(n, j, table)
            m_scr[n] = jnp.maximum(m_scr[n], jnp.maximum(s[:, :, :LANES], s[:, :, LANES:]))

    m_scr[...] = jnp.full(m_scr.shape, -1e30, F32)
    lax.fori_loop(0, n_far, lambda j, c: (max_block(j, None), c)[1], 0)
    pl.when(n_kb >= 2)(lambda: max_block(n_kb - 2, table_a))
    max_block(n_kb - 1, table_b)
    m_b = [jnp.broadcast_to(jnp.max(m_scr[n], axis=2, keepdims=True), (GQA, QB_B, KB_B)) for n in groups]

    def pv_block(j, table):
        start = pl.multiple_of(j * KB_B, KB_B)
        for n in groups:
            p = jnp.exp(scores(n, j, table) - m_b[n]).astype(BF16).reshape(GQA * QB_B, KB_B)
            v_n = vb_ref[pl.ds(start, KB_B), n * HEAD_DIM:(n + 1) * HEAD_DIM]
            acc_scr[n] += _dot(p, jnp.concatenate([v_n, ones], axis=1))

    acc_scr[...] = jnp.zeros(acc_scr.shape, F32)
    lax.fori_loop(0, n_far, lambda j, c: (pv_block(j, None), c)[1], 0)
    pl.when(n_kb >= 2)(lambda: pv_block(n_kb - 2, table_a))
    pv_block(n_kb - 1, table_b)
    for n in groups:
        for g, h in enumerate(heads[n]):
            a = acc_scr[n, g * QB_B:(g + 1) * QB_B, :]
            o_ref[:, h * HEAD_DIM:(h + 1) * HEAD_DIM] = (a[:, :HEAD_DIM] / a[:, HEAD_DIM:]).astype(BF16)


def _dsa_prompt(p32, pb, s, t5_tables):
    nq = s // QB_B
    k_top = min(TOPK_MAX, s // 4)
    once = pl.Buffered(1)
    return pl.pallas_call(
        functools.partial(_dsa_kernel, k_top=k_top),
        grid=(nq,),
        in_specs=[
            pl.BlockSpec((QB_B, D_B), lambda i: (i, OFF_QB // D_B)),
            pl.BlockSpec((QB_B, H_IDX * D_IDX), lambda i: (i, OFF_QI // (H_IDX * D_IDX))),
            pl.BlockSpec((QB_B, LANES), lambda i: (i, OFF_MISC // LANES)),
            pl.BlockSpec((s, LANES), lambda i: (0, OFF_MISC // LANES), pipeline_mode=once),
            pl.BlockSpec((s, D_B_KV), lambda i: (0, OFF_KB // D_B_KV), pipeline_mode=once),
            pl.BlockSpec((s, D_B_KV), lambda i: (0, OFF_VB // D_B_KV), pipeline_mode=once),
            pl.BlockSpec((4 * H_B, QB_B, KB_B), lambda i: (0, 0, 0), pipeline_mode=once),
        ],
        out_specs=pl.BlockSpec((QB_B, D_B), lambda i: (i, 0)),
        out_shape=jax.ShapeDtypeStruct((s, D_B), BF16),
        scratch_shapes=[
            pltpu.VMEM((QB_B, s), I32),
            pltpu.VMEM((H_IDX, QB_B, KB_B), F32),
            pltpu.VMEM((2, QB_B, KB_B), I32),
            pltpu.VMEM((H_B_KV, GQA, QB_B, LANES), F32),
            pltpu.VMEM((H_B_KV, GQA * QB_B, LANES), F32),
        ],
        compiler_params=_cparams(1),
    )(pb, pb, p32, pb, pb, pb, t5_tables)


QB_C = 512
HEADS_PER_STEP_C = LANES // HEAD_DIM


PREP_TILE_C = 512


def _fox_prep_kernel(q_ref, k_ref, v_ref, cum_ref, qa_ref, ka_ref, va_ref):
    rows = q_ref.shape[0]
    lane = lax.broadcasted_iota(I32, (rows, HEAD_DIM), 1)
    ones = jnp.ones((rows, HEAD_DIM), BF16)
    for h in range(H_C):
        hs = slice(h * HEAD_DIM, (h + 1) * HEAD_DIM)
        c1, c2, c3 = [c.astype(F32) for c in _split3(cum_ref[:, h:h + 1])]
        unit = jnp.where(lane < 6, 1.0, 0.0)
        q_extra = jnp.where(lane < 3, jnp.where(lane == 0, c1, jnp.where(lane == 1, c2, c3)), unit)
        k_extra = jnp.where(lane < 3, unit, -jnp.where(lane == 3, c1, jnp.where(lane == 4, c2, c3)) * unit)
        qa_ref[h] = jnp.concatenate([q_ref[:, hs] * ATTN_SCALE, q_extra.astype(BF16)], axis=1)
        ka_ref[h] = jnp.concatenate([k_ref[:, hs], k_extra.astype(BF16)], axis=1)
        va_ref[h] = jnp.concatenate([v_ref[:, hs], ones], axis=1)


def _fox_prepare(pb, s, cum):
    tm = PREP_TILE_C
    row = lambda off: pl.BlockSpec((tm, D_C), lambda i: (i, off // D_C))
    out = pl.BlockSpec((H_C, tm, LANES), lambda i: (0, i, 0))
    return pl.pallas_call(
        _fox_prep_kernel,
        grid=(s // tm,),
        in_specs=[row(OFF_QC), row(OFF_KC), row(OFF_VC), pl.BlockSpec((tm, 16), lambda i: (i, 0))],
        out_specs=[out, out, out],
        out_shape=[jax.ShapeDtypeStruct((H_C, s, LANES), BF16)] * 3,
        compiler_params=_cparams(1),
    )(pb, pb, pb, cum)


def _fox_kernel(q_ref, k_ref, v_ref, o_ref):
    i = pl.program_id(1)
    row = lax.broadcasted_iota(I32, (QB_C, QB_C), 0)
    col = lax.broadcasted_iota(I32, (QB_C, QB_C), 1)
    heads = range(HEADS_PER_STEP_C)
    q = [q_ref[hh] for hh in heads]

    def scores(hh, j):
        return _dot_t(q[hh], k_ref[hh, pl.ds(pl.multiple_of(j * QB_C, QB_C), QB_C), :])

    def halves_max(s):
        return functools.reduce(jnp.maximum, [s[:, c:c + LANES] for c in range(0, QB_C, LANES)])

    diag = [jnp.where(col <= row, scores(hh, i), NEG_INF) for hh in heads]

    def max_body(j, ms):
        return tuple(jnp.maximum(ms[hh], halves_max(scores(hh, j))) for hh in heads)

    ms = lax.fori_loop(0, i, max_body, tuple(halves_max(diag[hh]) for hh in heads))
    m_b = [jnp.broadcast_to(jnp.max(ms[hh], axis=1, keepdims=True), (QB_C, QB_C)) for hh in heads]

    def pv(hh, s, j):
        p = jnp.exp(s - m_b[hh]).astype(BF16)
        return _dot(p, v_ref[hh, pl.ds(pl.multiple_of(j * QB_C, QB_C), QB_C), :])

    def pv_body(j, accs):
        return tuple(accs[hh] + pv(hh, scores(hh, j), j) for hh in heads)

    accs = lax.fori_loop(0, i, pv_body, tuple(pv(hh, diag[hh], i) for hh in heads))
    o_ref[...] = jnp.concatenate(
        [accs[hh][:, :HEAD_DIM] / accs[hh][:, HEAD_DIM:] for hh in heads], axis=1).astype(BF16)


def _fox_prompt(qa, ka, va, s):
    nq = s // QB_C
    ng = H_C // HEADS_PER_STEP_C
    hp = HEADS_PER_STEP_C
    return pl.pallas_call(
        _fox_kernel,
        grid=(ng, nq),
        in_specs=[
            pl.BlockSpec((hp, QB_C, LANES), lambda g, i: (g, i, 0)),
            pl.BlockSpec((hp, s, LANES), lambda g, i: (g, 0, 0)),
            pl.BlockSpec((hp, s, LANES), lambda g, i: (g, 0, 0)),
        ],
        out_specs=pl.BlockSpec((QB_C, LANES), lambda g, i: (i, g)),
        out_shape=jax.ShapeDtypeStruct((s, D_C), BF16),
        compiler_params=_cparams(2),
    )(qa, ka, va)


def _softmax_pv(s, v_all):
    m = jnp.max(s, axis=1, keepdims=True)
    p = jnp.exp(s - m)
    l = jnp.sum(p, axis=1, keepdims=True)
    return _dot(p.astype(BF16), v_all) / l


def _with_new_rows(cache, new, pad_rows):
    parts = [cache.astype(BF16), new]
    if pad_rows:
        parts.append(jnp.zeros((pad_rows, new.shape[1]), BF16))
    return jnp.concatenate(parts, axis=0)


def _sample_kernel(qa_ref, ka_ref, va_ref, kvb_ref, qi_ref, qb_ref, qc_ref, kc_ref, vc_ref, misc16_ref,
                   misc32_ref, ca_ref, cb_ref, cbi_ref, cc_ref, clf_ref, clft_ref, lfn_ref, lfnt_ref,
                   bias_a_ref, t5_ref, triu_ref, tril_ref,
                   oa_ref, ob_ref, oc_ref, *, t, past, na, k_top):
    la = na + LANES
    lk = past + LANES
    pad = LANES - t

    ca = ca_ref[...]
    ka_all = _with_new_rows(ca[:, :D_A], ka_ref[...], pad)
    va_all = _with_new_rows(ca[:, D_A:], va_ref[...], pad)
    for h in range(H_A):
        hs = slice(h * HEAD_DIM, (h + 1) * HEAD_DIM)
        s = _dot_t(qa_ref[:, hs], ka_all[:, hs]) * ATTN_SCALE + bias_a_ref[h]
        oa_ref[:, hs] = _softmax_pv(s, va_all[:, hs]).astype(BF16)

    col = lax.broadcasted_iota(I32, (t, lk), 1)
    ki_all = _with_new_rows(cbi_ref[...], misc16_ref[:, :D_IDX], pad)
    wi = misc32_ref[:, MISC_WI:MISC_WI + H_IDX] * (H_IDX ** -0.5 * D_IDX ** -0.5)
    acc = jnp.zeros((t, lk), F32)
    for h in range(H_IDX):
        isc = _dot_t(qi_ref[:, h * D_IDX:(h + 1) * D_IDX], ki_all)
        acc = acc + jnp.maximum(isc, 0.0) * wi[:, h:h + 1]
    keys = jnp.where(col < past + t, _sortable(acc), jnp.int32(NEG_KEY))

    def count(pred):
        return jnp.sum(jnp.where(pred, 1.0, 0.0), axis=1, keepdims=True)

    thr, _ = _kth_largest(lambda cand: count(keys >= cand), t, k_top)
    live = thr > jnp.int32(NEG_KEY)
    thr = jnp.maximum(thr, jnp.int32(NEG_KEY))
    need = float(k_top) - count(keys > thr)
    jstar = _tie_limit(lambda cand: count((keys == thr) & (col < cand)), need, t)
    sel = (keys > thr) | ((keys == thr) & (col < jnp.where(live, jstar, 0)))

    cb = cb_ref[...]
    kb_all = _with_new_rows(cb[:, :D_B_KV], kvb_ref[:, :D_B_KV], pad)
    vb_all = _with_new_rows(cb[:, D_B_KV:], kvb_ref[:, D_B_KV:], pad)
    for n in range(H_B_KV):
        ns = slice(n * HEAD_DIM, (n + 1) * HEAD_DIM)
        for g in range(GQA):
            h = n * GQA + g
            hs = slice(h * HEAD_DIM, (h + 1) * HEAD_DIM)
            s = _dot_t(qb_ref[:, hs], kb_all[:, ns]) * ATTN_SCALE + t5_ref[h]
            s = jnp.where(sel, s, NEG_INF)
            ob_ref[:, hs] = _softmax_pv(s, vb_all[:, ns]).astype(BF16)

    cc = cc_ref[...]
    kc_all = _with_new_rows(cc[:, :D_C], kc_ref[...], pad)
    vc_all = _with_new_rows(cc[:, D_C:], vc_ref[...], pad)
    lft = jnp.concatenate([clft_ref[...], lfnt_ref[...]], axis=1)
    t1, t2, t3 = _split3(lft)
    triu = triu_ref[...]
    cum_t = _dot(t1, triu) + _dot(t2, triu) + _dot(t3, triu)
    total = jnp.sum(clf_ref[...], axis=0, keepdims=True)
    n1, n2, n3 = _split3(lfn_ref[...])
    tril = tril_ref[...]
    cum_q = total + _dot(tril, n1) + _dot(tril, n2) + _dot(tril, n3)
    row = lax.broadcasted_iota(I32, (t, lk), 0)
    causal = col <= past + row
    for h in range(H_C):
        hs = slice(h * HEAD_DIM, (h + 1) * HEAD_DIM)
        s = _dot_t(qc_ref[:, hs], kc_all[:, hs]) * ATTN_SCALE + cum_q[:, h:h + 1] - cum_t[h:h + 1, :]
        s = jnp.where(causal, s, NEG_INF)
        oc_ref[:, hs] = _softmax_pv(s, vc_all[:, hs]).astype(BF16)


def _sample_mixers(p32, pb, s, ca, cb, cbi, cc, clf, clft, lfn, lfnt, bias_a, t5_tab, triu, tril, layer):
    nb, na = ca.shape[1], ca.shape[2]
    past = cb.shape[2]
    t = lfn.shape[1]
    k_top = min(TOPK_MAX, (past + t) // 4)
    r0 = s // t
    lk = past + LANES
    la = na + LANES
    row = lambda width, off: pl.BlockSpec((t, width), lambda b: (r0 + b, off // width))
    cache = lambda rows, width: pl.BlockSpec((None, None, rows, width), lambda b: (layer, b, 0, 0))
    full = lambda shape: pl.BlockSpec(shape, lambda b: (0,) * len(shape))
    in_specs = [
        row(D_A, OFF_QA), row(D_A, OFF_KA), row(D_A, OFF_VA), row(2 * D_B_KV, OFF_KB),
        row(H_IDX * D_IDX, OFF_QI), row(D_B, OFF_QB), row(D_C, OFF_QC), row(D_C, OFF_KC), row(D_C, OFF_VC),
        row(LANES, OFF_MISC), row(LANES, OFF_MISC),
        cache(na, 2 * D_A), cache(past, 2 * D_B_KV), cache(past, D_IDX), cache(past, 2 * D_C),
        cache(past, 16), cache(16, past),
        pl.BlockSpec((None, t, 16), lambda b: (b, 0, 0)),
        pl.BlockSpec((None, 16, LANES), lambda b: (b, 0, 0)),
        full((H_A, t, la)), full((H_B, t, lk)), full((lk, lk)), full((t, t)),
    ]
    out = lambda width: pl.BlockSpec((t, width), lambda b: (b, 0))
    return pl.pallas_call(
        functools.partial(_sample_kernel, t=t, past=past, na=na, k_top=k_top),
        grid=(nb,),
        in_specs=in_specs,
        out_specs=[out(D_A), out(D_B), out(D_C)],
        out_shape=[jax.ShapeDtypeStruct((nb * t, w), BF16) for w in (D_A, D_B, D_C)],
        compiler_params=_cparams(1),
    )(pb, pb, pb, pb, pb, pb, pb, pb, pb, pb, p32, ca, cb, cbi, cc, clf, clft, lfn, lfnt,
      bias_a, t5_tab, triu, tril)


def _outproj_kernel(x_ref, g_ref, ap, bp, cp, as_, bs, cs, w_ref, o_ref, *, n_prompt_blocks):
    i = pl.program_id(0)

    def run(a, b, c):
        y = (_dot(a[...], w_ref[:D_A, :]) + _dot(b[...], w_ref[D_A:D_A + D_B, :])
             + _dot(c[...], w_ref[D_A + D_B:, :]))
        o_ref[...] = x_ref[...] + g_ref[...] * y

    pl.when(i < n_prompt_blocks)(lambda: run(ap, bp, cp))
    pl.when(i >= n_prompt_blocks)(lambda: run(as_, bs, cs))


def _out_project(x, gate, mix_p, mix_s, w_out_b, layer, n_prompt_blocks):
    m = x.shape[0]
    tm = TOKEN_TILE
    last_p = n_prompt_blocks - 1
    p_map = lambda i: (jnp.minimum(i, last_p), 0)
    s_map = lambda i: (jnp.maximum(i - n_prompt_blocks, 0), 0)
    widths = (D_A, D_B, D_C)
    return pl.pallas_call(
        functools.partial(_outproj_kernel, n_prompt_blocks=n_prompt_blocks),
        grid=(m // tm,),
        in_specs=[
            pl.BlockSpec((tm, D_MODEL), lambda i: (i, 0)),
            pl.BlockSpec((tm, D_MODEL), lambda i: (_mod_block_index(i, n_prompt_blocks), 0)),
            *[pl.BlockSpec((tm, w), p_map) for w in widths],
            *[pl.BlockSpec((tm, w), s_map) for w in widths],
            pl.BlockSpec((None, D_MODEL, D_MODEL), lambda i: (layer, 0, 0)),
        ],
        out_specs=pl.BlockSpec((tm, D_MODEL), lambda i: (i, 0)),
        out_shape=jax.ShapeDtypeStruct((m, D_MODEL), F32),
        compiler_params=_cparams(1),
    )(x, gate, *mix_p, *mix_s, w_out_b)


def _lane_pick(vals, lane, idx):
    return jnp.sum(jnp.where(lane == idx, vals, 0.0), axis=1, keepdims=True)


def _first_argmax(vals, lane):
    m = jnp.max(vals, axis=1, keepdims=True)
    idx = jnp.min(jnp.where(vals == m, lane, float(LANES)), axis=1, keepdims=True)
    return m, idx


def _router_kernel(x_ref, gain_ref, sc_ref, sh_ref, wr_ref, br_ref, tri_ref, h_ref, meta_ref, cnt_ref, carry):
    @pl.when(pl.program_id(0) == 0)
    def _():
        carry[...] = jnp.zeros_like(carry)

    h = _norm_mod(x_ref[...], gain_ref[...], sc_ref[...], sh_ref[...])
    hb = h.astype(BF16)
    h_ref[...] = h
    scores = jax.nn.sigmoid(_dot(hb, wr_ref[...]))
    lane_i = lax.broadcasted_iota(I32, scores.shape, 1)
    lane = lane_i.astype(F32)
    sel = jnp.where(lane_i < N_EXPERTS, scores + br_ref[...], NEG_INF)
    group = (lane_i // (N_EXPERTS // N_GROUPS)).astype(F32)

    best = None
    for g in range(N_GROUPS):
        in_g = jnp.where(group == float(g), sel, NEG_INF)
        m1, i1 = _first_argmax(in_g, lane)
        m2 = jnp.max(jnp.where(lane == i1, NEG_INF, in_g), axis=1, keepdims=True)
        gs = m1 + m2
        if best is None:
            best, gbest = gs, jnp.zeros_like(i1)
        else:
            better = gs > best
            best = jnp.where(better, gs, best)
            gbest = jnp.where(better, float(g), gbest)

    in_best = jnp.where(group == gbest, sel, NEG_INF)
    _, e0 = _first_argmax(in_best, lane)
    _, e1 = _first_argmax(jnp.where(lane == e0, NEG_INF, in_best), lane)
    w0 = _lane_pick(scores, lane, e0)
    w1 = _lane_pick(scores, lane, e1)
    wsum = w0 + w1

    onehot = jnp.where((lane == e0) | (lane == e1), 1.0, 0.0)
    before = _dot(tri_ref[...], onehot.astype(BF16)) + carry[...]
    r0 = _lane_pick(before, lane, e0)
    r1 = _lane_pick(before, lane, e1)
    carry[...] = carry[...] + jnp.sum(onehot, axis=0, keepdims=True)
    cnt_ref[...] = carry[...]

    meta = jnp.zeros(scores.shape, F32)
    for k, v in enumerate((e0, e1, r0, r1, w0 / wsum, w1 / wsum)):
        meta = jnp.where(lane_i == k, v, meta)
    meta_ref[...] = meta


def _router(x, gain, sc, sh, w_router_p, b_router_p, tri, n_prompt_blocks):
    m = x.shape[0]
    tm = TOKEN_TILE
    mod_map = lambda i: (_mod_block_index(i, n_prompt_blocks), 0)
    return pl.pallas_call(
        _router_kernel,
        grid=(m // tm,),
        in_specs=[
            pl.BlockSpec((tm, D_MODEL), lambda i: (i, 0)),
            pl.BlockSpec((1, D_MODEL), lambda i: (0, 0)),
            pl.BlockSpec((tm, D_MODEL), mod_map),
            pl.BlockSpec((tm, D_MODEL), mod_map),
            pl.BlockSpec((D_MODEL, LANES), lambda i: (0, 0)),
            pl.BlockSpec((1, LANES), lambda i: (0, 0)),
            pl.BlockSpec((tm, tm), lambda i: (0, 0)),
        ],
        out_specs=[
            pl.BlockSpec((tm, D_MODEL), lambda i: (i, 0)),
            pl.BlockSpec((tm, LANES), lambda i: (i, 0)),
            pl.BlockSpec((1, LANES), lambda i: (0, 0)),
        ],
        out_shape=[
            jax.ShapeDtypeStruct((m, D_MODEL), F32),
            jax.ShapeDtypeStruct((m, LANES), F32),
            jax.ShapeDtypeStruct((1, LANES), F32),
        ],
        scratch_shapes=[pltpu.VMEM((1, LANES), F32)],
        compiler_params=_cparams(1),
    )(x, gain, sc, sh, w_router_p, b_router_p, tri)


EXPERT_ROWS = 256
EXPERT_TF = 512


def _row_copy(src_hbm, src_row, dst_buf, dst_row, sem):
    return pltpu.make_async_copy(src_hbm.at[pl.ds(src_row, 1), :], dst_buf.at[pl.ds(dst_row, 1), :], sem)


def _expert_kernel(be_ref, nu_ref, st_ref, h_hbm, wg_ref, wu_ref, wd_ref, o_ref, xrows, x16, sems):
    b = pl.program_id(0)
    f = pl.program_id(1)
    n_used = nu_ref[0]
    rows_per_step = EXPERT_ROWS // (D_EXPERT // EXPERT_TF)

    def start_rows(blk, first, count):
        for r in range(count):
            tok = st_ref[blk * EXPERT_ROWS + first + r]
            _row_copy(h_hbm, tok, xrows.at[blk % 2], first + r, sems.at[blk % 2]).start()

    @pl.when((b == 0) & (f == 0))
    def _():
        start_rows(0, 0, EXPERT_ROWS)

    @pl.when((f == 0) & (b < n_used))
    def _():
        for r in range(EXPERT_ROWS):
            _row_copy(h_hbm, 0, xrows.at[b % 2], r, sems.at[b % 2]).wait()
        x16[...] = xrows[b % 2].astype(BF16)

    @pl.when(b + 1 < n_used)
    def _():
        start_rows(b + 1, f * rows_per_step, rows_per_step)

    @pl.when((f == 0) & (b >= n_used))
    def _():
        o_ref[...] = jnp.zeros(o_ref.shape, F32)

    @pl.when(b < n_used)
    def _():
        x = x16[...]
        a = _dot(x, wg_ref[...].astype(BF16))
        u = _dot(x, wu_ref[...].astype(BF16))
        act = (a * jax.nn.sigmoid(a) * u).astype(BF16)
        y = _dot(act, wd_ref[...].astype(BF16))

        @pl.when(f == 0)
        def _():
            o_ref[...] = y

        @pl.when(f > 0)
        def _():
            o_ref[...] += y


def _experts(h2, slot_tok, block_e, n_used, w_g, w_u, w_d, layer):
    nslots = slot_tok.shape[0]
    nb = nslots // EXPERT_ROWS
    nf = D_EXPERT // EXPERT_TF

    def blk(b, nu):
        return jnp.minimum(b, nu[0] - 1)

    def fidx(b, f, nu):
        return jnp.where(b < nu[0], f, nf - 1)

    grid_spec = pltpu.PrefetchScalarGridSpec(
        num_scalar_prefetch=3,
        grid=(nb, nf),
        in_specs=[
            pl.BlockSpec(memory_space=pl.ANY),
            pl.BlockSpec((None, None, D_MODEL, EXPERT_TF),
                         lambda b, f, be, nu, st: (layer, be[blk(b, nu)], 0, fidx(b, f, nu))),
            pl.BlockSpec((None, None, D_MODEL, EXPERT_TF),
                         lambda b, f, be, nu, st: (layer, be[blk(b, nu)], 0, fidx(b, f, nu))),
            pl.BlockSpec((None, None, EXPERT_TF, D_MODEL),
                         lambda b, f, be, nu, st: (layer, be[blk(b, nu)], fidx(b, f, nu), 0)),
        ],
        out_specs=pl.BlockSpec((EXPERT_ROWS, D_MODEL), lambda b, f, be, nu, st: (b, 0)),
        scratch_shapes=[
            pltpu.VMEM((2, EXPERT_ROWS, D_MODEL), F32),
            pltpu.VMEM((EXPERT_ROWS, D_MODEL), BF16),
            pltpu.SemaphoreType.DMA((2,)),
        ],
    )
    return pl.pallas_call(
        _expert_kernel,
        grid_spec=grid_spec,
        out_shape=jax.ShapeDtypeStruct((nslots, D_MODEL), F32),
        compiler_params=_cparams(2),
    )(block_e, n_used, slot_tok, h2, w_g, w_u, w_d)


COMBINE_TILE = 256


def _combine_kernel(d0_ref, d1_ref, x_ref, g_ref, meta_ref, y_hbm, o_ref, ybuf, sems):
    i = pl.program_id(0)
    n = pl.num_programs(0)

    def start_rows(blk):
        slot = blk % 2

        def body(r, _):
            t = blk * COMBINE_TILE + r
            _row_copy(y_hbm, d0_ref[t], ybuf.at[slot, 0], r, sems.at[slot]).start()
            _row_copy(y_hbm, d1_ref[t], ybuf.at[slot, 1], r, sems.at[slot]).start()
            return 0

        lax.fori_loop(0, COMBINE_TILE, body, 0, unroll=8)

    @pl.when(i == 0)
    def _():
        start_rows(i)

    slot = i % 2
    for k in range(2):
        for r in range(COMBINE_TILE):
            _row_copy(y_hbm, 0, ybuf.at[slot, k], r, sems.at[slot]).wait()

    @pl.when(i + 1 < n)
    def _():
        start_rows(i + 1)

    moe = ybuf[slot, 0] * meta_ref[:, 4:5] + ybuf[slot, 1] * meta_ref[:, 5:6]
    o_ref[...] = x_ref[...] + g_ref[...] * moe


def _combine(x, gate, meta, yb, dest0, dest1, n_prompt_blocks):
    m = x.shape[0]
    tm = COMBINE_TILE
    ratio = TOKEN_TILE // tm
    mod_map = lambda i, d0, d1: (jnp.where(i < n_prompt_blocks * ratio, 0, i - (n_prompt_blocks - 1) * ratio), 0)
    row = lambda width: pl.BlockSpec((tm, width), lambda i, d0, d1: (i, 0))
    grid_spec = pltpu.PrefetchScalarGridSpec(
        num_scalar_prefetch=2,
        grid=(m // tm,),
        in_specs=[row(D_MODEL), pl.BlockSpec((tm, D_MODEL), mod_map), row(LANES), pl.BlockSpec(memory_space=pl.ANY)],
        out_specs=row(D_MODEL),
        scratch_shapes=[pltpu.VMEM((2, 2, tm, D_MODEL), F32), pltpu.SemaphoreType.DMA((2,))],
    )
    return pl.pallas_call(
        _combine_kernel,
        grid_spec=grid_spec,
        out_shape=jax.ShapeDtypeStruct((m, D_MODEL), F32),
        compiler_params=_cparams(1),
    )(dest0, dest1, x, gate, meta, yb)


def _t5_bucket(rel):
    nb = T5_BUCKETS // 2
    max_exact = nb // 2
    ret = jnp.where(rel < 0, nb, 0)
    n = jnp.abs(rel)
    nf = jnp.maximum(n, 1).astype(F32)
    large = max_exact + (jnp.log(nf / max_exact) / math.log(T5_MAX_DIST / max_exact) * (nb - max_exact)).astype(I32)
    large = jnp.minimum(large, nb - 1)
    return ret + jnp.where(n < max_exact, n, large)


def _toeplitz(by_rel, q0, k0, nq, nk):
    d = jnp.arange(-(nk - 1), nq)
    rev = by_rel(q0 - k0 + d)[:, ::-1]
    rows = jax.vmap(lambda i: lax.dynamic_slice_in_dim(rev, i, nk, axis=1))(nq - 1 - jnp.arange(nq))
    return jnp.moveaxis(rows, 0, 1)


def _band_bias(table, q0, k0, nq, nk, n_valid):
    by_rel = lambda rel: table.astype(F32)[:, jnp.clip(rel, -A_REL_FUTURE, A_REL_PAST) + A_REL_FUTURE]
    qc = (q0 + jnp.arange(nq))[:, None] // CHUNK
    col = jnp.arange(nk)[None, :]
    kc = (k0 + col) // CHUNK
    ok = (col < n_valid) & (kc <= qc) & (kc >= qc - A_LEFT_CHUNKS)
    return jnp.where(ok[None], _toeplitz(by_rel, q0, k0, nq, nk), NEG_INF)


def _t5_bias(t5, q0, k0, nq, nk):
    return _toeplitz(lambda rel: t5.astype(F32)[_t5_bucket(rel)].T, q0, k0, nq, nk)


def _relayout_w_in(w_in_l):
    qa, ka, va, qb, kb, vb, qi, ki, wi, qc, kc, vc, fc = jnp.split(w_in_l, PROJ_SPLITS, axis=1)
    pad = jnp.zeros((D_MODEL, P_COLS - OFF_MISC - D_IDX - H_IDX - H_C), w_in_l.dtype)
    w = jnp.concatenate([qa, ka, va, kb, vb, qi, qb, qc, kc, vc, ki, wi, fc, pad], axis=1).astype(BF16)
    return jnp.swapaxes(w.reshape(D_MODEL, P_COLS // PROJ_TN, PROJ_TN), 0, 1)


def _column_params(qk_gain_l, b_forget_l):
    ones = lambda n: jnp.ones((n,), F32)
    g = qk_gain_l.astype(F32)
    gq = jnp.concatenate([
        jnp.tile(g[0], H_A), jnp.tile(g[1], H_A), ones(D_A),
        jnp.tile(g[3], H_B_KV), ones(D_B_KV), ones(H_IDX * D_IDX),
        jnp.tile(g[2], H_B), jnp.tile(g[4], H_C), jnp.tile(g[5], H_C), ones(D_C), ones(P_COLS - OFF_MISC)])
    bq = jnp.zeros((P_COLS,), F32).at[OFF_MISC + MISC_FC:OFF_MISC + MISC_FC + H_C].set(b_forget_l.astype(F32))
    return gq[None, :], bq[None, :]


def _tri(n, *, strict=False, upper=False):
    r = jnp.arange(n)[:, None]
    c = jnp.arange(n)[None, :]
    m = (r < c if strict else r <= c) if upper else (c < r if strict else c <= r)
    return m.astype(BF16)


def kernel(x_prompt, x_sample, c_prompt, c_sample, cache_a_kv, cache_b_kv, cache_b_kidx, cache_c_kv, cache_c_logf,
           w_ada, b_ada, norm_gain, w_in, b_forget, qk_gain, rel_bias_a, t5_bias, w_out, w_router, b_router,
           w_e_gate, w_e_up, w_e_down):
    depth = w_in.shape[0]
    bp, s, d = x_prompt.shape
    nb, t, _ = x_sample.shape
    ns = nb * t
    tm = TOKEN_TILE
    assert bp == 1 and d == D_MODEL and s % tm == 0 and ns % tm == 0 and s % QB_C == 0
    n_pb = s // tm
    m = s + ns
    na = cache_a_kv.shape[2]
    past = cache_b_kv.shape[2]
    keep = min(A_LEFT_CHUNKS * CHUNK, s)

    x = jnp.concatenate([x_prompt.reshape(s, d), x_sample.reshape(ns, d)], axis=0)
    c_all = jnp.concatenate([c_prompt, c_sample, jnp.zeros((-(bp + nb) % 8, d), F32)], axis=0)

    t5_far = t5_bias.astype(F32)[T5_BUCKETS // 2 - 1]
    t5_tables = jnp.concatenate(
        [_t5_bias(t5_bias, off, 0, QB_B, KB_B) - t5_far[:, None, None] for off in (QB_B, 2 * QB_B, 0)]
        + [jnp.zeros((H_B, QB_B, KB_B), F32)], axis=0)
    t5_s = _t5_bias(t5_bias, past, 0, t, past + LANES)
    bd = ((jnp.arange(PROJ_TN)[:, None] // HEAD_DIM == jnp.arange(PROJ_TN)[None, :] // HEAD_DIM)
          .astype(F32) / HEAD_DIM).astype(BF16)
    tri_cum = _tri(CUM_TILE)
    tri_rank = _tri(tm, strict=True)
    triu_s = _tri(past + LANES, upper=True)
    tril_s = _tri(t)
    w_out_b = w_out.astype(BF16)
    w_router_p = jnp.pad(w_router, ((0, 0), (0, LANES - N_EXPERTS))).astype(BF16)
    b_router_p = jnp.pad(b_router.astype(F32), (0, LANES - N_EXPERTS))[None, :]

    ca = cache_a_kv.reshape(depth, nb, na, 2 * D_A)
    cb = cache_b_kv.reshape(depth, nb, past, 2 * D_B_KV)
    cc = cache_c_kv.reshape(depth, nb, past, 2 * D_C)
    clf = jnp.pad(cache_c_logf.astype(F32), ((0, 0), (0, 0), (0, 0), (0, 16 - H_C)))
    clft = jnp.swapaxes(clf, 2, 3)

    n_asg = 2 * m
    n_eb = n_asg // EXPERT_ROWS + N_EXPERTS
    tok = jnp.arange(m, dtype=I32)

    states_p, states_s = [], []
    for l in range(depth):
        mod = _ada(c_all, w_ada, b_ada, l)
        mods = []
        for part in jnp.split(mod, 6, axis=1):
            mods.append(jnp.concatenate([jnp.broadcast_to(part[:1], (tm, d)), jnp.repeat(part[bp:bp + nb], t, axis=0)], 0))
        sh1, sc1, g1, sh2, sc2, g2 = mods

        gq, bq = _column_params(qk_gain[l], b_forget[l])
        p32, pb = _project(x, norm_gain[l, 0][None, :], sc1, sh1, _relayout_w_in(w_in[l]), gq, bq, bd, n_pb)

        bias_a_p = _band_bias(rel_bias_a[l], (A_KEY_BLOCKS - 1) * QB_A, 0, QB_A, A_KEY_BLOCKS * QB_A,
                              A_KEY_BLOCKS * QB_A)
        oa_p = _band_prompt(pb, s, bias_a_p)
        ob_p = _dsa_prompt(p32, pb, s, t5_tables)
        cum = _cumsum_rows(p32, s, tri_cum)[:, MISC_FC:MISC_FC + 16]
        oc_p = _fox_prompt(*_fox_prepare(pb, s, cum), s)

        lfn = p32[s:, OFF_MISC + MISC_FC:OFF_MISC + MISC_FC + 16].reshape(nb, t, 16)
        lfnt = jnp.pad(jnp.swapaxes(lfn, 1, 2), ((0, 0), (0, 0), (0, LANES - t)))
        bias_a_s = _band_bias(rel_bias_a[l], past, past - na, t, na + LANES, na + t)
        oa_s, ob_s, oc_s = _sample_mixers(p32, pb, s, ca, cb, cache_b_kidx, cc, clf, clft, lfn, lfnt,
                                          bias_a_s, t5_s, triu_s, tril_s, l)

        x = _out_project(x, g1, (oa_p, ob_p, oc_p), (oa_s, ob_s, oc_s), w_out_b, l, n_pb)

        h2, meta, counts = _router(x, norm_gain[l, 1][None, :], sc2, sh2, w_router_p, b_router_p, tri_rank, n_pb)
        e0 = meta[:, 0].astype(I32)
        e1 = meta[:, 1].astype(I32)
        counts = counts[0, :N_EXPERTS].astype(I32)
        padded = (counts + EXPERT_ROWS - 1) // EXPERT_ROWS * EXPERT_ROWS
        pend = jnp.cumsum(padded)
        pstart = pend - padded
        dest0 = pstart[e0] + meta[:, 2].astype(I32)
        dest1 = pstart[e1] + meta[:, 3].astype(I32)
        slot_tok = jnp.zeros((n_eb * EXPERT_ROWS,), I32).at[dest0].set(tok).at[dest1].set(tok)
        block_e = jnp.minimum(jnp.searchsorted(pend, jnp.arange(n_eb, dtype=I32) * EXPERT_ROWS, side='right'),
                              N_EXPERTS - 1).astype(I32)
        n_used = (pend[-1:] // EXPERT_ROWS).astype(I32)
        yb = _experts(h2, slot_tok, block_e, n_used, w_e_gate, w_e_up, w_e_down, l)
        x = _combine(x, g2, meta, yb, dest0, dest1, n_pb)

        def states(rows, nbatch, a_rows):
            r = p32[rows]
            n = r.shape[0] // nbatch
            kv = lambda off, heads: r[:, off:off + 2 * heads * HEAD_DIM].reshape(nbatch, n, 2, heads, HEAD_DIM)
            return (kv(OFF_KA, H_A)[:, n - a_rows:], kv(OFF_KB, H_B_KV),
                    r[:, OFF_MISC:OFF_MISC + D_IDX].reshape(nbatch, n, D_IDX), kv(OFF_KC, H_C),
                    r[:, OFF_MISC + MISC_FC:OFF_MISC + MISC_FC + H_C].reshape(nbatch, n, H_C))

        states_p.append(states(slice(0, s), bp, keep))
        states_s.append(states(slice(s, m), nb, t))

    stk = lambda sts, i: jnp.stack([st[i] for st in sts], axis=0)
    return (x[:s].reshape(bp, s, d), x[s:].reshape(nb, t, d),
            *[stk(states_p, i) for i in range(5)], *[stk(states_s, i) for i in range(5)])
```

```python
import functools
import math

import jax
import jax.numpy as jnp
from jax import lax
from jax.experimental import pallas as pl
from jax.experimental.pallas import tpu as pltpu

F32 = jnp.float32
BF16 = jnp.bfloat16
I32 = jnp.int32

D_MODEL = 2048
HEAD_DIM = 64
CHUNK = 64
EPS = 1e-6
ATTN_SCALE = HEAD_DIM ** -0.5
H_A = 8
A_LEFT_CHUNKS = 8
A_REL_PAST = 128
A_REL_FUTURE = CHUNK - 1
H_B = 12
H_B_KV = 4
GQA = H_B // H_B_KV
H_IDX = 16
D_IDX = 64
TOPK_MAX = 256
H_C = 12
T5_BUCKETS = 32
T5_MAX_DIST = 128
N_EXPERTS = 32
N_GROUPS = 4
D_EXPERT = 1024

D_A = H_A * HEAD_DIM
D_B = H_B * HEAD_DIM
D_B_KV = H_B_KV * HEAD_DIM
D_C = H_C * HEAD_DIM
PROJ_SIZES = (D_A, D_A, D_A, D_B, D_B_KV, D_B_KV, H_IDX * D_IDX, D_IDX, H_IDX, D_C, D_C, D_C, H_C)
PROJ_SPLITS = tuple(sum(PROJ_SIZES[:i + 1]) for i in range(len(PROJ_SIZES) - 1))

OFF_QA, OFF_KA, OFF_VA = 0, 512, 1024
OFF_KB, OFF_VB = 1536, 1792
OFF_QI = 2048
OFF_QB = 3072
OFF_QC, OFF_KC, OFF_VC = 3840, 4608, 5376
OFF_MISC = 6144
MISC_WI = 64
MISC_FC = 80
P_COLS = 6400
PROJ_TN = 256
NORM_COL_BLOCKS = (0, 1, 2, 3, 6, 12, 13, 14, 15, 16, 17, 18, 19, 20)
MISC_COL_BLOCK = OFF_MISC // PROJ_TN

TOKEN_TILE = 512
LANES = 128
VMEM_LIMIT = 56 * 1024 * 1024

NEG_INF = float("-inf")
NEG_KEY = (0xFF800000 ^ 0x7FFFFFFF) - (1 << 32)
IDX_BIG = 1 << 30


def _cparams(n_axes):
    return pltpu.CompilerParams(dimension_semantics=("arbitrary",) * n_axes, vmem_limit_bytes=VMEM_LIMIT)


def _split3(x):
    x1 = x.astype(BF16)
    r1 = x - x1.astype(F32)
    x2 = r1.astype(BF16)
    r2 = r1 - x2.astype(F32)
    return x1, x2, r2.astype(BF16)


def _dot(a, b):
    return jnp.dot(a, b, preferred_element_type=F32)


def _dot_t(a, b):
    return lax.dot_general(a, b, (((1,), (1,)), ((), ())), preferred_element_type=F32)


def _sortable(x):
    b = lax.bitcast_convert_type(x, I32)
    return jnp.where(b < 0, b ^ jnp.int32(0x7FFFFFFF), b)


def _ada_kernel(c_ref, w_ref, b_ref, o_ref):
    c = c_ref[...]
    a = (c * jax.nn.sigmoid(c)).astype(BF16)
    o_ref[...] = _dot(a, w_ref[...].astype(BF16)) + b_ref[...]


def _ada(c_all, w_ada, b_ada, layer):
    rows = c_all.shape[0]
    n_out = w_ada.shape[2]
    tn = 1024
    return pl.pallas_call(
        _ada_kernel,
        grid=(n_out // tn,),
        in_specs=[
            pl.BlockSpec((rows, D_MODEL), lambda n: (0, 0)),
            pl.BlockSpec((None, D_MODEL, tn), lambda n: (layer, 0, n)),
            pl.BlockSpec((None, 1, tn), lambda n: (layer, 0, n)),
        ],
        out_specs=pl.BlockSpec((rows, tn), lambda n: (0, n)),
        out_shape=jax.ShapeDtypeStruct((rows, n_out), F32),
        compiler_params=_cparams(1),
    )(c_all, w_ada, b_ada.reshape(b_ada.shape[0], 1, n_out))


def _mod_block_index(i, n_prompt_blocks):
    return jnp.where(i < n_prompt_blocks, 0, i - n_prompt_blocks + 1)


def _norm_mod(x, gain, sc, sh):
    ms = jnp.mean(x * x, axis=-1, keepdims=True)
    return (x * lax.rsqrt(ms + EPS) * gain) * (1.0 + sc) + sh


def _proj_kernel(x_ref, gain_ref, sc_ref, sh_ref, w_ref, gq_ref, bq_ref, bd_ref, o32_ref, o16_ref, h_scr):
    n = pl.program_id(1)

    @pl.when(n == 0)
    def _():
        h_scr[...] = _norm_mod(x_ref[...], gain_ref[...], sc_ref[...], sh_ref[...]).astype(BF16)

    y = _dot(h_scr[...], w_ref[...])

    is_norm = functools.reduce(jnp.logical_or, [n == b for b in NORM_COL_BLOCKS])
    is_misc = n == MISC_COL_BLOCK

    def emit(v):
        o32_ref[...] = v
        o16_ref[...] = v.astype(BF16)

    @pl.when(is_norm)
    def _():
        s1, s2, s3 = _split3(y * y)
        bd = bd_ref[...]
        ms = _dot(s1, bd) + _dot(s2, bd) + _dot(s3, bd)
        emit(y * lax.rsqrt(ms + EPS) * gq_ref[...])

    @pl.when(is_misc)
    def _():
        lane = lax.broadcasted_iota(I32, y.shape, 1)
        z = y + bq_ref[...]
        logsig = jnp.minimum(z, 0.0) - jnp.log(1.0 + jnp.exp(-jnp.abs(z)))
        emit(jnp.where((lane >= MISC_FC) & (lane < MISC_FC + H_C), logsig, y))

    @pl.when(jnp.logical_not(jnp.logical_or(is_norm, is_misc)))
    def _():
        emit(y)


def _project(x, gain, sc, sh, w_r, gq, bq, bd, n_prompt_blocks):
    m = x.shape[0]
    tm = TOKEN_TILE
    mod_map = lambda i, n: (_mod_block_index(i, n_prompt_blocks), 0)
    return pl.pallas_call(
        _proj_kernel,
        grid=(m // tm, P_COLS // PROJ_TN),
        in_specs=[
            pl.BlockSpec((tm, D_MODEL), lambda i, n: (i, 0)),
            pl.BlockSpec((1, D_MODEL), lambda i, n: (0, 0)),
            pl.BlockSpec((tm, D_MODEL), mod_map),
            pl.BlockSpec((tm, D_MODEL), mod_map),
            pl.BlockSpec((None, D_MODEL, PROJ_TN), lambda i, n: (n, 0, 0)),
            pl.BlockSpec((1, PROJ_TN), lambda i, n: (0, n)),
            pl.BlockSpec((1, PROJ_TN), lambda i, n: (0, n)),
            pl.BlockSpec((PROJ_TN, PROJ_TN), lambda i, n: (0, 0)),
        ],
        out_specs=[
            pl.BlockSpec((tm, PROJ_TN), lambda i, n: (i, n)),
            pl.BlockSpec((tm, PROJ_TN), lambda i, n: (i, n)),
        ],
        out_shape=[jax.ShapeDtypeStruct((m, P_COLS), F32), jax.ShapeDtypeStruct((m, P_COLS), BF16)],
        scratch_shapes=[pltpu.VMEM((tm, D_MODEL), BF16)],
        compiler_params=_cparams(2),
    )(x, gain, sc, sh, w_r, gq, bq, bd)


CUM_TILE = 256


def _cumsum_kernel(x_ref, tri_ref, o_ref, carry):
    @pl.when(pl.program_id(0) == 0)
    def _():
        carry[...] = jnp.zeros_like(carry)

    x1, x2, x3 = _split3(x_ref[...])
    tri = tri_ref[...]
    c = _dot(tri, x1) + _dot(tri, x2) + _dot(tri, x3) + carry[...]
    o_ref[...] = c
    carry[...] = c[CUM_TILE - 1:CUM_TILE, :]


def _cumsum_rows(p32, s, tri):
    return pl.pallas_call(
        _cumsum_kernel,
        grid=(s // CUM_TILE,),
        in_specs=[
            pl.BlockSpec((CUM_TILE, LANES), lambda i: (i, OFF_MISC // LANES)),
            pl.BlockSpec((CUM_TILE, CUM_TILE), lambda i: (0, 0)),
        ],
        out_specs=pl.BlockSpec((CUM_TILE, LANES), lambda i: (i, 0)),
        out_shape=jax.ShapeDtypeStruct((s, LANES), F32),
        scratch_shapes=[pltpu.VMEM((1, LANES), F32)],
        compiler_params=_cparams(1),
    )(p32, tri)


QB_A = 128
A_KEY_BLOCKS = 5


def _band_kernel(q_ref, *refs):
    k_refs = refs[:A_KEY_BLOCKS]
    v_refs = refs[A_KEY_BLOCKS:2 * A_KEY_BLOCKS]
    bias_ref, o_ref = refs[2 * A_KEY_BLOCKS:]
    i = pl.program_id(0)
    for h in range(H_A):
        hs = slice(h * HEAD_DIM, (h + 1) * HEAD_DIM)
        q = q_ref[:, hs]
        parts = []
        for j in range(A_KEY_BLOCKS):
            s = _dot_t(q, k_refs[j][:, hs]) * ATTN_SCALE + bias_ref[h, :, j * QB_A:(j + 1) * QB_A]
            in_range = i - (A_KEY_BLOCKS - 1) + j >= 0
            parts.append(jnp.where(in_range, s, NEG_INF))
        m = functools.reduce(jnp.maximum, [jnp.max(s, axis=1, keepdims=True) for s in parts])
        l = jnp.zeros((QB_A, 1), F32)
        o = jnp.zeros((QB_A, HEAD_DIM), F32)
        for j in range(A_KEY_BLOCKS):
            p = jnp.exp(parts[j] - m)
            l = l + jnp.sum(p, axis=1, keepdims=True)
            o = o + _dot(p.astype(BF16), v_refs[j][:, hs])
        o_ref[:, hs] = (o / l).astype(BF16)


def _band_prompt(pb, s, bias_a):
    nq = s // QB_A
    back = A_KEY_BLOCKS - 1

    def kv_spec(j, col):
        return pl.BlockSpec((QB_A, D_A), lambda i: (jnp.maximum(i - back + j, 0), col))

    in_specs = [pl.BlockSpec((QB_A, D_A), lambda i: (i, OFF_QA // D_A))]
    in_specs += [kv_spec(j, OFF_KA // D_A) for j in range(A_KEY_BLOCKS)]
    in_specs += [kv_spec(j, OFF_VA // D_A) for j in range(A_KEY_BLOCKS)]
    in_specs += [pl.BlockSpec((H_A, QB_A, A_KEY_BLOCKS * QB_A), lambda i: (0, 0, 0))]
    return pl.pallas_call(
        _band_kernel,
        grid=(nq,),
        in_specs=in_specs,
        out_specs=pl.BlockSpec((QB_A, D_A), lambda i: (i, 0)),
        out_shape=jax.ShapeDtypeStruct((s, D_A), BF16),
        compiler_params=_cparams(1),
    )(pb, *([pb] * (2 * A_KEY_BLOCKS)), bias_a)


def _kth_largest(count_ge, rows, k):
    def body(carry):
        it, lo, cnt_lo = carry
        cand = lo + lax.shift_left(jnp.int32(1), jnp.int32(31) - it)
        c = count_ge(cand)
        keep = c >= float(k)
        return it + 1, jnp.where(keep, cand, lo), jnp.where(keep, c, cnt_lo)

    def unsettled(carry):
        it, _, cnt_lo = carry
        return (it < 32) & (jnp.max(jnp.abs(cnt_lo - float(k))) > 0.0)

    lo0 = jnp.full((rows, 1), -(1 << 31), I32)
    _, lo, cnt = lax.while_loop(unsettled, body, (jnp.int32(0), lo0, count_ge(lo0)))
    return lo, cnt


def _tie_limit(count_eq_below, need, rows):
    def body(it, j):
        cand = j + lax.shift_left(jnp.int32(1), jnp.int32(14) - it)
        return jnp.where(count_eq_below(cand) <= need, cand, j)

    return lax.fori_loop(0, 15, body, jnp.zeros((rows, 1), I32))


QB_B = 128
KB_B = 256


def _dsa_kernel(qb_ref, qi_ref, misc_ref, kidx_ref, kb_ref, vb_ref, tb_ref, o_ref,
                key_scr, w_scr, thr_scr, m_scr, acc_scr, *, k_top):
    i = pl.program_id(0)
    n_kb = (i * QB_B) // KB_B + 1
    row = lax.broadcasted_iota(I32, (QB_B, KB_B), 0)
    col = lax.broadcasted_iota(I32, (QB_B, KB_B), 1)

    wi = misc_ref[:, MISC_WI:MISC_WI + H_IDX] * (H_IDX ** -0.5 * D_IDX ** -0.5)
    for h in range(H_IDX):
        w_scr[h] = jnp.broadcast_to(wi[:, h:h + 1], (QB_B, KB_B))

    def score_block(j, _):
        kblk = kidx_ref[pl.ds(pl.multiple_of(j * KB_B, KB_B), KB_B), :][:, :D_IDX]
        acc = jnp.zeros((QB_B, KB_B), F32)
        for h in range(H_IDX):
            isc = _dot_t(qi_ref[:, h * D_IDX:(h + 1) * D_IDX], kblk)
            acc = acc + jnp.maximum(isc, 0.0) * w_scr[h]
        admissible = (j * KB_B + col) // CHUNK <= (i * QB_B + row) // CHUNK
        key_scr[:, pl.ds(pl.multiple_of(j * KB_B, KB_B), KB_B)] = jnp.where(
            admissible, _sortable(acc), jnp.int32(NEG_KEY))
        return 0

    lax.fori_loop(0, n_kb, score_block, 0)

    def count_where(pred):
        def body(j, c):
            blk = key_scr[:, pl.ds(pl.multiple_of(j * KB_B, KB_B), KB_B)]
            return c + jnp.where(pred(blk, j * KB_B + col), 1.0, 0.0)
        c = lax.fori_loop(0, n_kb, body, jnp.zeros((QB_B, KB_B), F32))
        return jnp.sum(c, axis=1, keepdims=True)

    thr, cnt_ge = _kth_largest(lambda cand: count_where(lambda blk, idx: blk >= cand), QB_B, k_top)
    live = thr > jnp.int32(NEG_KEY)
    thr = jnp.maximum(thr, jnp.int32(NEG_KEY))
    thr_scr[0] = jnp.broadcast_to(thr, (QB_B, KB_B))
    thr_scr[1] = jnp.broadcast_to(jnp.where(live, jnp.int32(IDX_BIG), 0), (QB_B, KB_B))
    has_ties = jnp.max(jnp.where(live & (cnt_ge > float(k_top)), 1.0, 0.0)) > 0.0

    @pl.when(has_ties)
    def _():
        cnt_gt = count_where(lambda blk, idx: blk > thr)
        need = float(k_top) - cnt_gt
        jstar = _tie_limit(lambda cand: count_where(lambda blk, idx: (blk == thr) & (idx < cand)), need, QB_B)
        thr_scr[1] = jnp.broadcast_to(jnp.where(live, jstar, 0), (QB_B, KB_B))

    def mask_block(j, _):
        start = pl.multiple_of(j * KB_B, KB_B)
        keys = key_scr[:, pl.ds(start, KB_B)]
        thr_b = thr_scr[0]
        sel = (keys > thr_b) | ((keys == thr_b) & (j * KB_B + col < thr_scr[1]))
        key_scr[:, pl.ds(start, KB_B)] = lax.bitcast_convert_type(jnp.where(sel, 0.0, NEG_INF), I32)
        return 0

    lax.fori_loop(0, n_kb, mask_block, 0)

    odd = i % 2
    table_a = jnp.where(odd == 1, 3, 1)
    table_b = jnp.where(odd == 1, 0, 2)
    n_far = jnp.maximum(n_kb - 2, 0)
    ones = jnp.ones((KB_B, HEAD_DIM), BF16)
    groups = range(H_B_KV)
    heads = [[n * GQA + g for g in range(GQA)] for n in groups]
    q3 = [jnp.concatenate([qb_ref[:, h * HEAD_DIM:(h + 1) * HEAD_DIM] for h in heads[n]], axis=0) * ATTN_SCALE
          for n in groups]

    def scores(n, j, table):
        start = pl.multiple_of(j * KB_B, KB_B)
        negm = lax.bitcast_convert_type(key_scr[:, pl.ds(start, KB_B)], F32)
        k_n = kb_ref[pl.ds(start, KB_B), n * HEAD_DIM:(n + 1) * HEAD_DIM]
        s = _dot_t(q3[n], k_n).reshape(GQA, QB_B, KB_B) + negm[None]
        if table is not None:
            s = s + jnp.stack([tb_ref[table * H_B + h] for h in heads[n]])
        return s

    def max_block(j, table):
        for n in groups:
            s = scores(n, j, table)
            m_scr[n] = jnp.maximum(m_scr[n], jnp.maximum(s[:, :, :LANES], s[:, :, LANES:]))

    m_scr[...] = jnp.full(m_scr.shape, -1e30, F32)
    lax.fori_loop(0, n_far, lambda j, c: (max_block(j, None), c)[1], 0)
    pl.when(n_kb >= 2)(lambda: max_block(n_kb - 2, table_a))
    max_block(n_kb - 1, table_b)
    m_b = [jnp.broadcast_to(jnp.max(m_scr[n], axis=2, keepdims=True), (GQA, QB_B, KB_B)) for n in groups]

    def pv_block(j, table):
        start = pl.multiple_of(j * KB_B, KB_B)
        for n in groups:
            p = jnp.exp(scores(n, j, table) - m_b[n]).astype(BF16).reshape(GQA * QB_B, KB_B)
            v_n = vb_ref[pl.ds(start, KB_B), n * HEAD_DIM:(n + 1) * HEAD_DIM]
            acc_scr[n] += _dot(p, jnp.concatenate([v_n, ones], axis=1))

    acc_scr[...] = jnp.zeros(acc_scr.shape, F32)
    lax.fori_loop(0, n_far, lambda j, c: (pv_block(j, None), c)[1], 0)
    pl.when(n_kb >= 2)(lambda: pv_block(n_kb - 2, table_a))
    pv_block(n_kb - 1, table_b)
    for n in groups:
        for g, h in enumerate(heads[n]):
            a = acc_scr[n, g * QB_B:(g + 1) * QB_B, :]
            o_ref[:, h * HEAD_DIM:(h + 1) * HEAD_DIM] = (a[:, :HEAD_DIM] / a[:, HEAD_DIM:]).astype(BF16)


def _dsa_prompt(p32, pb, s, t5_tables):
    nq = s // QB_B
    k_top = min(TOPK_MAX, s // 4)
    once = pl.Buffered(1)
    return pl.pallas_call(
        functools.partial(_dsa_kernel, k_top=k_top),
        grid=(nq,),
        in_specs=[
            pl.BlockSpec((QB_B, D_B), lambda i: (i, OFF_QB // D_B)),
            pl.BlockSpec((QB_B, H_IDX * D_IDX), lambda i: (i, OFF_QI // (H_IDX * D_IDX))),
            pl.BlockSpec((QB_B, LANES), lambda i: (i, OFF_MISC // LANES)),
            pl.BlockSpec((s, LANES), lambda i: (0, OFF_MISC // LANES), pipeline_mode=once),
            pl.BlockSpec((s, D_B_KV), lambda i: (0, OFF_KB // D_B_KV), pipeline_mode=once),
            pl.BlockSpec((s, D_B_KV), lambda i: (0, OFF_VB // D_B_KV), pipeline_mode=once),
            pl.BlockSpec((4 * H_B, QB_B, KB_B), lambda i: (0, 0, 0), pipeline_mode=once),
        ],
        out_specs=pl.BlockSpec((QB_B, D_B), lambda i: (i, 0)),
        out_shape=jax.ShapeDtypeStruct((s, D_B), BF16),
        scratch_shapes=[
            pltpu.VMEM((QB_B, s), I32),
            pltpu.VMEM((H_IDX, QB_B, KB_B), F32),
            pltpu.VMEM((2, QB_B, KB_B), I32),
            pltpu.VMEM((H_B_KV, GQA, QB_B, LANES), F32),
            pltpu.VMEM((H_B_KV, GQA * QB_B, LANES), F32),
        ],
        compiler_params=_cparams(1),
    )(pb, pb, p32, pb, pb, pb, t5_tables)


QB_C = 512
HEADS_PER_STEP_C = LANES // HEAD_DIM


PREP_TILE_C = 512


def _fox_prep_kernel(q_ref, k_ref, v_ref, cum_ref, qa_ref, ka_ref, va_ref):
    rows = q_ref.shape[0]
    lane = lax.broadcasted_iota(I32, (rows, HEAD_DIM), 1)
    ones = jnp.ones((rows, HEAD_DIM), BF16)
    for h in range(H_C):
        hs = slice(h * HEAD_DIM, (h + 1) * HEAD_DIM)
        c1, c2, c3 = [c.astype(F32) for c in _split3(cum_ref[:, h:h + 1])]
        unit = jnp.where(lane < 6, 1.0, 0.0)
        q_extra = jnp.where(lane < 3, jnp.where(lane == 0, c1, jnp.where(lane == 1, c2, c3)), unit)
        k_extra = jnp.where(lane < 3, unit, -jnp.where(lane == 3, c1, jnp.where(lane == 4, c2, c3)) * unit)
        qa_ref[h] = jnp.concatenate([q_ref[:, hs] * ATTN_SCALE, q_extra.astype(BF16)], axis=1)
        ka_ref[h] = jnp.concatenate([k_ref[:, hs], k_extra.astype(BF16)], axis=1)
        va_ref[h] = jnp.concatenate([v_ref[:, hs], ones], axis=1)


def _fox_prepare(pb, s, cum):
    tm = PREP_TILE_C
    row = lambda off: pl.BlockSpec((tm, D_C), lambda i: (i, off // D_C))
    out = pl.BlockSpec((H_C, tm, LANES), lambda i: (0, i, 0))
    return pl.pallas_call(
        _fox_prep_kernel,
        grid=(s // tm,),
        in_specs=[row(OFF_QC), row(OFF_KC), row(OFF_VC), pl.BlockSpec((tm, 16), lambda i: (i, 0))],
        out_specs=[out, out, out],
        out_shape=[jax.ShapeDtypeStruct((H_C, s, LANES), BF16)] * 3,
        compiler_params=_cparams(1),
    )(pb, pb, pb, cum)


def _fox_kernel(q_ref, k_ref, v_ref, o_ref):
    i = pl.program_id(1)
    row = lax.broadcasted_iota(I32, (QB_C, QB_C), 0)
    col = lax.broadcasted_iota(I32, (QB_C, QB_C), 1)
    heads = range(HEADS_PER_STEP_C)
    q = [q_ref[hh] for hh in heads]

    def scores(hh, j):
        return _dot_t(q[hh], k_ref[hh, pl.ds(pl.multiple_of(j * QB_C, QB_C), QB_C), :])

    def halves_max(s):
        return functools.reduce(jnp.maximum, [s[:, c:c + LANES] for c in range(0, QB_C, LANES)])

    diag = [jnp.where(col <= row, scores(hh, i), NEG_INF) for hh in heads]

    def max_body(j, ms):
        return tuple(jnp.maximum(ms[hh], halves_max(scores(hh, j))) for hh in heads)

    ms = lax.fori_loop(0, i, max_body, tuple(halves_max(diag[hh]) for hh in heads))
    m_b = [jnp.broadcast_to(jnp.max(ms[hh], axis=1, keepdims=True), (QB_C, QB_C)) for hh in heads]

    def pv(hh, s, j):
        p = jnp.exp(s - m_b[hh]).astype(BF16)
        return _dot(p, v_ref[hh, pl.ds(pl.multiple_of(j * QB_C, QB_C), QB_C), :])

    def pv_body(j, accs):
        return tuple(accs[hh] + pv(hh, scores(hh, j), j) for hh in heads)

    accs = lax.fori_loop(0, i, pv_body, tuple(pv(hh, diag[hh], i) for hh in heads))
    o_ref[...] = jnp.concatenate(
        [accs[hh][:, :HEAD_DIM] / accs[hh][:, HEAD_DIM:] for hh in heads], axis=1).astype(BF16)


def _fox_prompt(qa, ka, va, s):
    nq = s // QB_C
    ng = H_C // HEADS_PER_STEP_C
    hp = HEADS_PER_STEP_C
    return pl.pallas_call(
        _fox_kernel,
        grid=(ng, nq),
        in_specs=[
            pl.BlockSpec((hp, QB_C, LANES), lambda g, i: (g, i, 0)),
            pl.BlockSpec((hp, s, LANES), lambda g, i: (g, 0, 0)),
            pl.BlockSpec((hp, s, LANES), lambda g, i: (g, 0, 0)),
        ],
        out_specs=pl.BlockSpec((QB_C, LANES), lambda g, i: (i, g)),
        out_shape=jax.ShapeDtypeStruct((s, D_C), BF16),
        compiler_params=_cparams(2),
    )(qa, ka, va)


def _softmax_pv(s, v_all):
    m = jnp.max(s, axis=1, keepdims=True)
    p = jnp.exp(s - m)
    l = jnp.sum(p, axis=1, keepdims=True)
    return _dot(p.astype(BF16), v_all) / l


def _with_new_rows(cache, new, pad_rows):
    parts = [cache.astype(BF16), new]
    if pad_rows:
        parts.append(jnp.zeros((pad_rows, new.shape[1]), BF16))
    return jnp.concatenate(parts, axis=0)


def _sample_kernel(qa_ref, ka_ref, va_ref, kvb_ref, qi_ref, qb_ref, qc_ref, kc_ref, vc_ref, misc16_ref,
                   misc32_ref, ca_ref, cb_ref, cbi_ref, cc_ref, clf_ref, clft_ref, lfn_ref, lfnt_ref,
                   bias_a_ref, t5_ref, triu_ref, tril_ref,
                   oa_ref, ob_ref, oc_ref, *, t, past, na, k_top):
    la = na + LANES
    lk = past + LANES
    pad = LANES - t

    ca = ca_ref[...]
    ka_all = _with_new_rows(ca[:, :D_A], ka_ref[...], pad)
    va_all = _with_new_rows(ca[:, D_A:], va_ref[...], pad)
    for h in range(H_A):
        hs = slice(h * HEAD_DIM, (h + 1) * HEAD_DIM)
        s = _dot_t(qa_ref[:, hs], ka_all[:, hs]) * ATTN_SCALE + bias_a_ref[h]
        oa_ref[:, hs] = _softmax_pv(s, va_all[:, hs]).astype(BF16)

    col = lax.broadcasted_iota(I32, (t, lk), 1)
    ki_all = _with_new_rows(cbi_ref[...], misc16_ref[:, :D_IDX], pad)
    wi = misc32_ref[:, MISC_WI:MISC_WI + H_IDX] * (H_IDX ** -0.5 * D_IDX ** -0.5)
    acc = jnp.zeros((t, lk), F32)
    for h in range(H_IDX):
        isc = _dot_t(qi_ref[:, h * D_IDX:(h + 1) * D_IDX], ki_all)
        acc = acc + jnp.maximum(isc, 0.0) * wi[:, h:h + 1]
    keys = jnp.where(col < past + t, _sortable(acc), jnp.int32(NEG_KEY))

    def count(pred):
        return jnp.sum(jnp.where(pred, 1.0, 0.0), axis=1, keepdims=True)

    thr, _ = _kth_largest(lambda cand: count(keys >= cand), t, k_top)
    live = thr > jnp.int32(NEG_KEY)
    thr = jnp.maximum(thr, jnp.int32(NEG_KEY))
    need = float(k_top) - count(keys > thr)
    jstar = _tie_limit(lambda cand: count((keys == thr) & (col < cand)), need, t)
    sel = (keys > thr) | ((keys == thr) & (col < jnp.where(live, jstar, 0)))

    cb = cb_ref[...]
    kb_all = _with_new_rows(cb[:, :D_B_KV], kvb_ref[:, :D_B_KV], pad)
    vb_all = _with_new_rows(cb[:, D_B_KV:], kvb_ref[:, D_B_KV:], pad)
    for n in range(H_B_KV):
        ns = slice(n * HEAD_DIM, (n + 1) * HEAD_DIM)
        for g in range(GQA):
            h = n * GQA + g
            hs = slice(h * HEAD_DIM, (h + 1) * HEAD_DIM)
            s = _dot_t(qb_ref[:, hs], kb_all[:, ns]) * ATTN_SCALE + t5_ref[h]
            s = jnp.where(sel, s, NEG_INF)
            ob_ref[:, hs] = _softmax_pv(s, vb_all[:, ns]).astype(BF16)

    cc = cc_ref[...]
    kc_all = _with_new_rows(cc[:, :D_C], kc_ref[...], pad)
    vc_all = _with_new_rows(cc[:, D_C:], vc_ref[...], pad)
    lft = jnp.concatenate([clft_ref[...], lfnt_ref[...]], axis=1)
    t1, t2, t3 = _split3(lft)
    triu = triu_ref[...]
    cum_t = _dot(t1, triu) + _dot(t2, triu) + _dot(t3, triu)
    total = jnp.sum(clf_ref[...], axis=0, keepdims=True)
    n1, n2, n3 = _split3(lfn_ref[...])
    tril = tril_ref[...]
    cum_q = total + _dot(tril, n1) + _dot(tril, n2) + _dot(tril, n3)
    row = lax.broadcasted_iota(I32, (t, lk), 0)
    causal = col <= past + row
    for h in range(H_C):
        hs = slice(h * HEAD_DIM, (h + 1) * HEAD_DIM)
        s = _dot_t(qc_ref[:, hs], kc_all[:, hs]) * ATTN_SCALE + cum_q[:, h:h + 1] - cum_t[h:h + 1, :]
        s = jnp.where(causal, s, NEG_INF)
        oc_ref[:, hs] = _softmax_pv(s, vc_all[:, hs]).astype(BF16)


def _sample_mixers(p32, pb, s, ca, cb, cbi, cc, clf, clft, lfn, lfnt, bias_a, t5_tab, triu, tril, layer):
    nb, na = ca.shape[1], ca.shape[2]
    past = cb.shape[2]
    t = lfn.shape[1]
    k_top = min(TOPK_MAX, (past + t) // 4)
    r0 = s // t
    lk = past + LANES
    la = na + LANES
    row = lambda width, off: pl.BlockSpec((t, width), lambda b: (r0 + b, off // width))
    cache = lambda rows, width: pl.BlockSpec((None, None, rows, width), lambda b: (layer, b, 0, 0))
    full = lambda shape: pl.BlockSpec(shape, lambda b: (0,) * len(shape))
    in_specs = [
        row(D_A, OFF_QA), row(D_A, OFF_KA), row(D_A, OFF_VA), row(2 * D_B_KV, OFF_KB),
        row(H_IDX * D_IDX, OFF_QI), row(D_B, OFF_QB), row(D_C, OFF_QC), row(D_C, OFF_KC), row(D_C, OFF_VC),
        row(LANES, OFF_MISC), row(LANES, OFF_MISC),
        cache(na, 2 * D_A), cache(past, 2 * D_B_KV), cache(past, D_IDX), cache(past, 2 * D_C),
        cache(past, 16), cache(16, past),
        pl.BlockSpec((None, t, 16), lambda b: (b, 0, 0)),
        pl.BlockSpec((None, 16, LANES), lambda b: (b, 0, 0)),
        full((H_A, t, la)), full((H_B, t, lk)), full((lk, lk)), full((t, t)),
    ]
    out = lambda width: pl.BlockSpec((t, width), lambda b: (b, 0))
    return pl.pallas_call(
        functools.partial(_sample_kernel, t=t, past=past, na=na, k_top=k_top),
        grid=(nb,),
        in_specs=in_specs,
        out_specs=[out(D_A), out(D_B), out(D_C)],
        out_shape=[jax.ShapeDtypeStruct((nb * t, w), BF16) for w in (D_A, D_B, D_C)],
        compiler_params=_cparams(1),
    )(pb, pb, pb, pb, pb, pb, pb, pb, pb, pb, p32, ca, cb, cbi, cc, clf, clft, lfn, lfnt,
      bias_a, t5_tab, triu, tril)


def _outproj_kernel(x_ref, g_ref, ap, bp, cp, as_, bs, cs, w_ref, o_ref, *, n_prompt_blocks):
    i = pl.program_id(0)

    def run(a, b, c):
        y = (_dot(a[...], w_ref[:D_A, :]) + _dot(b[...], w_ref[D_A:D_A + D_B, :])
             + _dot(c[...], w_ref[D_A + D_B:, :]))
        o_ref[...] = x_ref[...] + g_ref[...] * y

    pl.when(i < n_prompt_blocks)(lambda: run(ap, bp, cp))
    pl.when(i >= n_prompt_blocks)(lambda: run(as_, bs, cs))


def _out_project(x, gate, mix_p, mix_s, w_out_b, layer, n_prompt_blocks):
    m = x.shape[0]
    tm = TOKEN_TILE
    last_p = n_prompt_blocks - 1
    p_map = lambda i: (jnp.minimum(i, last_p), 0)
    s_map = lambda i: (jnp.maximum(i - n_prompt_blocks, 0), 0)
    widths = (D_A, D_B, D_C)
    return pl.pallas_call(
        functools.partial(_outproj_kernel, n_prompt_blocks=n_prompt_blocks),
        grid=(m // tm,),
        in_specs=[
            pl.BlockSpec((tm, D_MODEL), lambda i: (i, 0)),
            pl.BlockSpec((tm, D_MODEL), lambda i: (_mod_block_index(i, n_prompt_blocks), 0)),
            *[pl.BlockSpec((tm, w), p_map) for w in widths],
            *[pl.BlockSpec((tm, w), s_map) for w in widths],
            pl.BlockSpec((None, D_MODEL, D_MODEL), lambda i: (layer, 0, 0)),
        ],
        out_specs=pl.BlockSpec((tm, D_MODEL), lambda i: (i, 0)),
        out_shape=jax.ShapeDtypeStruct((m, D_MODEL), F32),
        compiler_params=_cparams(1),
    )(x, gate, *mix_p, *mix_s, w_out_b)


def _lane_pick(vals, lane, idx):
    return jnp.sum(jnp.where(lane == idx, vals, 0.0), axis=1, keepdims=True)


def _first_argmax(vals, lane):
    m = jnp.max(vals, axis=1, keepdims=True)
    idx = jnp.min(jnp.where(vals == m, lane, float(LANES)), axis=1, keepdims=True)
    return m, idx


def _router_kernel(x_ref, gain_ref, sc_ref, sh_ref, wr_ref, br_ref, tri_ref, h_ref, meta_ref, cnt_ref, carry):
    @pl.when(pl.program_id(0) == 0)
    def _():
        carry[...] = jnp.zeros_like(carry)

    h = _norm_mod(x_ref[...], gain_ref[...], sc_ref[...], sh_ref[...])
    hb = h.astype(BF16)
    h_ref[...] = h
    scores = jax.nn.sigmoid(_dot(hb, wr_ref[...]))
    lane_i = lax.broadcasted_iota(I32, scores.shape, 1)
    lane = lane_i.astype(F32)
    sel = jnp.where(lane_i < N_EXPERTS, scores + br_ref[...], NEG_INF)
    group = (lane_i // (N_EXPERTS // N_GROUPS)).astype(F32)

    best = None
    for g in range(N_GROUPS):
        in_g = jnp.where(group == float(g), sel, NEG_INF)
        m1, i1 = _first_argmax(in_g, lane)
        m2 = jnp.max(jnp.where(lane == i1, NEG_INF, in_g), axis=1, keepdims=True)
        gs = m1 + m2
        if best is None:
            best, gbest = gs, jnp.zeros_like(i1)
        else:
            better = gs > best
            best = jnp.where(better, gs, best)
            gbest = jnp.where(better, float(g), gbest)

    in_best = jnp.where(group == gbest, sel, NEG_INF)
    _, e0 = _first_argmax(in_best, lane)
    _, e1 = _first_argmax(jnp.where(lane == e0, NEG_INF, in_best), lane)
    w0 = _lane_pick(scores, lane, e0)
    w1 = _lane_pick(scores, lane, e1)
    wsum = w0 + w1

    onehot = jnp.where((lane == e0) | (lane == e1), 1.0, 0.0)
    before = _dot(tri_ref[...], onehot.astype(BF16)) + carry[...]
    r0 = _lane_pick(before, lane, e0)
    r1 = _lane_pick(before, lane, e1)
    carry[...] = carry[...] + jnp.sum(onehot, axis=0, keepdims=True)
    cnt_ref[...] = carry[...]

    meta = jnp.zeros(scores.shape, F32)
    for k, v in enumerate((e0, e1, r0, r1, w0 / wsum, w1 / wsum)):
        meta = jnp.where(lane_i == k, v, meta)
    meta_ref[...] = meta


def _router(x, gain, sc, sh, w_router_p, b_router_p, tri, n_prompt_blocks):
    m = x.shape[0]
    tm = TOKEN_TILE
    mod_map = lambda i: (_mod_block_index(i, n_prompt_blocks), 0)
    return pl.pallas_call(
        _router_kernel,
        grid=(m // tm,),
        in_specs=[
            pl.BlockSpec((tm, D_MODEL), lambda i: (i, 0)),
            pl.BlockSpec((1, D_MODEL), lambda i: (0, 0)),
            pl.BlockSpec((tm, D_MODEL), mod_map),
            pl.BlockSpec((tm, D_MODEL), mod_map),
            pl.BlockSpec((D_MODEL, LANES), lambda i: (0, 0)),
            pl.BlockSpec((1, LANES), lambda i: (0, 0)),
            pl.BlockSpec((tm, tm), lambda i: (0, 0)),
        ],
        out_specs=[
            pl.BlockSpec((tm, D_MODEL), lambda i: (i, 0)),
            pl.BlockSpec((tm, LANES), lambda i: (i, 0)),
            pl.BlockSpec((1, LANES), lambda i: (0, 0)),
        ],
        out_shape=[
            jax.ShapeDtypeStruct((m, D_MODEL), F32),
            jax.ShapeDtypeStruct((m, LANES), F32),
            jax.ShapeDtypeStruct((1, LANES), F32),
        ],
        scratch_shapes=[pltpu.VMEM((1, LANES), F32)],
        compiler_params=_cparams(1),
    )(x, gain, sc, sh, w_router_p, b_router_p, tri)


EXPERT_ROWS = 384
EXPERT_TF = 512


def _row_copy(src_hbm, src_row, dst_buf, dst_row, sem):
    return pltpu.make_async_copy(src_hbm.at[pl.ds(src_row, 1), :], dst_buf.at[pl.ds(dst_row, 1), :], sem)


def _expert_kernel(be_ref, nu_ref, st_ref, h_hbm, wg_ref, wu_ref, wd_ref, o_ref, xrows, x16, sems):
    b = pl.program_id(0)
    f = pl.program_id(1)
    n_used = nu_ref[0]
    rows_per_step = EXPERT_ROWS // (D_EXPERT // EXPERT_TF)

    def start_rows(blk, first, count):
        for r in range(count):
            tok = st_ref[blk * EXPERT_ROWS + first + r]
            _row_copy(h_hbm, tok, xrows.at[blk % 2], first + r, sems.at[blk % 2]).start()

    @pl.when((b == 0) & (f == 0))
    def _():
        start_rows(0, 0, EXPERT_ROWS)

    @pl.when((f == 0) & (b < n_used))
    def _():
        for r in range(EXPERT_ROWS):
            _row_copy(h_hbm, 0, xrows.at[b % 2], r, sems.at[b % 2]).wait()
        x16[...] = xrows[b % 2].astype(BF16)

    @pl.when(b + 1 < n_used)
    def _():
        start_rows(b + 1, f * rows_per_step, rows_per_step)

    @pl.when((f == 0) & (b >= n_used))
    def _():
        o_ref[...] = jnp.zeros(o_ref.shape, F32)

    @pl.when(b < n_used)
    def _():
        x = x16[...]
        a = _dot(x, wg_ref[...].astype(BF16))
        u = _dot(x, wu_ref[...].astype(BF16))
        act = (a * jax.nn.sigmoid(a) * u).astype(BF16)
        y = _dot(act, wd_ref[...].astype(BF16))

        @pl.when(f == 0)
        def _():
            o_ref[...] = y

        @pl.when(f > 0)
        def _():
            o_ref[...] += y


def _experts(h2, slot_tok, block_e, n_used, w_g, w_u, w_d, layer):
    nslots = slot_tok.shape[0]
    nb = nslots // EXPERT_ROWS
    nf = D_EXPERT // EXPERT_TF

    def blk(b, nu):
        return jnp.minimum(b, nu[0] - 1)

    def fidx(b, f, nu):
        return jnp.where(b < nu[0], f, nf - 1)

    grid_spec = pltpu.PrefetchScalarGridSpec(
        num_scalar_prefetch=3,
        grid=(nb, nf),
        in_specs=[
            pl.BlockSpec(memory_space=pl.ANY),
            pl.BlockSpec((None, None, D_MODEL, EXPERT_TF),
                         lambda b, f, be, nu, st: (layer, be[blk(b, nu)], 0, fidx(b, f, nu))),
            pl.BlockSpec((None, None, D_MODEL, EXPERT_TF),
                         lambda b, f, be, nu, st: (layer, be[blk(b, nu)], 0, fidx(b, f, nu))),
            pl.BlockSpec((None, None, EXPERT_TF, D_MODEL),
                         lambda b, f, be, nu, st: (layer, be[blk(b, nu)], fidx(b, f, nu), 0)),
        ],
        out_specs=pl.BlockSpec((EXPERT_ROWS, D_MODEL), lambda b, f, be, nu, st: (b, 0)),
        scratch_shapes=[
            pltpu.VMEM((2, EXPERT_ROWS, D_MODEL), F32),
            pltpu.VMEM((EXPERT_ROWS, D_MODEL), BF16),
            pltpu.SemaphoreType.DMA((2,)),
        ],
    )
    return pl.pallas_call(
        _expert_kernel,
        grid_spec=grid_spec,
        out_shape=jax.ShapeDtypeStruct((nslots, D_MODEL), F32),
        compiler_params=_cparams(2),
    )(block_e, n_used, slot_tok, h2, w_g, w_u, w_d)


COMBINE_TILE = 256


def _combine_kernel(d0_ref, d1_ref, x_ref, g_ref, meta_ref, y_hbm, o_ref, ybuf, sems):
    i = pl.program_id(0)
    n = pl.num_programs(0)

    def start_rows(blk):
        slot = blk % 2

        def body(r, _):
            t = blk * COMBINE_TILE + r
            _row_copy(y_hbm, d0_ref[t], ybuf.at[slot, 0], r, sems.at[slot]).start()
            _row_copy(y_hbm, d1_ref[t], ybuf.at[slot, 1], r, sems.at[slot]).start()
            return 0

        lax.fori_loop(0, COMBINE_TILE, body, 0, unroll=8)

    @pl.when(i == 0)
    def _():
        start_rows(i)

    slot = i % 2
    for k in range(2):
        for r in range(COMBINE_TILE):
            _row_copy(y_hbm, 0, ybuf.at[slot, k], r, sems.at[slot]).wait()

    @pl.when(i + 1 < n)
    def _():
        start_rows(i + 1)

    moe = ybuf[slot, 0] * meta_ref[:, 4:5] + ybuf[slot, 1] * meta_ref[:, 5:6]
    o_ref[...] = x_ref[...] + g_ref[...] * moe


def _combine(x, gate, meta, yb, dest0, dest1, n_prompt_blocks):
    m = x.shape[0]
    tm = COMBINE_TILE
    ratio = TOKEN_TILE // tm
    mod_map = lambda i, d0, d1: (jnp.where(i < n_prompt_blocks * ratio, 0, i - (n_prompt_blocks - 1) * ratio), 0)
    row = lambda width: pl.BlockSpec((tm, width), lambda i, d0, d1: (i, 0))
    grid_spec = pltpu.PrefetchScalarGridSpec(
        num_scalar_prefetch=2,
        grid=(m // tm,),
        in_specs=[row(D_MODEL), pl.BlockSpec((tm, D_MODEL), mod_map), row(LANES), pl.BlockSpec(memory_space=pl.ANY)],
        out_specs=row(D_MODEL),
        scratch_shapes=[pltpu.VMEM((2, 2, tm, D_MODEL), F32), pltpu.SemaphoreType.DMA((2,))],
    )
    return pl.pallas_call(
        _combine_kernel,
        grid_spec=grid_spec,
        out_shape=jax.ShapeDtypeStruct((m, D_MODEL), F32),
        compiler_params=_cparams(1),
    )(dest0, dest1, x, gate, meta, yb)


def _t5_bucket(rel):
    nb = T5_BUCKETS // 2
    max_exact = nb // 2
    ret = jnp.where(rel < 0, nb, 0)
    n = jnp.abs(rel)
    nf = jnp.maximum(n, 1).astype(F32)
    large = max_exact + (jnp.log(nf / max_exact) / math.log(T5_MAX_DIST / max_exact) * (nb - max_exact)).astype(I32)
    large = jnp.minimum(large, nb - 1)
    return ret + jnp.where(n < max_exact, n, large)


def _toeplitz(by_rel, q0, k0, nq, nk):
    d = jnp.arange(-(nk - 1), nq)
    rev = by_rel(q0 - k0 + d)[:, ::-1]
    h, length = rev.shape
    flat = jnp.tile(jnp.pad(rev, ((0, 0), (0, 1))), (1, nq))[:, :nq * length]
    return flat.reshape(h, nq, length)[:, :, nq - 1:nq - 1 + nk]


def _band_bias(table, q0, k0, nq, nk, n_valid):
    by_rel = lambda rel: table.astype(F32)[:, jnp.clip(rel, -A_REL_FUTURE, A_REL_PAST) + A_REL_FUTURE]
    qc = (q0 + jnp.arange(nq))[:, None] // CHUNK
    col = jnp.arange(nk)[None, :]
    kc = (k0 + col) // CHUNK
    ok = (col < n_valid) & (kc <= qc) & (kc >= qc - A_LEFT_CHUNKS)
    return jnp.where(ok[None], _toeplitz(by_rel, q0, k0, nq, nk), NEG_INF)


def _t5_bias(t5, q0, k0, nq, nk):
    return _toeplitz(lambda rel: t5.astype(F32)[_t5_bucket(rel)].T, q0, k0, nq, nk)


def _relayout_w_in(w_in_l):
    qa, ka, va, qb, kb, vb, qi, ki, wi, qc, kc, vc, fc = jnp.split(w_in_l, PROJ_SPLITS, axis=1)
    pad = jnp.zeros((D_MODEL, P_COLS - OFF_MISC - D_IDX - H_IDX - H_C), w_in_l.dtype)
    w = jnp.concatenate([qa, ka, va, kb, vb, qi, qb, qc, kc, vc, ki, wi, fc, pad], axis=1).astype(BF16)
    return jnp.swapaxes(w.reshape(D_MODEL, P_COLS // PROJ_TN, PROJ_TN), 0, 1)


def _column_params(qk_gain_l, b_forget_l):
    ones = lambda n: jnp.ones((n,), F32)
    g = qk_gain_l.astype(F32)
    gq = jnp.concatenate([
        jnp.tile(g[0], H_A), jnp.tile(g[1], H_A), ones(D_A),
        jnp.tile(g[3], H_B_KV), ones(D_B_KV), ones(H_IDX * D_IDX),
        jnp.tile(g[2], H_B), jnp.tile(g[4], H_C), jnp.tile(g[5], H_C), ones(D_C), ones(P_COLS - OFF_MISC)])
    bq = jnp.zeros((P_COLS,), F32).at[OFF_MISC + MISC_FC:OFF_MISC + MISC_FC + H_C].set(b_forget_l.astype(F32))
    return gq[None, :], bq[None, :]


def _tri(n, *, strict=False, upper=False):
    r = jnp.arange(n)[:, None]
    c = jnp.arange(n)[None, :]
    m = (r < c if strict else r <= c) if upper else (c < r if strict else c <= r)
    return m.astype(BF16)


def kernel(x_prompt, x_sample, c_prompt, c_sample, cache_a_kv, cache_b_kv, cache_b_kidx, cache_c_kv, cache_c_logf,
           w_ada, b_ada, norm_gain, w_in, b_forget, qk_gain, rel_bias_a, t5_bias, w_out, w_router, b_router,
           w_e_gate, w_e_up, w_e_down):
    depth = w_in.shape[0]
    bp, s, d = x_prompt.shape
    nb, t, _ = x_sample.shape
    ns = nb * t
    tm = TOKEN_TILE
    assert bp == 1 and d == D_MODEL and s % tm == 0 and ns % tm == 0 and s % QB_C == 0
    n_pb = s // tm
    m = s + ns
    na = cache_a_kv.shape[2]
    past = cache_b_kv.shape[2]
    keep = min(A_LEFT_CHUNKS * CHUNK, s)

    x = jnp.concatenate([x_prompt.reshape(s, d), x_sample.reshape(ns, d)], axis=0)
    c_all = jnp.concatenate([c_prompt, c_sample, jnp.zeros((-(bp + nb) % 8, d), F32)], axis=0)

    t5_far = t5_bias.astype(F32)[T5_BUCKETS // 2 - 1]
    t5_tables = jnp.concatenate(
        [_t5_bias(t5_bias, off, 0, QB_B, KB_B) - t5_far[:, None, None] for off in (QB_B, 2 * QB_B, 0)]
        + [jnp.zeros((H_B, QB_B, KB_B), F32)], axis=0)
    t5_s = _t5_bias(t5_bias, past, 0, t, past + LANES)
    bd = ((jnp.arange(PROJ_TN)[:, None] // HEAD_DIM == jnp.arange(PROJ_TN)[None, :] // HEAD_DIM)
          .astype(F32) / HEAD_DIM).astype(BF16)
    tri_cum = _tri(CUM_TILE)
    tri_rank = _tri(tm, strict=True)
    triu_s = _tri(past + LANES, upper=True)
    tril_s = _tri(t)
    w_out_b = w_out.astype(BF16)
    w_router_p = jnp.pad(w_router, ((0, 0), (0, LANES - N_EXPERTS))).astype(BF16)
    b_router_p = jnp.pad(b_router.astype(F32), (0, LANES - N_EXPERTS))[None, :]

    ca = cache_a_kv.reshape(depth, nb, na, 2 * D_A)
    cb = cache_b_kv.reshape(depth, nb, past, 2 * D_B_KV)
    cc = cache_c_kv.reshape(depth, nb, past, 2 * D_C)
    clf = jnp.pad(cache_c_logf.astype(F32), ((0, 0), (0, 0), (0, 0), (0, 16 - H_C)))
    clft = jnp.swapaxes(clf, 2, 3)

    n_asg = 2 * m
    n_eb = -(-n_asg // EXPERT_ROWS) + N_EXPERTS
    tok = jnp.arange(m, dtype=I32)

    states_p, states_s = [], []
    for l in range(depth):
        mod = _ada(c_all, w_ada, b_ada, l)
        mods = []
        for part in jnp.split(mod, 6, axis=1):
            mods.append(jnp.concatenate([jnp.broadcast_to(part[:1], (tm, d)), jnp.repeat(part[bp:bp + nb], t, axis=0)], 0))
        sh1, sc1, g1, sh2, sc2, g2 = mods

        gq, bq = _column_params(qk_gain[l], b_forget[l])
        p32, pb = _project(x, norm_gain[l, 0][None, :], sc1, sh1, _relayout_w_in(w_in[l]), gq, bq, bd, n_pb)

        bias_a_p = _band_bias(rel_bias_a[l], (A_KEY_BLOCKS - 1) * QB_A, 0, QB_A, A_KEY_BLOCKS * QB_A,
                              A_KEY_BLOCKS * QB_A)
        oa_p = _band_prompt(pb, s, bias_a_p)
        ob_p = _dsa_prompt(p32, pb, s, t5_tables)
        cum = _cumsum_rows(p32, s, tri_cum)[:, MISC_FC:MISC_FC + 16]
        oc_p = _fox_prompt(*_fox_prepare(pb, s, cum), s)

        lfn = p32[s:, OFF_MISC + MISC_FC:OFF_MISC + MISC_FC + 16].reshape(nb, t, 16)
        lfnt = jnp.pad(jnp.swapaxes(lfn, 1, 2), ((0, 0), (0, 0), (0, LANES - t)))
        bias_a_s = _band_bias(rel_bias_a[l], past, past - na, t, na + LANES, na + t)
        oa_s, ob_s, oc_s = _sample_mixers(p32, pb, s, ca, cb, cache_b_kidx, cc, clf, clft, lfn, lfnt,
                                          bias_a_s, t5_s, triu_s, tril_s, l)

        x = _out_project(x, g1, (oa_p, ob_p, oc_p), (oa_s, ob_s, oc_s), w_out_b, l, n_pb)

        h2, meta, counts = _router(x, norm_gain[l, 1][None, :], sc2, sh2, w_router_p, b_router_p, tri_rank, n_pb)
        e0 = meta[:, 0].astype(I32)
        e1 = meta[:, 1].astype(I32)
        counts = counts[0, :N_EXPERTS].astype(I32)
        padded = (counts + EXPERT_ROWS - 1) // EXPERT_ROWS * EXPERT_ROWS
        pend = jnp.cumsum(padded)
        pstart = pend - padded
        dest0 = pstart[e0] + meta[:, 2].astype(I32)
        dest1 = pstart[e1] + meta[:, 3].astype(I32)
        slot_tok = jnp.zeros((n_eb * EXPERT_ROWS,), I32).at[dest0].set(tok).at[dest1].set(tok)
        block_e = jnp.minimum(jnp.searchsorted(pend, jnp.arange(n_eb, dtype=I32) * EXPERT_ROWS, side='right'),
                              N_EXPERTS - 1).astype(I32)
        n_used = (pend[-1:] // EXPERT_ROWS).astype(I32)
        yb = _experts(h2, slot_tok, block_e, n_used, w_e_gate, w_e_up, w_e_down, l)
        x = _combine(x, g2, meta, yb, dest0, dest1, n_pb)

        def states(rows, nbatch, a_rows):
            r = p32[rows]
            n = r.shape[0] // nbatch
            kv = lambda off, heads: r[:, off:off + 2 * heads * HEAD_DIM].reshape(nbatch, n, 2, heads, HEAD_DIM)
            return (kv(OFF_KA, H_A)[:, n - a_rows:], kv(OFF_KB, H_B_KV),
                    r[:, OFF_MISC:OFF_MISC + D_IDX].reshape(nbatch, n, D_IDX), kv(OFF_KC, H_C),
                    r[:, OFF_MISC + MISC_FC:OFF_MISC + MISC_FC + H_C].reshape(nbatch, n, H_C))

        states_p.append(states(slice(0, s), bp, keep))
        states_s.append(states(slice(s, m), nb, t))

    stk = lambda sts, i: jnp.stack([st[i] for st in sts], axis=0)
    return (x[:s].reshape(bp, s, d), x[s:].reshape(nb, t, d),
            *[stk(states_p, i) for i in range(5)], *[stk(states_s, i) for i in range(5)])
```

```python
import functools
import math

import jax
import jax.numpy as jnp
from jax import lax
from jax.experimental import pallas as pl
from jax.experimental.pallas import tpu as pltpu

F32 = jnp.float32
BF16 = jnp.bfloat16
I32 = jnp.int32

D_MODEL = 2048
HEAD_DIM = 64
CHUNK = 64
EPS = 1e-6
ATTN_SCALE = HEAD_DIM ** -0.5
H_A = 8
A_LEFT_CHUNKS = 8
A_REL_PAST = 128
A_REL_FUTURE = CHUNK - 1
H_B = 12
H_B_KV = 4
GQA = H_B // H_B_KV
H_IDX = 16
D_IDX = 64
TOPK_MAX = 256
H_C = 12
T5_BUCKETS = 32
T5_MAX_DIST = 128
N_EXPERTS = 32
N_GROUPS = 4
D_EXPERT = 1024

D_A = H_A * HEAD_DIM
D_B = H_B * HEAD_DIM
D_B_KV = H_B_KV * HEAD_DIM
D_C = H_C * HEAD_DIM
PROJ_SIZES = (D_A, D_A, D_A, D_B, D_B_KV, D_B_KV, H_IDX * D_IDX, D_IDX, H_IDX, D_C, D_C, D_C, H_C)
PROJ_SPLITS = tuple(sum(PROJ_SIZES[:i + 1]) for i in range(len(PROJ_SIZES) - 1))

OFF_QA, OFF_KA, OFF_VA = 0, 512, 1024
OFF_KB, OFF_VB = 1536, 1792
OFF_QI = 2048
OFF_QB = 3072
OFF_QC, OFF_KC, OFF_VC = 3840, 4608, 5376
OFF_MISC = 6144
MISC_WI = 64
MISC_FC = 80
P_COLS = 6400
PROJ_TN = 256
NORM_COL_BLOCKS = (0, 1, 2, 3, 6, 12, 13, 14, 15, 16, 17, 18, 19, 20)
MISC_COL_BLOCK = OFF_MISC // PROJ_TN

TOKEN_TILE = 512
LANES = 128
VMEM_LIMIT = 56 * 1024 * 1024

NEG_INF = float("-inf")
SOFTMAX_SUM_FLOOR = 2.0 ** -40
BOUND_SLACK = 1.001
NEG_KEY = (0xFF800000 ^ 0x7FFFFFFF) - (1 << 32)
IDX_BIG = 1 << 30


def _cparams(n_axes):
    return pltpu.CompilerParams(dimension_semantics=("arbitrary",) * n_axes, vmem_limit_bytes=VMEM_LIMIT)


def _split3(x):
    x1 = x.astype(BF16)
    r1 = x - x1.astype(F32)
    x2 = r1.astype(BF16)
    r2 = r1 - x2.astype(F32)
    return x1, x2, r2.astype(BF16)


def _dot(a, b):
    return jnp.dot(a, b, preferred_element_type=F32)


def _dot_t(a, b):
    return lax.dot_general(a, b, (((1,), (1,)), ((), ())), preferred_element_type=F32)


def _sortable(x):
    b = lax.bitcast_convert_type(x, I32)
    return jnp.where(b < 0, b ^ jnp.int32(0x7FFFFFFF), b)


def _ada_kernel(c_ref, w_ref, b_ref, o_ref):
    c = c_ref[...]
    a = (c * jax.nn.sigmoid(c)).astype(BF16)
    o_ref[...] = _dot(a, w_ref[...].astype(BF16)) + b_ref[...]


def _ada(c_all, w_ada, b_ada, layer):
    rows = c_all.shape[0]
    n_out = w_ada.shape[2]
    tn = 1024
    return pl.pallas_call(
        _ada_kernel,
        grid=(n_out // tn,),
        in_specs=[
            pl.BlockSpec((rows, D_MODEL), lambda n: (0, 0)),
            pl.BlockSpec((None, D_MODEL, tn), lambda n: (layer, 0, n)),
            pl.BlockSpec((None, 1, tn), lambda n: (layer, 0, n)),
        ],
        out_specs=pl.BlockSpec((rows, tn), lambda n: (0, n)),
        out_shape=jax.ShapeDtypeStruct((rows, n_out), F32),
        compiler_params=_cparams(1),
    )(c_all, w_ada, b_ada.reshape(b_ada.shape[0], 1, n_out))


def _mod_block_index(i, n_prompt_blocks):
    return jnp.where(i < n_prompt_blocks, 0, i - n_prompt_blocks + 1)


def _norm_mod(x, gain, sc, sh):
    ms = jnp.mean(x * x, axis=-1, keepdims=True)
    return (x * lax.rsqrt(ms + EPS) * gain) * (1.0 + sc) + sh


def _proj_kernel(x_ref, gain_ref, sc_ref, sh_ref, w_ref, gq_ref, bq_ref, bd_ref, o32_ref, o16_ref, h_scr):
    n = pl.program_id(1)

    @pl.when(n == 0)
    def _():
        h_scr[...] = _norm_mod(x_ref[...], gain_ref[...], sc_ref[...], sh_ref[...]).astype(BF16)

    y = _dot(h_scr[...], w_ref[...])

    is_norm = functools.reduce(jnp.logical_or, [n == b for b in NORM_COL_BLOCKS])
    is_misc = n == MISC_COL_BLOCK

    def emit(v):
        o32_ref[...] = v
        o16_ref[...] = v.astype(BF16)

    @pl.when(is_norm)
    def _():
        s1, s2, s3 = _split3(y * y)
        bd = bd_ref[...]
        ms = _dot(s1, bd) + _dot(s2, bd) + _dot(s3, bd)
        emit(y * lax.rsqrt(ms + EPS) * gq_ref[...])

    @pl.when(is_misc)
    def _():
        lane = lax.broadcasted_iota(I32, y.shape, 1)
        z = y + bq_ref[...]
        logsig = jnp.minimum(z, 0.0) - jnp.log(1.0 + jnp.exp(-jnp.abs(z)))
        emit(jnp.where((lane >= MISC_FC) & (lane < MISC_FC + H_C), logsig, y))

    @pl.when(jnp.logical_not(jnp.logical_or(is_norm, is_misc)))
    def _():
        emit(y)


def _project(x, gain, sc, sh, w_r, gq, bq, bd, n_prompt_blocks):
    m = x.shape[0]
    tm = TOKEN_TILE
    mod_map = lambda i, n: (_mod_block_index(i, n_prompt_blocks), 0)
    return pl.pallas_call(
        _proj_kernel,
        grid=(m // tm, P_COLS // PROJ_TN),
        in_specs=[
            pl.BlockSpec((tm, D_MODEL), lambda i, n: (i, 0)),
            pl.BlockSpec((1, D_MODEL), lambda i, n: (0, 0)),
            pl.BlockSpec((tm, D_MODEL), mod_map),
            pl.BlockSpec((tm, D_MODEL), mod_map),
            pl.BlockSpec((None, D_MODEL, PROJ_TN), lambda i, n: (n, 0, 0)),
            pl.BlockSpec((1, PROJ_TN), lambda i, n: (0, n)),
            pl.BlockSpec((1, PROJ_TN), lambda i, n: (0, n)),
            pl.BlockSpec((PROJ_TN, PROJ_TN), lambda i, n: (0, 0)),
        ],
        out_specs=[
            pl.BlockSpec((tm, PROJ_TN), lambda i, n: (i, n)),
            pl.BlockSpec((tm, PROJ_TN), lambda i, n: (i, n)),
        ],
        out_shape=[jax.ShapeDtypeStruct((m, P_COLS), F32), jax.ShapeDtypeStruct((m, P_COLS), BF16)],
        scratch_shapes=[pltpu.VMEM((tm, D_MODEL), BF16)],
        compiler_params=_cparams(2),
    )(x, gain, sc, sh, w_r, gq, bq, bd)


CUM_TILE = 256


def _cumsum_kernel(x_ref, tri_ref, o_ref, carry):
    @pl.when(pl.program_id(0) == 0)
    def _():
        carry[...] = jnp.zeros_like(carry)

    x1, x2, x3 = _split3(x_ref[...])
    tri = tri_ref[...]
    c = _dot(tri, x1) + _dot(tri, x2) + _dot(tri, x3) + carry[...]
    o_ref[...] = c
    carry[...] = c[CUM_TILE - 1:CUM_TILE, :]


def _cumsum_rows(p32, s, tri):
    return pl.pallas_call(
        _cumsum_kernel,
        grid=(s // CUM_TILE,),
        in_specs=[
            pl.BlockSpec((CUM_TILE, LANES), lambda i: (i, OFF_MISC // LANES)),
            pl.BlockSpec((CUM_TILE, CUM_TILE), lambda i: (0, 0)),
        ],
        out_specs=pl.BlockSpec((CUM_TILE, LANES), lambda i: (i, 0)),
        out_shape=jax.ShapeDtypeStruct((s, LANES), F32),
        scratch_shapes=[pltpu.VMEM((1, LANES), F32)],
        compiler_params=_cparams(1),
    )(p32, tri)


QB_A = 128
A_KEY_BLOCKS = 5


def _band_kernel(q_ref, *refs):
    k_refs = refs[:A_KEY_BLOCKS]
    v_refs = refs[A_KEY_BLOCKS:2 * A_KEY_BLOCKS]
    bias_ref, o_ref = refs[2 * A_KEY_BLOCKS:]
    i = pl.program_id(0)
    for h in range(H_A):
        hs = slice(h * HEAD_DIM, (h + 1) * HEAD_DIM)
        q = q_ref[:, hs]
        parts = []
        for j in range(A_KEY_BLOCKS):
            s = _dot_t(q, k_refs[j][:, hs]) * ATTN_SCALE + bias_ref[h, :, j * QB_A:(j + 1) * QB_A]
            in_range = i - (A_KEY_BLOCKS - 1) + j >= 0
            parts.append(jnp.where(in_range, s, NEG_INF))
        m = functools.reduce(jnp.maximum, [jnp.max(s, axis=1, keepdims=True) for s in parts])
        l = jnp.zeros((QB_A, 1), F32)
        o = jnp.zeros((QB_A, HEAD_DIM), F32)
        for j in range(A_KEY_BLOCKS):
            p = jnp.exp(parts[j] - m)
            l = l + jnp.sum(p, axis=1, keepdims=True)
            o = o + _dot(p.astype(BF16), v_refs[j][:, hs])
        o_ref[:, hs] = (o / l).astype(BF16)


def _band_prompt(pb, s, bias_a):
    nq = s // QB_A
    back = A_KEY_BLOCKS - 1

    def kv_spec(j, col):
        return pl.BlockSpec((QB_A, D_A), lambda i: (jnp.maximum(i - back + j, 0), col))

    in_specs = [pl.BlockSpec((QB_A, D_A), lambda i: (i, OFF_QA // D_A))]
    in_specs += [kv_spec(j, OFF_KA // D_A) for j in range(A_KEY_BLOCKS)]
    in_specs += [kv_spec(j, OFF_VA // D_A) for j in range(A_KEY_BLOCKS)]
    in_specs += [pl.BlockSpec((H_A, QB_A, A_KEY_BLOCKS * QB_A), lambda i: (0, 0, 0))]
    return pl.pallas_call(
        _band_kernel,
        grid=(nq,),
        in_specs=in_specs,
        out_specs=pl.BlockSpec((QB_A, D_A), lambda i: (i, 0)),
        out_shape=jax.ShapeDtypeStruct((s, D_A), BF16),
        compiler_params=_cparams(1),
    )(pb, *([pb] * (2 * A_KEY_BLOCKS)), bias_a)


def _kth_largest(count_ge, rows, k):
    def body(carry):
        it, lo, cnt_lo = carry
        cand = lo + lax.shift_left(jnp.int32(1), jnp.int32(31) - it)
        c = count_ge(cand)
        keep = c >= float(k)
        return it + 1, jnp.where(keep, cand, lo), jnp.where(keep, c, cnt_lo)

    def unsettled(carry):
        it, _, cnt_lo = carry
        return (it < 32) & (jnp.max(jnp.abs(cnt_lo - float(k))) > 0.0)

    lo0 = jnp.full((rows, 1), -(1 << 31), I32)
    _, lo, cnt = lax.while_loop(unsettled, body, (jnp.int32(0), lo0, count_ge(lo0)))
    return lo, cnt


def _tie_limit(count_eq_below, need, rows):
    def body(it, j):
        cand = j + lax.shift_left(jnp.int32(1), jnp.int32(14) - it)
        return jnp.where(count_eq_below(cand) <= need, cand, j)

    return lax.fori_loop(0, 15, body, jnp.zeros((rows, 1), I32))


QB_B = 128
KB_B = 256


def _dsa_kernel(bnd_ref, qb_ref, qi_ref, misc_ref, kidx_ref, kb_ref, vb_ref, tb_ref, o_ref,
                key_scr, w_scr, thr_scr, m_scr, acc_scr, *, k_top):
    i = pl.program_id(0)
    n_kb = (i * QB_B) // KB_B + 1
    row = lax.broadcasted_iota(I32, (QB_B, KB_B), 0)
    col = lax.broadcasted_iota(I32, (QB_B, KB_B), 1)

    wi = misc_ref[:, MISC_WI:MISC_WI + H_IDX] * (H_IDX ** -0.5 * D_IDX ** -0.5)
    for h in range(H_IDX):
        w_scr[h] = jnp.broadcast_to(wi[:, h:h + 1], (QB_B, KB_B))

    def score_block(j, _):
        kblk = kidx_ref[pl.ds(pl.multiple_of(j * KB_B, KB_B), KB_B), :][:, :D_IDX]
        acc = jnp.zeros((QB_B, KB_B), F32)
        for h in range(H_IDX):
            isc = _dot_t(qi_ref[:, h * D_IDX:(h + 1) * D_IDX], kblk)
            acc = acc + jnp.maximum(isc, 0.0) * w_scr[h]
        admissible = (j * KB_B + col) // CHUNK <= (i * QB_B + row) // CHUNK
        key_scr[:, pl.ds(pl.multiple_of(j * KB_B, KB_B), KB_B)] = jnp.where(
            admissible, _sortable(acc), jnp.int32(NEG_KEY))
        return 0

    lax.fori_loop(0, n_kb, score_block, 0)

    def count_where(pred):
        def body(j, c):
            blk = key_scr[:, pl.ds(pl.multiple_of(j * KB_B, KB_B), KB_B)]
            return c + jnp.where(pred(blk, j * KB_B + col), 1.0, 0.0)
        c = lax.fori_loop(0, n_kb, body, jnp.zeros((QB_B, KB_B), F32))
        return jnp.sum(c, axis=1, keepdims=True)

    thr, cnt_ge = _kth_largest(lambda cand: count_where(lambda blk, idx: blk >= cand), QB_B, k_top)
    live = thr > jnp.int32(NEG_KEY)
    thr = jnp.maximum(thr, jnp.int32(NEG_KEY))
    thr_scr[0] = jnp.broadcast_to(thr, (QB_B, KB_B))
    thr_scr[1] = jnp.broadcast_to(jnp.where(live, jnp.int32(IDX_BIG), 0), (QB_B, KB_B))
    has_ties = jnp.max(jnp.where(live & (cnt_ge > float(k_top)), 1.0, 0.0)) > 0.0

    @pl.when(has_ties)
    def _():
        cnt_gt = count_where(lambda blk, idx: blk > thr)
        need = float(k_top) - cnt_gt
        jstar = _tie_limit(lambda cand: count_where(lambda blk, idx: (blk == thr) & (idx < cand)), need, QB_B)
        thr_scr[1] = jnp.broadcast_to(jnp.where(live, jstar, 0), (QB_B, KB_B))

    def mask_block(j, _):
        start = pl.multiple_of(j * KB_B, KB_B)
        keys = key_scr[:, pl.ds(start, KB_B)]
        thr_b = thr_scr[0]
        sel = (keys > thr_b) | ((keys == thr_b) & (j * KB_B + col < thr_scr[1]))
        key_scr[:, pl.ds(start, KB_B)] = lax.bitcast_convert_type(jnp.where(sel, 0.0, NEG_INF), I32)
        return 0

    lax.fori_loop(0, n_kb, mask_block, 0)

    odd = i % 2
    table_a = jnp.where(odd == 1, 3, 1)
    table_b = jnp.where(odd == 1, 0, 2)
    n_far = jnp.maximum(n_kb - 2, 0)
    ones = jnp.ones((KB_B, HEAD_DIM), BF16)
    groups = range(H_B_KV)
    heads = [[n * GQA + g for g in range(GQA)] for n in groups]
    q3 = [jnp.concatenate([qb_ref[:, h * HEAD_DIM:(h + 1) * HEAD_DIM] for h in heads[n]], axis=0) * ATTN_SCALE
          for n in groups]

    def scores(n, j, table):
        start = pl.multiple_of(j * KB_B, KB_B)
        negm = lax.bitcast_convert_type(key_scr[:, pl.ds(start, KB_B)], F32)
        k_n = kb_ref[pl.ds(start, KB_B), n * HEAD_DIM:(n + 1) * HEAD_DIM]
        s = _dot_t(q3[n], k_n).reshape(GQA, QB_B, KB_B) + negm[None]
        if table is not None:
            s = s + jnp.stack([tb_ref[table * H_B + h] for h in heads[n]])
        return s

    def over_blocks(block_fn):
        lax.fori_loop(0, n_far, lambda j, c: (block_fn(j, None), c)[1], 0)
        pl.when(n_kb >= 2)(lambda: block_fn(n_kb - 2, table_a))
        block_fn(n_kb - 1, table_b)

    def exp_pv_pass(m_b):
        def pv_block(j, table):
            start = pl.multiple_of(j * KB_B, KB_B)
            for n in groups:
                p = jnp.exp(scores(n, j, table) - m_b[n]).astype(BF16).reshape(GQA * QB_B, KB_B)
                v_n = vb_ref[pl.ds(start, KB_B), n * HEAD_DIM:(n + 1) * HEAD_DIM]
                acc_scr[n] += _dot(p, jnp.concatenate([v_n, ones], axis=1))

        acc_scr[...] = jnp.zeros(acc_scr.shape, F32)
        over_blocks(pv_block)

    def spread(m):
        return jnp.broadcast_to(m, (GQA, QB_B, KB_B))

    def bound(n):
        qn = jnp.sqrt(jnp.sum(jnp.square(q3[n].astype(F32)), axis=1, keepdims=True)) * bnd_ref[n]
        return spread(jnp.stack([qn[g * QB_B:(g + 1) * QB_B] + bnd_ref[H_B_KV + h] for g, h in enumerate(heads[n])]))

    exp_pv_pass([bound(n) for n in groups])
    smallest = functools.reduce(jnp.minimum, [jnp.min(acc_scr[n][:, HEAD_DIM:HEAD_DIM + 1]) for n in groups])

    @pl.when(jnp.logical_not(smallest >= SOFTMAX_SUM_FLOOR))
    def _():
        def max_block(j, table):
            for n in groups:
                s = scores(n, j, table)
                m_scr[n] = jnp.maximum(m_scr[n], jnp.maximum(s[:, :, :LANES], s[:, :, LANES:]))

        m_scr[...] = jnp.full(m_scr.shape, -1e30, F32)
        over_blocks(max_block)
        exp_pv_pass([spread(jnp.max(m_scr[n], axis=2, keepdims=True)) for n in groups])

    for n in groups:
        for g, h in enumerate(heads[n]):
            a = acc_scr[n, g * QB_B:(g + 1) * QB_B, :]
            o_ref[:, h * HEAD_DIM:(h + 1) * HEAD_DIM] = (a[:, :HEAD_DIM] / a[:, HEAD_DIM:]).astype(BF16)


def _key_norm_max(pb, s, off, heads):
    k = pb[:s, off:off + heads * HEAD_DIM].astype(F32).reshape(s, heads, HEAD_DIM)
    return jnp.sqrt(jnp.max(jnp.sum(k * k, axis=2), axis=0)) * BOUND_SLACK


def _dsa_prompt(p32, pb, s, t5_tables):
    nq = s // QB_B
    k_top = min(TOPK_MAX, s // 4)
    once = pl.Buffered(1)
    bias_max = jnp.maximum(jnp.max(t5_tables.reshape(4, H_B, -1), axis=(0, 2)), 0.0)
    bounds = jnp.concatenate([_key_norm_max(pb, s, OFF_KB, H_B_KV), bias_max])
    return pl.pallas_call(
        functools.partial(_dsa_kernel, k_top=k_top),
        grid=(nq,),
        in_specs=[
            pl.BlockSpec(memory_space=pltpu.SMEM),
            pl.BlockSpec((QB_B, D_B), lambda i: (i, OFF_QB // D_B)),
            pl.BlockSpec((QB_B, H_IDX * D_IDX), lambda i: (i, OFF_QI // (H_IDX * D_IDX))),
            pl.BlockSpec((QB_B, LANES), lambda i: (i, OFF_MISC // LANES)),
            pl.BlockSpec((s, LANES), lambda i: (0, OFF_MISC // LANES), pipeline_mode=once),
            pl.BlockSpec((s, D_B_KV), lambda i: (0, OFF_KB // D_B_KV), pipeline_mode=once),
            pl.BlockSpec((s, D_B_KV), lambda i: (0, OFF_VB // D_B_KV), pipeline_mode=once),
            pl.BlockSpec((4 * H_B, QB_B, KB_B), lambda i: (0, 0, 0), pipeline_mode=once),
        ],
        out_specs=pl.BlockSpec((QB_B, D_B), lambda i: (i, 0)),
        out_shape=jax.ShapeDtypeStruct((s, D_B), BF16),
        scratch_shapes=[
            pltpu.VMEM((QB_B, s), I32),
            pltpu.VMEM((H_IDX, QB_B, KB_B), F32),
            pltpu.VMEM((2, QB_B, KB_B), I32),
            pltpu.VMEM((H_B_KV, GQA, QB_B, LANES), F32),
            pltpu.VMEM((H_B_KV, GQA * QB_B, LANES), F32),
        ],
        compiler_params=_cparams(1),
    )(bounds, pb, pb, p32, pb, pb, pb, t5_tables)


QB_C = 512
HEADS_PER_STEP_C = LANES // HEAD_DIM


PREP_TILE_C = 512


def _fox_prep_kernel(q_ref, k_ref, v_ref, cum_ref, qa_ref, ka_ref, va_ref):
    rows = q_ref.shape[0]
    lane = lax.broadcasted_iota(I32, (rows, HEAD_DIM), 1)
    ones = jnp.ones((rows, HEAD_DIM), BF16)
    for h in range(H_C):
        hs = slice(h * HEAD_DIM, (h + 1) * HEAD_DIM)
        c1, c2, c3 = [c.astype(F32) for c in _split3(cum_ref[:, h:h + 1])]
        unit = jnp.where(lane < 6, 1.0, 0.0)
        q_extra = jnp.where(lane < 3, jnp.where(lane == 0, c1, jnp.where(lane == 1, c2, c3)), unit)
        k_extra = jnp.where(lane < 3, unit, -jnp.where(lane == 3, c1, jnp.where(lane == 4, c2, c3)) * unit)
        qa_ref[h] = jnp.concatenate([q_ref[:, hs] * ATTN_SCALE, q_extra.astype(BF16)], axis=1)
        ka_ref[h] = jnp.concatenate([k_ref[:, hs], k_extra.astype(BF16)], axis=1)
        va_ref[h] = jnp.concatenate([v_ref[:, hs], ones], axis=1)


def _fox_prepare(pb, s, cum):
    tm = PREP_TILE_C
    row = lambda off: pl.BlockSpec((tm, D_C), lambda i: (i, off // D_C))
    out = pl.BlockSpec((H_C, tm, LANES), lambda i: (0, i, 0))
    return pl.pallas_call(
        _fox_prep_kernel,
        grid=(s // tm,),
        in_specs=[row(OFF_QC), row(OFF_KC), row(OFF_VC), pl.BlockSpec((tm, 16), lambda i: (i, 0))],
        out_specs=[out, out, out],
        out_shape=[jax.ShapeDtypeStruct((H_C, s, LANES), BF16)] * 3,
        compiler_params=_cparams(1),
    )(pb, pb, pb, cum)


def _fox_kernel(kmax_ref, q_ref, k_ref, v_ref, o_ref):
    g = pl.program_id(0)
    i = pl.program_id(1)
    row = lax.broadcasted_iota(I32, (QB_C, QB_C), 0)
    col = lax.broadcasted_iota(I32, (QB_C, QB_C), 1)
    heads = range(HEADS_PER_STEP_C)
    q = [q_ref[hh] for hh in heads]

    def scores(hh, j):
        return _dot_t(q[hh], k_ref[hh, pl.ds(pl.multiple_of(j * QB_C, QB_C), QB_C), :])

    def halves_max(s):
        return functools.reduce(jnp.maximum, [s[:, c:c + LANES] for c in range(0, QB_C, LANES)])

    def spread(m):
        return jnp.broadcast_to(m, (QB_C, QB_C))

    diag = [jnp.where(col <= row, scores(hh, i), NEG_INF) for hh in heads]

    def exp_pv_pass(m_b):
        def pv(hh, s, j):
            p = jnp.exp(s - m_b[hh]).astype(BF16)
            return _dot(p, v_ref[hh, pl.ds(pl.multiple_of(j * QB_C, QB_C), QB_C), :])

        def pv_body(j, accs):
            return tuple(accs[hh] + pv(hh, scores(hh, j), j) for hh in heads)

        return lax.fori_loop(0, i, pv_body, tuple(pv(hh, diag[hh], i) for hh in heads))

    lane = lax.broadcasted_iota(I32, (QB_C, LANES), 1)

    def bound(hh):
        qf = jnp.where(lane < HEAD_DIM, q[hh].astype(F32), 0.0)
        return spread(jnp.sqrt(jnp.sum(qf * qf, axis=1, keepdims=True)) * kmax_ref[g * HEADS_PER_STEP_C + hh])

    accs = exp_pv_pass([bound(hh) for hh in heads])
    smallest = functools.reduce(jnp.minimum, [jnp.min(accs[hh][:, HEAD_DIM:HEAD_DIM + 1]) for hh in heads])

    def exact():
        def max_body(j, ms):
            return tuple(jnp.maximum(ms[hh], halves_max(scores(hh, j))) for hh in heads)

        ms = lax.fori_loop(0, i, max_body, tuple(halves_max(diag[hh]) for hh in heads))
        return exp_pv_pass([spread(jnp.max(ms[hh], axis=1, keepdims=True)) for hh in heads])

    accs = lax.cond(smallest >= SOFTMAX_SUM_FLOOR, lambda: accs, exact)
    o_ref[...] = jnp.concatenate(
        [accs[hh][:, :HEAD_DIM] / accs[hh][:, HEAD_DIM:] for hh in heads], axis=1).astype(BF16)


def _fox_prompt(qa, ka, va, kmax, s):
    nq = s // QB_C
    ng = H_C // HEADS_PER_STEP_C
    hp = HEADS_PER_STEP_C
    return pl.pallas_call(
        _fox_kernel,
        grid=(ng, nq),
        in_specs=[
            pl.BlockSpec(memory_space=pltpu.SMEM),
            pl.BlockSpec((hp, QB_C, LANES), lambda g, i: (g, i, 0)),
            pl.BlockSpec((hp, s, LANES), lambda g, i: (g, 0, 0)),
            pl.BlockSpec((hp, s, LANES), lambda g, i: (g, 0, 0)),
        ],
        out_specs=pl.BlockSpec((QB_C, LANES), lambda g, i: (i, g)),
        out_shape=jax.ShapeDtypeStruct((s, D_C), BF16),
        compiler_params=_cparams(2),
    )(kmax, qa, ka, va)


def _softmax_pv(s, v_all):
    m = jnp.max(s, axis=1, keepdims=True)
    p = jnp.exp(s - m)
    l = jnp.sum(p, axis=1, keepdims=True)
    return _dot(p.astype(BF16), v_all) / l


def _with_new_rows(cache, new, pad_rows):
    parts = [cache.astype(BF16), new]
    if pad_rows:
        parts.append(jnp.zeros((pad_rows, new.shape[1]), BF16))
    return jnp.concatenate(parts, axis=0)


def _sample_kernel(qa_ref, ka_ref, va_ref, kvb_ref, qi_ref, qb_ref, qc_ref, kc_ref, vc_ref, misc16_ref,
                   misc32_ref, ca_ref, cb_ref, cbi_ref, cc_ref, clf_ref, clft_ref, lfn_ref, lfnt_ref,
                   bias_a_ref, t5_ref, triu_ref, tril_ref,
                   oa_ref, ob_ref, oc_ref, *, t, past, na, k_top):
    la = na + LANES
    lk = past + LANES
    pad = LANES - t

    ca = ca_ref[...]
    ka_all = _with_new_rows(ca[:, :D_A], ka_ref[...], pad)
    va_all = _with_new_rows(ca[:, D_A:], va_ref[...], pad)
    for h in range(H_A):
        hs = slice(h * HEAD_DIM, (h + 1) * HEAD_DIM)
        s = _dot_t(qa_ref[:, hs], ka_all[:, hs]) * ATTN_SCALE + bias_a_ref[h]
        oa_ref[:, hs] = _softmax_pv(s, va_all[:, hs]).astype(BF16)

    col = lax.broadcasted_iota(I32, (t, lk), 1)
    ki_all = _with_new_rows(cbi_ref[...], misc16_ref[:, :D_IDX], pad)
    wi = misc32_ref[:, MISC_WI:MISC_WI + H_IDX] * (H_IDX ** -0.5 * D_IDX ** -0.5)
    acc = jnp.zeros((t, lk), F32)
    for h in range(H_IDX):
        isc = _dot_t(qi_ref[:, h * D_IDX:(h + 1) * D_IDX], ki_all)
        acc = acc + jnp.maximum(isc, 0.0) * wi[:, h:h + 1]
    keys = jnp.where(col < past + t, _sortable(acc), jnp.int32(NEG_KEY))

    def count(pred):
        return jnp.sum(jnp.where(pred, 1.0, 0.0), axis=1, keepdims=True)

    thr, _ = _kth_largest(lambda cand: count(keys >= cand), t, k_top)
    live = thr > jnp.int32(NEG_KEY)
    thr = jnp.maximum(thr, jnp.int32(NEG_KEY))
    need = float(k_top) - count(keys > thr)
    jstar = _tie_limit(lambda cand: count((keys == thr) & (col < cand)), need, t)
    sel = (keys > thr) | ((keys == thr) & (col < jnp.where(live, jstar, 0)))

    cb = cb_ref[...]
    kb_all = _with_new_rows(cb[:, :D_B_KV], kvb_ref[:, :D_B_KV], pad)
    vb_all = _with_new_rows(cb[:, D_B_KV:], kvb_ref[:, D_B_KV:], pad)
    for n in range(H_B_KV):
        ns = slice(n * HEAD_DIM, (n + 1) * HEAD_DIM)
        for g in range(GQA):
            h = n * GQA + g
            hs = slice(h * HEAD_DIM, (h + 1) * HEAD_DIM)
            s = _dot_t(qb_ref[:, hs], kb_all[:, ns]) * ATTN_SCALE + t5_ref[h]
            s = jnp.where(sel, s, NEG_INF)
            ob_ref[:, hs] = _softmax_pv(s, vb_all[:, ns]).astype(BF16)

    cc = cc_ref[...]
    kc_all = _with_new_rows(cc[:, :D_C], kc_ref[...], pad)
    vc_all = _with_new_rows(cc[:, D_C:], vc_ref[...], pad)
    lft = jnp.concatenate([clft_ref[...], lfnt_ref[...]], axis=1)
    t1, t2, t3 = _split3(lft)
    triu = triu_ref[...]
    cum_t = _dot(t1, triu) + _dot(t2, triu) + _dot(t3, triu)
    total = jnp.sum(clf_ref[...], axis=0, keepdims=True)
    n1, n2, n3 = _split3(lfn_ref[...])
    tril = tril_ref[...]
    cum_q = total + _dot(tril, n1) + _dot(tril, n2) + _dot(tril, n3)
    row = lax.broadcasted_iota(I32, (t, lk), 0)
    causal = col <= past + row
    for h in range(H_C):
        hs = slice(h * HEAD_DIM, (h + 1) * HEAD_DIM)
        s = _dot_t(qc_ref[:, hs], kc_all[:, hs]) * ATTN_SCALE + cum_q[:, h:h + 1] - cum_t[h:h + 1, :]
        s = jnp.where(causal, s, NEG_INF)
        oc_ref[:, hs] = _softmax_pv(s, vc_all[:, hs]).astype(BF16)


def _sample_mixers(p32, pb, s, ca, cb, cbi, cc, clf, clft, lfn, lfnt, bias_a, t5_tab, triu, tril, layer):
    nb, na = ca.shape[1], ca.shape[2]
    past = cb.shape[2]
    t = lfn.shape[1]
    k_top = min(TOPK_MAX, (past + t) // 4)
    r0 = s // t
    lk = past + LANES
    la = na + LANES
    row = lambda width, off: pl.BlockSpec((t, width), lambda b: (r0 + b, off // width))
    cache = lambda rows, width: pl.BlockSpec((None, None, rows, width), lambda b: (layer, b, 0, 0))
    full = lambda shape: pl.BlockSpec(shape, lambda b: (0,) * len(shape))
    in_specs = [
        row(D_A, OFF_QA), row(D_A, OFF_KA), row(D_A, OFF_VA), row(2 * D_B_KV, OFF_KB),
        row(H_IDX * D_IDX, OFF_QI), row(D_B, OFF_QB), row(D_C, OFF_QC), row(D_C, OFF_KC), row(D_C, OFF_VC),
        row(LANES, OFF_MISC), row(LANES, OFF_MISC),
        cache(na, 2 * D_A), cache(past, 2 * D_B_KV), cache(past, D_IDX), cache(past, 2 * D_C),
        cache(past, 16), cache(16, past),
        pl.BlockSpec((None, t, 16), lambda b: (b, 0, 0)),
        pl.BlockSpec((None, 16, LANES), lambda b: (b, 0, 0)),
        full((H_A, t, la)), full((H_B, t, lk)), full((lk, lk)), full((t, t)),
    ]
    out = lambda width: pl.BlockSpec((t, width), lambda b: (b, 0))
    return pl.pallas_call(
        functools.partial(_sample_kernel, t=t, past=past, na=na, k_top=k_top),
        grid=(nb,),
        in_specs=in_specs,
        out_specs=[out(D_A), out(D_B), out(D_C)],
        out_shape=[jax.ShapeDtypeStruct((nb * t, w), BF16) for w in (D_A, D_B, D_C)],
        compiler_params=_cparams(1),
    )(pb, pb, pb, pb, pb, pb, pb, pb, pb, pb, p32, ca, cb, cbi, cc, clf, clft, lfn, lfnt,
      bias_a, t5_tab, triu, tril)


def _outproj_kernel(x_ref, g_ref, ap, bp, cp, as_, bs, cs, w_ref, o_ref, *, n_prompt_blocks):
    i = pl.program_id(0)

    def run(a, b, c):
        y = (_dot(a[...], w_ref[:D_A, :]) + _dot(b[...], w_ref[D_A:D_A + D_B, :])
             + _dot(c[...], w_ref[D_A + D_B:, :]))
        o_ref[...] = x_ref[...] + g_ref[...] * y

    pl.when(i < n_prompt_blocks)(lambda: run(ap, bp, cp))
    pl.when(i >= n_prompt_blocks)(lambda: run(as_, bs, cs))


def _out_project(x, gate, mix_p, mix_s, w_out_b, layer, n_prompt_blocks):
    m = x.shape[0]
    tm = TOKEN_TILE
    last_p = n_prompt_blocks - 1
    p_map = lambda i: (jnp.minimum(i, last_p), 0)
    s_map = lambda i: (jnp.maximum(i - n_prompt_blocks, 0), 0)
    widths = (D_A, D_B, D_C)
    return pl.pallas_call(
        functools.partial(_outproj_kernel, n_prompt_blocks=n_prompt_blocks),
        grid=(m // tm,),
        in_specs=[
            pl.BlockSpec((tm, D_MODEL), lambda i: (i, 0)),
            pl.BlockSpec((tm, D_MODEL), lambda i: (_mod_block_index(i, n_prompt_blocks), 0)),
            *[pl.BlockSpec((tm, w), p_map) for w in widths],
            *[pl.BlockSpec((tm, w), s_map) for w in widths],
            pl.BlockSpec((None, D_MODEL, D_MODEL), lambda i: (layer, 0, 0)),
        ],
        out_specs=pl.BlockSpec((tm, D_MODEL), lambda i: (i, 0)),
        out_shape=jax.ShapeDtypeStruct((m, D_MODEL), F32),
        compiler_params=_cparams(1),
    )(x, gate, *mix_p, *mix_s, w_out_b)


def _lane_pick(vals, lane, idx):
    return jnp.sum(jnp.where(lane == idx, vals, 0.0), axis=1, keepdims=True)


def _first_argmax(vals, lane):
    m = jnp.max(vals, axis=1, keepdims=True)
    idx = jnp.min(jnp.where(vals == m, lane, float(LANES)), axis=1, keepdims=True)
    return m, idx


def _router_kernel(x_ref, gain_ref, sc_ref, sh_ref, wr_ref, br_ref, tri_ref, h_ref, meta_ref, cnt_ref, carry):
    @pl.when(pl.program_id(0) == 0)
    def _():
        carry[...] = jnp.zeros_like(carry)

    h = _norm_mod(x_ref[...], gain_ref[...], sc_ref[...], sh_ref[...])
    hb = h.astype(BF16)
    h_ref[...] = h
    scores = jax.nn.sigmoid(_dot(hb, wr_ref[...]))
    lane_i = lax.broadcasted_iota(I32, scores.shape, 1)
    lane = lane_i.astype(F32)
    sel = jnp.where(lane_i < N_EXPERTS, scores + br_ref[...], NEG_INF)
    group = (lane_i // (N_EXPERTS // N_GROUPS)).astype(F32)

    best = None
    for g in range(N_GROUPS):
        in_g = jnp.where(group == float(g), sel, NEG_INF)
        m1, i1 = _first_argmax(in_g, lane)
        m2 = jnp.max(jnp.where(lane == i1, NEG_INF, in_g), axis=1, keepdims=True)
        gs = m1 + m2
        if best is None:
            best, gbest = gs, jnp.zeros_like(i1)
        else:
            better = gs > best
            best = jnp.where(better, gs, best)
            gbest = jnp.where(better, float(g), gbest)

    in_best = jnp.where(group == gbest, sel, NEG_INF)
    _, e0 = _first_argmax(in_best, lane)
    _, e1 = _first_argmax(jnp.where(lane == e0, NEG_INF, in_best), lane)
    w0 = _lane_pick(scores, lane, e0)
    w1 = _lane_pick(scores, lane, e1)
    wsum = w0 + w1

    onehot = jnp.where((lane == e0) | (lane == e1), 1.0, 0.0)
    before = _dot(tri_ref[...], onehot.astype(BF16)) + carry[...]
    r0 = _lane_pick(before, lane, e0)
    r1 = _lane_pick(before, lane, e1)
    carry[...] = carry[...] + jnp.sum(onehot, axis=0, keepdims=True)
    cnt_ref[...] = carry[...]

    meta = jnp.zeros(scores.shape, F32)
    for k, v in enumerate((e0, e1, r0, r1, w0 / wsum, w1 / wsum)):
        meta = jnp.where(lane_i == k, v, meta)
    meta_ref[...] = meta


def _router(x, gain, sc, sh, w_router_p, b_router_p, tri, n_prompt_blocks):
    m = x.shape[0]
    tm = TOKEN_TILE
    mod_map = lambda i: (_mod_block_index(i, n_prompt_blocks), 0)
    return pl.pallas_call(
        _router_kernel,
        grid=(m // tm,),
        in_specs=[
            pl.BlockSpec((tm, D_MODEL), lambda i: (i, 0)),
            pl.BlockSpec((1, D_MODEL), lambda i: (0, 0)),
            pl.BlockSpec((tm, D_MODEL), mod_map),
            pl.BlockSpec((tm, D_MODEL), mod_map),
            pl.BlockSpec((D_MODEL, LANES), lambda i: (0, 0)),
            pl.BlockSpec((1, LANES), lambda i: (0, 0)),
            pl.BlockSpec((tm, tm), lambda i: (0, 0)),
        ],
        out_specs=[
            pl.BlockSpec((tm, D_MODEL), lambda i: (i, 0)),
            pl.BlockSpec((tm, LANES), lambda i: (i, 0)),
            pl.BlockSpec((1, LANES), lambda i: (0, 0)),
        ],
        out_shape=[
            jax.ShapeDtypeStruct((m, D_MODEL), F32),
            jax.ShapeDtypeStruct((m, LANES), F32),
            jax.ShapeDtypeStruct((1, LANES), F32),
        ],
        scratch_shapes=[pltpu.VMEM((1, LANES), F32)],
        compiler_params=_cparams(1),
    )(x, gain, sc, sh, w_router_p, b_router_p, tri)


EXPERT_ROWS = 384
EXPERT_TF = 512


def _row_copy(src_hbm, src_row, dst_buf, dst_row, sem):
    return pltpu.make_async_copy(src_hbm.at[pl.ds(src_row, 1), :], dst_buf.at[pl.ds(dst_row, 1), :], sem)


def _expert_kernel(be_ref, nu_ref, st_ref, h_hbm, wg_ref, wu_ref, wd_ref, o_ref, xrows, x16, sems):
    b = pl.program_id(0)
    f = pl.program_id(1)
    n_used = nu_ref[0]
    rows_per_step = EXPERT_ROWS // (D_EXPERT // EXPERT_TF)

    def start_rows(blk, first, count):
        for r in range(count):
            tok = st_ref[blk * EXPERT_ROWS + first + r]
            _row_copy(h_hbm, tok, xrows.at[blk % 2], first + r, sems.at[blk % 2]).start()

    @pl.when((b == 0) & (f == 0))
    def _():
        start_rows(0, 0, EXPERT_ROWS)

    @pl.when((f == 0) & (b < n_used))
    def _():
        for r in range(EXPERT_ROWS):
            _row_copy(h_hbm, 0, xrows.at[b % 2], r, sems.at[b % 2]).wait()
        x16[...] = xrows[b % 2].astype(BF16)

    @pl.when(b + 1 < n_used)
    def _():
        start_rows(b + 1, f * rows_per_step, rows_per_step)

    @pl.when((f == 0) & (b >= n_used))
    def _():
        o_ref[...] = jnp.zeros(o_ref.shape, F32)

    @pl.when(b < n_used)
    def _():
        x = x16[...]
        a = _dot(x, wg_ref[...].astype(BF16))
        u = _dot(x, wu_ref[...].astype(BF16))
        act = (a * jax.nn.sigmoid(a) * u).astype(BF16)
        y = _dot(act, wd_ref[...].astype(BF16))

        @pl.when(f == 0)
        def _():
            o_ref[...] = y

        @pl.when(f > 0)
        def _():
            o_ref[...] += y


def _experts(h2, slot_tok, block_e, n_used, w_g, w_u, w_d, layer):
    nslots = slot_tok.shape[0]
    nb = nslots // EXPERT_ROWS
    nf = D_EXPERT // EXPERT_TF

    def blk(b, nu):
        return jnp.minimum(b, nu[0] - 1)

    def fidx(b, f, nu):
        return jnp.where(b < nu[0], f, nf - 1)

    grid_spec = pltpu.PrefetchScalarGridSpec(
        num_scalar_prefetch=3,
        grid=(nb, nf),
        in_specs=[
            pl.BlockSpec(memory_space=pl.ANY),
            pl.BlockSpec((None, None, D_MODEL, EXPERT_TF),
                         lambda b, f, be, nu, st: (layer, be[blk(b, nu)], 0, fidx(b, f, nu))),
            pl.BlockSpec((None, None, D_MODEL, EXPERT_TF),
                         lambda b, f, be, nu, st: (layer, be[blk(b, nu)], 0, fidx(b, f, nu))),
            pl.BlockSpec((None, None, EXPERT_TF, D_MODEL),
                         lambda b, f, be, nu, st: (layer, be[blk(b, nu)], fidx(b, f, nu), 0)),
        ],
        out_specs=pl.BlockSpec((EXPERT_ROWS, D_MODEL), lambda b, f, be, nu, st: (b, 0)),
        scratch_shapes=[
            pltpu.VMEM((2, EXPERT_ROWS, D_MODEL), F32),
            pltpu.VMEM((EXPERT_ROWS, D_MODEL), BF16),
            pltpu.SemaphoreType.DMA((2,)),
        ],
    )
    return pl.pallas_call(
        _expert_kernel,
        grid_spec=grid_spec,
        out_shape=jax.ShapeDtypeStruct((nslots, D_MODEL), F32),
        compiler_params=_cparams(2),
    )(block_e, n_used, slot_tok, h2, w_g, w_u, w_d)


COMBINE_TILE = 256


def _combine_kernel(d0_ref, d1_ref, x_ref, g_ref, meta_ref, y_hbm, o_ref, ybuf, sems):
    i = pl.program_id(0)
    n = pl.num_programs(0)

    def start_rows(blk):
        slot = blk % 2

        def body(r, _):
            t = blk * COMBINE_TILE + r
            _row_copy(y_hbm, d0_ref[t], ybuf.at[slot, 0], r, sems.at[slot]).start()
            _row_copy(y_hbm, d1_ref[t], ybuf.at[slot, 1], r, sems.at[slot]).start()
            return 0

        lax.fori_loop(0, COMBINE_TILE, body, 0, unroll=8)

    @pl.when(i == 0)
    def _():
        start_rows(i)

    slot = i % 2
    for k in range(2):
        for r in range(COMBINE_TILE):
            _row_copy(y_hbm, 0, ybuf.at[slot, k], r, sems.at[slot]).wait()

    @pl.when(i + 1 < n)
    def _():
        start_rows(i + 1)

    moe = ybuf[slot, 0] * meta_ref[:, 4:5] + ybuf[slot, 1] * meta_ref[:, 5:6]
    o_ref[...] = x_ref[...] + g_ref[...] * moe


def _combine(x, gate, meta, yb, dest0, dest1, n_prompt_blocks):
    m = x.shape[0]
    tm = COMBINE_TILE
    ratio = TOKEN_TILE // tm
    mod_map = lambda i, d0, d1: (jnp.where(i < n_prompt_blocks * ratio, 0, i - (n_prompt_blocks - 1) * ratio), 0)
    row = lambda width: pl.BlockSpec((tm, width), lambda i, d0, d1: (i, 0))
    grid_spec = pltpu.PrefetchScalarGridSpec(
        num_scalar_prefetch=2,
        grid=(m // tm,),
        in_specs=[row(D_MODEL), pl.BlockSpec((tm, D_MODEL), mod_map), row(LANES), pl.BlockSpec(memory_space=pl.ANY)],
        out_specs=row(D_MODEL),
        scratch_shapes=[pltpu.VMEM((2, 2, tm, D_MODEL), F32), pltpu.SemaphoreType.DMA((2,))],
    )
    return pl.pallas_call(
        _combine_kernel,
        grid_spec=grid_spec,
        out_shape=jax.ShapeDtypeStruct((m, D_MODEL), F32),
        compiler_params=_cparams(1),
    )(dest0, dest1, x, gate, meta, yb)


def _t5_bucket(rel):
    nb = T5_BUCKETS // 2
    max_exact = nb // 2
    ret = jnp.where(rel < 0, nb, 0)
    n = jnp.abs(rel)
    nf = jnp.maximum(n, 1).astype(F32)
    large = max_exact + (jnp.log(nf / max_exact) / math.log(T5_MAX_DIST / max_exact) * (nb - max_exact)).astype(I32)
    large = jnp.minimum(large, nb - 1)
    return ret + jnp.where(n < max_exact, n, large)


def _toeplitz(by_rel, q0, k0, nq, nk):
    d = jnp.arange(-(nk - 1), nq)
    rev = by_rel(q0 - k0 + d)[:, ::-1]
    h, length = rev.shape
    flat = jnp.tile(jnp.pad(rev, ((0, 0), (0, 1))), (1, nq))[:, :nq * length]
    return flat.reshape(h, nq, length)[:, :, nq - 1:nq - 1 + nk]


def _band_bias(table, q0, k0, nq, nk, n_valid):
    by_rel = lambda rel: table.astype(F32)[:, jnp.clip(rel, -A_REL_FUTURE, A_REL_PAST) + A_REL_FUTURE]
    qc = (q0 + jnp.arange(nq))[:, None] // CHUNK
    col = jnp.arange(nk)[None, :]
    kc = (k0 + col) // CHUNK
    ok = (col < n_valid) & (kc <= qc) & (kc >= qc - A_LEFT_CHUNKS)
    return jnp.where(ok[None], _toeplitz(by_rel, q0, k0, nq, nk), NEG_INF)


def _t5_bias(t5, q0, k0, nq, nk):
    return _toeplitz(lambda rel: t5.astype(F32)[_t5_bucket(rel)].T, q0, k0, nq, nk)


def _relayout_w_in(w_in_l):
    qa, ka, va, qb, kb, vb, qi, ki, wi, qc, kc, vc, fc = jnp.split(w_in_l, PROJ_SPLITS, axis=1)
    pad = jnp.zeros((D_MODEL, P_COLS - OFF_MISC - D_IDX - H_IDX - H_C), w_in_l.dtype)
    w = jnp.concatenate([qa, ka, va, kb, vb, qi, qb, qc, kc, vc, ki, wi, fc, pad], axis=1).astype(BF16)
    return jnp.swapaxes(w.reshape(D_MODEL, P_COLS // PROJ_TN, PROJ_TN), 0, 1)


def _column_params(qk_gain_l, b_forget_l):
    ones = lambda n: jnp.ones((n,), F32)
    g = qk_gain_l.astype(F32)
    gq = jnp.concatenate([
        jnp.tile(g[0], H_A), jnp.tile(g[1], H_A), ones(D_A),
        jnp.tile(g[3], H_B_KV), ones(D_B_KV), ones(H_IDX * D_IDX),
        jnp.tile(g[2], H_B), jnp.tile(g[4], H_C), jnp.tile(g[5], H_C), ones(D_C), ones(P_COLS - OFF_MISC)])
    bq = jnp.zeros((P_COLS,), F32).at[OFF_MISC + MISC_FC:OFF_MISC + MISC_FC + H_C].set(b_forget_l.astype(F32))
    return gq[None, :], bq[None, :]


def _tri(n, *, strict=False, upper=False):
    r = jnp.arange(n)[:, None]
    c = jnp.arange(n)[None, :]
    m = (r < c if strict else r <= c) if upper else (c < r if strict else c <= r)
    return m.astype(BF16)


def kernel(x_prompt, x_sample, c_prompt, c_sample, cache_a_kv, cache_b_kv, cache_b_kidx, cache_c_kv, cache_c_logf,
           w_ada, b_ada, norm_gain, w_in, b_forget, qk_gain, rel_bias_a, t5_bias, w_out, w_router, b_router,
           w_e_gate, w_e_up, w_e_down):
    depth = w_in.shape[0]
    bp, s, d = x_prompt.shape
    nb, t, _ = x_sample.shape
    ns = nb * t
    tm = TOKEN_TILE
    assert bp == 1 and d == D_MODEL and s % tm == 0 and ns % tm == 0 and s % QB_C == 0
    n_pb = s // tm
    m = s + ns
    na = cache_a_kv.shape[2]
    past = cache_b_kv.shape[2]
    keep = min(A_LEFT_CHUNKS * CHUNK, s)

    x = jnp.concatenate([x_prompt.reshape(s, d), x_sample.reshape(ns, d)], axis=0)
    c_all = jnp.concatenate([c_prompt, c_sample, jnp.zeros((-(bp + nb) % 8, d), F32)], axis=0)

    t5_far = t5_bias.astype(F32)[T5_BUCKETS // 2 - 1]
    t5_tables = jnp.concatenate(
        [_t5_bias(t5_bias, off, 0, QB_B, KB_B) - t5_far[:, None, None] for off in (QB_B, 2 * QB_B, 0)]
        + [jnp.zeros((H_B, QB_B, KB_B), F32)], axis=0)
    t5_s = _t5_bias(t5_bias, past, 0, t, past + LANES)
    bd = ((jnp.arange(PROJ_TN)[:, None] // HEAD_DIM == jnp.arange(PROJ_TN)[None, :] // HEAD_DIM)
          .astype(F32) / HEAD_DIM).astype(BF16)
    tri_cum = _tri(CUM_TILE)
    tri_rank = _tri(tm, strict=True)
    triu_s = _tri(past + LANES, upper=True)
    tril_s = _tri(t)
    w_out_b = w_out.astype(BF16)
    w_router_p = jnp.pad(w_router, ((0, 0), (0, LANES - N_EXPERTS))).astype(BF16)
    b_router_p = jnp.pad(b_router.astype(F32), (0, LANES - N_EXPERTS))[None, :]

    ca = cache_a_kv.reshape(depth, nb, na, 2 * D_A)
    cb = cache_b_kv.reshape(depth, nb, past, 2 * D_B_KV)
    cc = cache_c_kv.reshape(depth, nb, past, 2 * D_C)
    clf = jnp.pad(cache_c_logf.astype(F32), ((0, 0), (0, 0), (0, 0), (0, 16 - H_C)))
    clft = jnp.swapaxes(clf, 2, 3)

    n_asg = 2 * m
    n_eb = -(-n_asg // EXPERT_ROWS) + N_EXPERTS
    tok = jnp.arange(m, dtype=I32)

    states_p, states_s = [], []
    for l in range(depth):
        mod = _ada(c_all, w_ada, b_ada, l)
        mods = []
        for part in jnp.split(mod, 6, axis=1):
            mods.append(jnp.concatenate([jnp.broadcast_to(part[:1], (tm, d)), jnp.repeat(part[bp:bp + nb], t, axis=0)], 0))
        sh1, sc1, g1, sh2, sc2, g2 = mods

        gq, bq = _column_params(qk_gain[l], b_forget[l])
        p32, pb = _project(x, norm_gain[l, 0][None, :], sc1, sh1, _relayout_w_in(w_in[l]), gq, bq, bd, n_pb)

        bias_a_p = _band_bias(rel_bias_a[l], (A_KEY_BLOCKS - 1) * QB_A, 0, QB_A, A_KEY_BLOCKS * QB_A,
                              A_KEY_BLOCKS * QB_A)
        oa_p = _band_prompt(pb, s, bias_a_p)
        ob_p = _dsa_prompt(p32, pb, s, t5_tables)
        cum = _cumsum_rows(p32, s, tri_cum)[:, MISC_FC:MISC_FC + 16]
        oc_p = _fox_prompt(*_fox_prepare(pb, s, cum), _key_norm_max(pb, s, OFF_KC, H_C), s)

        lfn = p32[s:, OFF_MISC + MISC_FC:OFF_MISC + MISC_FC + 16].reshape(nb, t, 16)
        lfnt = jnp.pad(jnp.swapaxes(lfn, 1, 2), ((0, 0), (0, 0), (0, LANES - t)))
        bias_a_s = _band_bias(rel_bias_a[l], past, past - na, t, na + LANES, na + t)
        oa_s, ob_s, oc_s = _sample_mixers(p32, pb, s, ca, cb, cache_b_kidx, cc, clf, clft, lfn, lfnt,
                                          bias_a_s, t5_s, triu_s, tril_s, l)

        x = _out_project(x, g1, (oa_p, ob_p, oc_p), (oa_s, ob_s, oc_s), w_out_b, l, n_pb)

        h2, meta, counts = _router(x, norm_gain[l, 1][None, :], sc2, sh2, w_router_p, b_router_p, tri_rank, n_pb)
        e0 = meta[:, 0].astype(I32)
        e1 = meta[:, 1].astype(I32)
        counts = counts[0, :N_EXPERTS].astype(I32)
        padded = (counts + EXPERT_ROWS - 1) // EXPERT_ROWS * EXPERT_ROWS
        pend = jnp.cumsum(padded)
        pstart = pend - padded
        dest0 = pstart[e0] + meta[:, 2].astype(I32)
        dest1 = pstart[e1] + meta[:, 3].astype(I32)
        slot_tok = jnp.zeros((n_eb * EXPERT_ROWS,), I32).at[dest0].set(tok).at[dest1].set(tok)
        block_e = jnp.minimum(jnp.searchsorted(pend, jnp.arange(n_eb, dtype=I32) * EXPERT_ROWS, side='right'),
                              N_EXPERTS - 1).astype(I32)
        n_used = (pend[-1:] // EXPERT_ROWS).astype(I32)
        yb = _experts(h2, slot_tok, block_e, n_used, w_e_gate, w_e_up, w_e_down, l)
        x = _combine(x, g2, meta, yb, dest0, dest1, n_pb)

        def states(rows, nbatch, a_rows):
            r = p32[rows]
            n = r.shape[0] // nbatch
            kv = lambda off, heads: r[:, off:off + 2 * heads * HEAD_DIM].reshape(nbatch, n, 2, heads, HEAD_DIM)
            return (kv(OFF_KA, H_A)[:, n - a_rows:], kv(OFF_KB, H_B_KV),
                    r[:, OFF_MISC:OFF_MISC + D_IDX].reshape(nbatch, n, D_IDX), kv(OFF_KC, H_C),
                    r[:, OFF_MISC + MISC_FC:OFF_MISC + MISC_FC + H_C].reshape(nbatch, n, H_C))

        states_p.append(states(slice(0, s), bp, keep))
        states_s.append(states(slice(s, m), nb, t))

    stk = lambda sts, i: jnp.stack([st[i] for st in sts], axis=0)
    return (x[:s].reshape(bp, s, d), x[s:].reshape(nb, t, d),
            *[stk(states_p, i) for i in range(5)], *[stk(states_s, i) for i in range(5)])
```

```python
import functools
import math

import jax
import jax.numpy as jnp
from jax import lax
from jax.experimental import pallas as pl
from jax.experimental.pallas import tpu as pltpu

F32 = jnp.float32
BF16 = jnp.bfloat16
I32 = jnp.int32

D_MODEL = 2048
HEAD_DIM = 64
CHUNK = 64
EPS = 1e-6
ATTN_SCALE = HEAD_DIM ** -0.5
H_A = 8
A_LEFT_CHUNKS = 8
A_REL_PAST = 128
A_REL_FUTURE = CHUNK - 1
H_B = 12
H_B_KV = 4
GQA = H_B // H_B_KV
H_IDX = 16
D_IDX = 64
TOPK_MAX = 256
H_C = 12
T5_BUCKETS = 32
T5_MAX_DIST = 128
N_EXPERTS = 32
N_GROUPS = 4
D_EXPERT = 1024

D_A = H_A * HEAD_DIM
D_B = H_B * HEAD_DIM
D_B_KV = H_B_KV * HEAD_DIM
D_C = H_C * HEAD_DIM
PROJ_SIZES = (D_A, D_A, D_A, D_B, D_B_KV, D_B_KV, H_IDX * D_IDX, D_IDX, H_IDX, D_C, D_C, D_C, H_C)
PROJ_SPLITS = tuple(sum(PROJ_SIZES[:i + 1]) for i in range(len(PROJ_SIZES) - 1))

OFF_QA, OFF_KA, OFF_VA = 0, 512, 1024
OFF_KB, OFF_VB = 1536, 1792
OFF_QI = 2048
OFF_QB = 3072
OFF_QC, OFF_KC, OFF_VC = 3840, 4608, 5376
OFF_MISC = 6144
MISC_WI = 64
MISC_FC = 80
P_COLS = 6400
PROJ_TN = 256
NORM_COL_BLOCKS = (0, 1, 2, 3, 6, 12, 13, 14, 15, 16, 17, 18, 19, 20)
MISC_COL_BLOCK = OFF_MISC // PROJ_TN

TOKEN_TILE = 512
LANES = 128
VMEM_LIMIT = 56 * 1024 * 1024

NEG_INF = float("-inf")
SOFTMAX_SUM_FLOOR = 2.0 ** -40
BOUND_SLACK = 1.001
NEG_KEY = (0xFF800000 ^ 0x7FFFFFFF) - (1 << 32)
IDX_BIG = 1 << 30


def _cparams(n_axes):
    return pltpu.CompilerParams(dimension_semantics=("arbitrary",) * n_axes, vmem_limit_bytes=VMEM_LIMIT)


def _split3(x):
    x1 = x.astype(BF16)
    r1 = x - x1.astype(F32)
    x2 = r1.astype(BF16)
    r2 = r1 - x2.astype(F32)
    return x1, x2, r2.astype(BF16)


def _dot(a, b):
    return jnp.dot(a, b, preferred_element_type=F32)


def _dot_t(a, b):
    return lax.dot_general(a, b, (((1,), (1,)), ((), ())), preferred_element_type=F32)


def _sortable(x):
    b = lax.bitcast_convert_type(x, I32)
    return jnp.where(b < 0, b ^ jnp.int32(0x7FFFFFFF), b)


def _ada_kernel(c_ref, w_ref, b_ref, o_ref):
    c = c_ref[...]
    a = (c * jax.nn.sigmoid(c)).astype(BF16)
    o_ref[...] = _dot(a, w_ref[...].astype(BF16)) + b_ref[...]


def _ada(c_all, w_ada, b_ada, layer):
    rows = c_all.shape[0]
    n_out = w_ada.shape[2]
    tn = 1024
    return pl.pallas_call(
        _ada_kernel,
        grid=(n_out // tn,),
        in_specs=[
            pl.BlockSpec((rows, D_MODEL), lambda n: (0, 0)),
            pl.BlockSpec((None, D_MODEL, tn), lambda n: (layer, 0, n)),
            pl.BlockSpec((None, 1, tn), lambda n: (layer, 0, n)),
        ],
        out_specs=pl.BlockSpec((rows, tn), lambda n: (0, n)),
        out_shape=jax.ShapeDtypeStruct((rows, n_out), F32),
        compiler_params=_cparams(1),
    )(c_all, w_ada, b_ada.reshape(b_ada.shape[0], 1, n_out))


def _mod_block_index(i, n_prompt_blocks):
    return jnp.where(i < n_prompt_blocks, 0, i - n_prompt_blocks + 1)


def _norm_mod(x, gain, sc, sh):
    ms = jnp.mean(x * x, axis=-1, keepdims=True)
    return (x * lax.rsqrt(ms + EPS) * gain) * (1.0 + sc) + sh


def _proj_kernel(x_ref, gain_ref, sc_ref, sh_ref, w_ref, gq_ref, bq_ref, bd_ref, o32_ref, o16_ref, h_scr):
    n = pl.program_id(1)

    @pl.when(n == 0)
    def _():
        h_scr[...] = _norm_mod(x_ref[...], gain_ref[...], sc_ref[...], sh_ref[...]).astype(BF16)

    y = _dot(h_scr[...], w_ref[...])

    is_norm = functools.reduce(jnp.logical_or, [n == b for b in NORM_COL_BLOCKS])
    is_misc = n == MISC_COL_BLOCK

    def emit(v):
        o32_ref[...] = v
        o16_ref[...] = v.astype(BF16)

    @pl.when(is_norm)
    def _():
        s1, s2, s3 = _split3(y * y)
        bd = bd_ref[...]
        ms = _dot(s1, bd) + _dot(s2, bd) + _dot(s3, bd)
        emit(y * lax.rsqrt(ms + EPS) * gq_ref[...])

    @pl.when(is_misc)
    def _():
        lane = lax.broadcasted_iota(I32, y.shape, 1)
        z = y + bq_ref[...]
        logsig = jnp.minimum(z, 0.0) - jnp.log(1.0 + jnp.exp(-jnp.abs(z)))
        emit(jnp.where((lane >= MISC_FC) & (lane < MISC_FC + H_C), logsig, y))

    @pl.when(jnp.logical_not(jnp.logical_or(is_norm, is_misc)))
    def _():
        emit(y)


def _project(x, gain, sc, sh, w_r, gq, bq, bd, n_prompt_blocks):
    m = x.shape[0]
    tm = TOKEN_TILE
    mod_map = lambda i, n: (_mod_block_index(i, n_prompt_blocks), 0)
    return pl.pallas_call(
        _proj_kernel,
        grid=(m // tm, P_COLS // PROJ_TN),
        in_specs=[
            pl.BlockSpec((tm, D_MODEL), lambda i, n: (i, 0)),
            pl.BlockSpec((1, D_MODEL), lambda i, n: (0, 0)),
            pl.BlockSpec((tm, D_MODEL), mod_map),
            pl.BlockSpec((tm, D_MODEL), mod_map),
            pl.BlockSpec((None, D_MODEL, PROJ_TN), lambda i, n: (n, 0, 0)),
            pl.BlockSpec((1, PROJ_TN), lambda i, n: (0, n)),
            pl.BlockSpec((1, PROJ_TN), lambda i, n: (0, n)),
            pl.BlockSpec((PROJ_TN, PROJ_TN), lambda i, n: (0, 0)),
        ],
        out_specs=[
            pl.BlockSpec((tm, PROJ_TN), lambda i, n: (i, n)),
            pl.BlockSpec((tm, PROJ_TN), lambda i, n: (i, n)),
        ],
        out_shape=[jax.ShapeDtypeStruct((m, P_COLS), F32), jax.ShapeDtypeStruct((m, P_COLS), BF16)],
        scratch_shapes=[pltpu.VMEM((tm, D_MODEL), BF16)],
        compiler_params=_cparams(2),
    )(x, gain, sc, sh, w_r, gq, bq, bd)


CUM_TILE = 256


def _cumsum_kernel(x_ref, tri_ref, o_ref, carry):
    @pl.when(pl.program_id(0) == 0)
    def _():
        carry[...] = jnp.zeros_like(carry)

    x1, x2, x3 = _split3(x_ref[...])
    tri = tri_ref[...]
    c = _dot(tri, x1) + _dot(tri, x2) + _dot(tri, x3) + carry[...]
    o_ref[...] = c
    carry[...] = c[CUM_TILE - 1:CUM_TILE, :]


def _cumsum_rows(p32, s, tri):
    return pl.pallas_call(
        _cumsum_kernel,
        grid=(s // CUM_TILE,),
        in_specs=[
            pl.BlockSpec((CUM_TILE, LANES), lambda i: (i, OFF_MISC // LANES)),
            pl.BlockSpec((CUM_TILE, CUM_TILE), lambda i: (0, 0)),
        ],
        out_specs=pl.BlockSpec((CUM_TILE, LANES), lambda i: (i, 0)),
        out_shape=jax.ShapeDtypeStruct((s, LANES), F32),
        scratch_shapes=[pltpu.VMEM((1, LANES), F32)],
        compiler_params=_cparams(1),
    )(p32, tri)


QB_A = 128
A_KEY_BLOCKS = 5


def _band_kernel(q_ref, *refs):
    k_refs = refs[:A_KEY_BLOCKS]
    v_refs = refs[A_KEY_BLOCKS:2 * A_KEY_BLOCKS]
    bias_ref, o_ref = refs[2 * A_KEY_BLOCKS:]
    i = pl.program_id(0)
    for h in range(H_A):
        hs = slice(h * HEAD_DIM, (h + 1) * HEAD_DIM)
        q = q_ref[:, hs]
        parts = []
        for j in range(A_KEY_BLOCKS):
            s = _dot_t(q, k_refs[j][:, hs]) * ATTN_SCALE + bias_ref[h, :, j * QB_A:(j + 1) * QB_A]
            in_range = i - (A_KEY_BLOCKS - 1) + j >= 0
            parts.append(jnp.where(in_range, s, NEG_INF))
        m = functools.reduce(jnp.maximum, [jnp.max(s, axis=1, keepdims=True) for s in parts])
        l = jnp.zeros((QB_A, 1), F32)
        o = jnp.zeros((QB_A, HEAD_DIM), F32)
        for j in range(A_KEY_BLOCKS):
            p = jnp.exp(parts[j] - m)
            l = l + jnp.sum(p, axis=1, keepdims=True)
            o = o + _dot(p.astype(BF16), v_refs[j][:, hs])
        o_ref[:, hs] = (o / l).astype(BF16)


def _band_prompt(pb, s, bias_a):
    nq = s // QB_A
    back = A_KEY_BLOCKS - 1

    def kv_spec(j, col):
        return pl.BlockSpec((QB_A, D_A), lambda i: (jnp.maximum(i - back + j, 0), col))

    in_specs = [pl.BlockSpec((QB_A, D_A), lambda i: (i, OFF_QA // D_A))]
    in_specs += [kv_spec(j, OFF_KA // D_A) for j in range(A_KEY_BLOCKS)]
    in_specs += [kv_spec(j, OFF_VA // D_A) for j in range(A_KEY_BLOCKS)]
    in_specs += [pl.BlockSpec((H_A, QB_A, A_KEY_BLOCKS * QB_A), lambda i: (0, 0, 0))]
    return pl.pallas_call(
        _band_kernel,
        grid=(nq,),
        in_specs=in_specs,
        out_specs=pl.BlockSpec((QB_A, D_A), lambda i: (i, 0)),
        out_shape=jax.ShapeDtypeStruct((s, D_A), BF16),
        compiler_params=_cparams(1),
    )(pb, *([pb] * (2 * A_KEY_BLOCKS)), bias_a)


def _kth_largest(count_ge, rows, k):
    def body(carry):
        it, lo, cnt_lo = carry
        cand = lo + lax.shift_left(jnp.int32(1), jnp.int32(31) - it)
        c = count_ge(cand)
        keep = c >= float(k)
        return it + 1, jnp.where(keep, cand, lo), jnp.where(keep, c, cnt_lo)

    def unsettled(carry):
        it, _, cnt_lo = carry
        return (it < 32) & (jnp.max(jnp.abs(cnt_lo - float(k))) > 0.0)

    lo0 = jnp.full((rows, 1), -(1 << 31), I32)
    _, lo, cnt = lax.while_loop(unsettled, body, (jnp.int32(0), lo0, count_ge(lo0)))
    return lo, cnt


def _tie_limit(count_eq_below, need, rows):
    def body(it, j):
        cand = j + lax.shift_left(jnp.int32(1), jnp.int32(14) - it)
        return jnp.where(count_eq_below(cand) <= need, cand, j)

    return lax.fori_loop(0, 15, body, jnp.zeros((rows, 1), I32))


QB_B = 128
KB_B = 256


def _dsa_kernel(bnd_ref, qb_ref, qi_ref, misc_ref, kidx_ref, kb_ref, vb_ref, tb_ref, o_ref,
                key_scr, w_scr, thr_scr, m_scr, acc_scr, *, k_top):
    i = pl.program_id(0)
    n_kb = (i * QB_B) // KB_B + 1
    row = lax.broadcasted_iota(I32, (QB_B, KB_B), 0)
    col = lax.broadcasted_iota(I32, (QB_B, KB_B), 1)

    wi = misc_ref[:, MISC_WI:MISC_WI + H_IDX] * (H_IDX ** -0.5 * D_IDX ** -0.5)
    for h in range(H_IDX):
        w_scr[h] = jnp.broadcast_to(wi[:, h:h + 1], (QB_B, KB_B))

    def score_block(j, _):
        kblk = kidx_ref[pl.ds(pl.multiple_of(j * KB_B, KB_B), KB_B), :][:, :D_IDX]
        acc = jnp.zeros((QB_B, KB_B), F32)
        for h in range(H_IDX):
            isc = _dot_t(qi_ref[:, h * D_IDX:(h + 1) * D_IDX], kblk)
            acc = acc + jnp.maximum(isc, 0.0) * w_scr[h]
        admissible = (j * KB_B + col) // CHUNK <= (i * QB_B + row) // CHUNK
        key_scr[:, pl.ds(pl.multiple_of(j * KB_B, KB_B), KB_B)] = jnp.where(
            admissible, _sortable(acc), jnp.int32(NEG_KEY))
        return 0

    lax.fori_loop(0, n_kb, score_block, 0)

    def count_where(pred):
        def body(j, c):
            blk = key_scr[:, pl.ds(pl.multiple_of(j * KB_B, KB_B), KB_B)]
            return c + jnp.where(pred(blk, j * KB_B + col), 1.0, 0.0)
        c = lax.fori_loop(0, n_kb, body, jnp.zeros((QB_B, KB_B), F32))
        return jnp.sum(c, axis=1, keepdims=True)

    thr, cnt_ge = _kth_largest(lambda cand: count_where(lambda blk, idx: blk >= cand), QB_B, k_top)
    live = thr > jnp.int32(NEG_KEY)
    thr = jnp.maximum(thr, jnp.int32(NEG_KEY))
    thr_scr[0] = jnp.broadcast_to(thr, (QB_B, KB_B))
    thr_scr[1] = jnp.broadcast_to(jnp.where(live, jnp.int32(IDX_BIG), 0), (QB_B, KB_B))
    has_ties = jnp.max(jnp.where(live & (cnt_ge > float(k_top)), 1.0, 0.0)) > 0.0

    @pl.when(has_ties)
    def _():
        cnt_gt = count_where(lambda blk, idx: blk > thr)
        need = float(k_top) - cnt_gt
        jstar = _tie_limit(lambda cand: count_where(lambda blk, idx: (blk == thr) & (idx < cand)), need, QB_B)
        thr_scr[1] = jnp.broadcast_to(jnp.where(live, jstar, 0), (QB_B, KB_B))

    def mask_block(j, _):
        start = pl.multiple_of(j * KB_B, KB_B)
        keys = key_scr[:, pl.ds(start, KB_B)]
        thr_b = thr_scr[0]
        sel = (keys > thr_b) | ((keys == thr_b) & (j * KB_B + col < thr_scr[1]))
        key_scr[:, pl.ds(start, KB_B)] = lax.bitcast_convert_type(jnp.where(sel, 0.0, NEG_INF), I32)
        return 0

    lax.fori_loop(0, n_kb, mask_block, 0)

    odd = i % 2
    table_a = jnp.where(odd == 1, 3, 1)
    table_b = jnp.where(odd == 1, 0, 2)
    n_far = jnp.maximum(n_kb - 2, 0)
    ones = jnp.ones((KB_B, HEAD_DIM), BF16)
    groups = range(H_B_KV)
    heads = [[n * GQA + g for g in range(GQA)] for n in groups]
    q3 = [jnp.concatenate([qb_ref[:, h * HEAD_DIM:(h + 1) * HEAD_DIM] for h in heads[n]], axis=0) * ATTN_SCALE
          for n in groups]

    def scores(n, j, table):
        start = pl.multiple_of(j * KB_B, KB_B)
        negm = lax.bitcast_convert_type(key_scr[:, pl.ds(start, KB_B)], F32)
        k_n = kb_ref[pl.ds(start, KB_B), n * HEAD_DIM:(n + 1) * HEAD_DIM]
        s = _dot_t(q3[n], k_n).reshape(GQA, QB_B, KB_B) + negm[None]
        if table is not None:
            s = s + jnp.stack([tb_ref[table * H_B + h] for h in heads[n]])
        return s

    def over_blocks(block_fn):
        lax.fori_loop(0, n_far, lambda j, c: (block_fn(j, None), c)[1], 0)
        pl.when(n_kb >= 2)(lambda: block_fn(n_kb - 2, table_a))
        block_fn(n_kb - 1, table_b)

    def exp_pv_pass(m_b):
        def pv_block(j, table):
            start = pl.multiple_of(j * KB_B, KB_B)
            for n in groups:
                p = jnp.exp(scores(n, j, table) - m_b[n]).astype(BF16).reshape(GQA * QB_B, KB_B)
                v_n = vb_ref[pl.ds(start, KB_B), n * HEAD_DIM:(n + 1) * HEAD_DIM]
                acc_scr[n] += _dot(p, jnp.concatenate([v_n, ones], axis=1))

        acc_scr[...] = jnp.zeros(acc_scr.shape, F32)
        over_blocks(pv_block)

    def spread(m):
        return jnp.broadcast_to(m, (GQA, QB_B, KB_B))

    def bound(n):
        qn = jnp.sqrt(jnp.sum(jnp.square(q3[n].astype(F32)), axis=1, keepdims=True)) * bnd_ref[n]
        return spread(jnp.stack([qn[g * QB_B:(g + 1) * QB_B] + bnd_ref[H_B_KV + h] for g, h in enumerate(heads[n])]))

    exp_pv_pass([bound(n) for n in groups])
    smallest = functools.reduce(jnp.minimum, [jnp.min(acc_scr[n][:, HEAD_DIM:HEAD_DIM + 1]) for n in groups])

    @pl.when(jnp.logical_not(smallest >= SOFTMAX_SUM_FLOOR))
    def _():
        def max_block(j, table):
            for n in groups:
                s = scores(n, j, table)
                m_scr[n] = jnp.maximum(m_scr[n], jnp.maximum(s[:, :, :LANES], s[:, :, LANES:]))

        m_scr[...] = jnp.full(m_scr.shape, -1e30, F32)
        over_blocks(max_block)
        exp_pv_pass([spread(jnp.max(m_scr[n], axis=2, keepdims=True)) for n in groups])

    for n in groups:
        for g, h in enumerate(heads[n]):
            a = acc_scr[n, g * QB_B:(g + 1) * QB_B, :]
            o_ref[:, h * HEAD_DIM:(h + 1) * HEAD_DIM] = (a[:, :HEAD_DIM] / a[:, HEAD_DIM:]).astype(BF16)


def _key_norm_max(pb, s, off, heads):
    k = pb[:s, off:off + heads * HEAD_DIM].astype(F32).reshape(s, heads, HEAD_DIM)
    return jnp.sqrt(jnp.max(jnp.sum(k * k, axis=2), axis=0)) * BOUND_SLACK


def _dsa_prompt(p32, pb, s, t5_tables):
    nq = s // QB_B
    k_top = min(TOPK_MAX, s // 4)
    once = pl.Buffered(1)
    bias_max = jnp.maximum(jnp.max(t5_tables.reshape(4, H_B, -1), axis=(0, 2)), 0.0)
    bounds = jnp.concatenate([_key_norm_max(pb, s, OFF_KB, H_B_KV), bias_max])
    return pl.pallas_call(
        functools.partial(_dsa_kernel, k_top=k_top),
        grid=(nq,),
        in_specs=[
            pl.BlockSpec(memory_space=pltpu.SMEM),
            pl.BlockSpec((QB_B, D_B), lambda i: (i, OFF_QB // D_B)),
            pl.BlockSpec((QB_B, H_IDX * D_IDX), lambda i: (i, OFF_QI // (H_IDX * D_IDX))),
            pl.BlockSpec((QB_B, LANES), lambda i: (i, OFF_MISC // LANES)),
            pl.BlockSpec((s, LANES), lambda i: (0, OFF_MISC // LANES), pipeline_mode=once),
            pl.BlockSpec((s, D_B_KV), lambda i: (0, OFF_KB // D_B_KV), pipeline_mode=once),
            pl.BlockSpec((s, D_B_KV), lambda i: (0, OFF_VB // D_B_KV), pipeline_mode=once),
            pl.BlockSpec((4 * H_B, QB_B, KB_B), lambda i: (0, 0, 0), pipeline_mode=once),
        ],
        out_specs=pl.BlockSpec((QB_B, D_B), lambda i: (i, 0)),
        out_shape=jax.ShapeDtypeStruct((s, D_B), BF16),
        scratch_shapes=[
            pltpu.VMEM((QB_B, s), I32),
            pltpu.VMEM((H_IDX, QB_B, KB_B), F32),
            pltpu.VMEM((2, QB_B, KB_B), I32),
            pltpu.VMEM((H_B_KV, GQA, QB_B, LANES), F32),
            pltpu.VMEM((H_B_KV, GQA * QB_B, LANES), F32),
        ],
        compiler_params=_cparams(1),
    )(bounds, pb, pb, p32, pb, pb, pb, t5_tables)


QB_C = 512
HEADS_PER_STEP_C = LANES // HEAD_DIM


PREP_TILE_C = 512


def _fox_prep_kernel(q_ref, k_ref, v_ref, cum_ref, qa_ref, ka_ref, va_ref):
    rows = q_ref.shape[0]
    lane = lax.broadcasted_iota(I32, (rows, HEAD_DIM), 1)
    ones = jnp.ones((rows, HEAD_DIM), BF16)
    for h in range(H_C):
        hs = slice(h * HEAD_DIM, (h + 1) * HEAD_DIM)
        c1, c2, c3 = [c.astype(F32) for c in _split3(cum_ref[:, h:h + 1])]
        unit = jnp.where(lane < 6, 1.0, 0.0)
        q_extra = jnp.where(lane < 3, jnp.where(lane == 0, c1, jnp.where(lane == 1, c2, c3)), unit)
        k_extra = jnp.where(lane < 3, unit, -jnp.where(lane == 3, c1, jnp.where(lane == 4, c2, c3)) * unit)
        qa_ref[h] = jnp.concatenate([q_ref[:, hs] * ATTN_SCALE, q_extra.astype(BF16)], axis=1)
        ka_ref[h] = jnp.concatenate([k_ref[:, hs], k_extra.astype(BF16)], axis=1)
        va_ref[h] = jnp.concatenate([v_ref[:, hs], ones], axis=1)


def _fox_prepare(pb, s, cum):
    tm = PREP_TILE_C
    row = lambda off: pl.BlockSpec((tm, D_C), lambda i: (i, off // D_C))
    out = pl.BlockSpec((H_C, tm, LANES), lambda i: (0, i, 0))
    return pl.pallas_call(
        _fox_prep_kernel,
        grid=(s // tm,),
        in_specs=[row(OFF_QC), row(OFF_KC), row(OFF_VC), pl.BlockSpec((tm, 16), lambda i: (i, 0))],
        out_specs=[out, out, out],
        out_shape=[jax.ShapeDtypeStruct((H_C, s, LANES), BF16)] * 3,
        compiler_params=_cparams(1),
    )(pb, pb, pb, cum)


def _fox_kernel(kmax_ref, q_ref, k_ref, v_ref, o_ref):
    g = pl.program_id(0)
    i = pl.program_id(1)
    row = lax.broadcasted_iota(I32, (QB_C, QB_C), 0)
    col = lax.broadcasted_iota(I32, (QB_C, QB_C), 1)
    heads = range(HEADS_PER_STEP_C)
    q = [q_ref[hh] for hh in heads]

    def scores(hh, j):
        return _dot_t(q[hh], k_ref[hh, pl.ds(pl.multiple_of(j * QB_C, QB_C), QB_C), :])

    def halves_max(s):
        return functools.reduce(jnp.maximum, [s[:, c:c + LANES] for c in range(0, QB_C, LANES)])

    def spread(m):
        return jnp.broadcast_to(m, (QB_C, QB_C))

    diag = [jnp.where(col <= row, scores(hh, i), NEG_INF) for hh in heads]

    def exp_pv_pass(m_b):
        def pv(hh, s, j):
            p = jnp.exp(s - m_b[hh]).astype(BF16)
            return _dot(p, v_ref[hh, pl.ds(pl.multiple_of(j * QB_C, QB_C), QB_C), :])

        def pv_body(j, accs):
            return tuple(accs[hh] + pv(hh, scores(hh, j), j) for hh in heads)

        return lax.fori_loop(0, i, pv_body, tuple(pv(hh, diag[hh], i) for hh in heads))

    lane = lax.broadcasted_iota(I32, (QB_C, LANES), 1)

    def bound(hh):
        qf = jnp.where(lane < HEAD_DIM, q[hh].astype(F32), 0.0)
        return spread(jnp.sqrt(jnp.sum(qf * qf, axis=1, keepdims=True)) * kmax_ref[g * HEADS_PER_STEP_C + hh])

    accs = exp_pv_pass([bound(hh) for hh in heads])
    smallest = functools.reduce(jnp.minimum, [jnp.min(accs[hh][:, HEAD_DIM:HEAD_DIM + 1]) for hh in heads])

    def exact():
        def max_body(j, ms):
            return tuple(jnp.maximum(ms[hh], halves_max(scores(hh, j))) for hh in heads)

        ms = lax.fori_loop(0, i, max_body, tuple(halves_max(diag[hh]) for hh in heads))
        return exp_pv_pass([spread(jnp.max(ms[hh], axis=1, keepdims=True)) for hh in heads])

    accs = lax.cond(smallest >= SOFTMAX_SUM_FLOOR, lambda: accs, exact)
    o_ref[...] = jnp.concatenate(
        [accs[hh][:, :HEAD_DIM] / accs[hh][:, HEAD_DIM:] for hh in heads], axis=1).astype(BF16)


def _fox_prompt(qa, ka, va, kmax, s):
    nq = s // QB_C
    ng = H_C // HEADS_PER_STEP_C
    hp = HEADS_PER_STEP_C
    return pl.pallas_call(
        _fox_kernel,
        grid=(ng, nq),
        in_specs=[
            pl.BlockSpec(memory_space=pltpu.SMEM),
            pl.BlockSpec((hp, QB_C, LANES), lambda g, i: (g, i, 0)),
            pl.BlockSpec((hp, s, LANES), lambda g, i: (g, 0, 0)),
            pl.BlockSpec((hp, s, LANES), lambda g, i: (g, 0, 0)),
        ],
        out_specs=pl.BlockSpec((QB_C, LANES), lambda g, i: (i, g)),
        out_shape=jax.ShapeDtypeStruct((s, D_C), BF16),
        compiler_params=_cparams(2),
    )(kmax, qa, ka, va)


def _softmax_pv(s, v_all):
    m = jnp.max(s, axis=1, keepdims=True)
    p = jnp.exp(s - m)
    l = jnp.sum(p, axis=1, keepdims=True)
    return _dot(p.astype(BF16), v_all) / l


def _with_new_rows(cache, new, pad_rows):
    parts = [cache.astype(BF16), new]
    if pad_rows:
        parts.append(jnp.zeros((pad_rows, new.shape[1]), BF16))
    return jnp.concatenate(parts, axis=0)


def _sample_kernel(qa_ref, ka_ref, va_ref, kvb_ref, qi_ref, qb_ref, qc_ref, kc_ref, vc_ref, misc16_ref,
                   misc32_ref, ca_ref, cb_ref, cbi_ref, cc_ref, clf_ref, clft_ref, lfn_ref, lfnt_ref,
                   bias_a_ref, t5_ref, triu_ref, tril_ref,
                   oa_ref, ob_ref, oc_ref, *, t, past, na, k_top):
    la = na + LANES
    lk = past + LANES
    pad = LANES - t

    ca = ca_ref[...]
    ka_all = _with_new_rows(ca[:, :D_A], ka_ref[...], pad)
    va_all = _with_new_rows(ca[:, D_A:], va_ref[...], pad)
    for h in range(H_A):
        hs = slice(h * HEAD_DIM, (h + 1) * HEAD_DIM)
        s = _dot_t(qa_ref[:, hs], ka_all[:, hs]) * ATTN_SCALE + bias_a_ref[h]
        oa_ref[:, hs] = _softmax_pv(s, va_all[:, hs]).astype(BF16)

    col = lax.broadcasted_iota(I32, (t, lk), 1)
    ki_all = _with_new_rows(cbi_ref[...], misc16_ref[:, :D_IDX], pad)
    wi = misc32_ref[:, MISC_WI:MISC_WI + H_IDX] * (H_IDX ** -0.5 * D_IDX ** -0.5)
    acc = jnp.zeros((t, lk), F32)
    for h in range(H_IDX):
        isc = _dot_t(qi_ref[:, h * D_IDX:(h + 1) * D_IDX], ki_all)
        acc = acc + jnp.maximum(isc, 0.0) * wi[:, h:h + 1]
    keys = jnp.where(col < past + t, _sortable(acc), jnp.int32(NEG_KEY))

    def count(pred):
        return jnp.sum(jnp.where(pred, 1.0, 0.0), axis=1, keepdims=True)

    thr, _ = _kth_largest(lambda cand: count(keys >= cand), t, k_top)
    live = thr > jnp.int32(NEG_KEY)
    thr = jnp.maximum(thr, jnp.int32(NEG_KEY))
    need = float(k_top) - count(keys > thr)
    jstar = _tie_limit(lambda cand: count((keys == thr) & (col < cand)), need, t)
    sel = (keys > thr) | ((keys == thr) & (col < jnp.where(live, jstar, 0)))

    cb = cb_ref[...]
    kb_all = _with_new_rows(cb[:, :D_B_KV], kvb_ref[:, :D_B_KV], pad)
    vb_all = _with_new_rows(cb[:, D_B_KV:], kvb_ref[:, D_B_KV:], pad)
    for n in range(H_B_KV):
        ns = slice(n * HEAD_DIM, (n + 1) * HEAD_DIM)
        heads = [n * GQA + g for g in range(GQA)]
        q3 = jnp.concatenate([qb_ref[:, h * HEAD_DIM:(h + 1) * HEAD_DIM] for h in heads], axis=0)
        s = (_dot_t(q3, kb_all[:, ns]) * ATTN_SCALE).reshape(GQA, t, lk) + jnp.stack([t5_ref[h] for h in heads])
        s = jnp.where(sel[None], s, NEG_INF).reshape(GQA * t, lk)
        o3 = _softmax_pv(s, vb_all[:, ns]).astype(BF16)
        for g, h in enumerate(heads):
            ob_ref[:, h * HEAD_DIM:(h + 1) * HEAD_DIM] = o3[g * t:(g + 1) * t, :]

    cc = cc_ref[...]
    kc_all = _with_new_rows(cc[:, :D_C], kc_ref[...], pad)
    vc_all = _with_new_rows(cc[:, D_C:], vc_ref[...], pad)
    lft = jnp.concatenate([clft_ref[...], lfnt_ref[...]], axis=1)
    t1, t2, t3 = _split3(lft)
    triu = triu_ref[...]
    cum_t = _dot(t1, triu) + _dot(t2, triu) + _dot(t3, triu)
    total = jnp.sum(clf_ref[...], axis=0, keepdims=True)
    n1, n2, n3 = _split3(lfn_ref[...])
    tril = tril_ref[...]
    cum_q = total + _dot(tril, n1) + _dot(tril, n2) + _dot(tril, n3)
    row = lax.broadcasted_iota(I32, (t, lk), 0)
    causal = col <= past + row
    for h in range(H_C):
        hs = slice(h * HEAD_DIM, (h + 1) * HEAD_DIM)
        s = _dot_t(qc_ref[:, hs], kc_all[:, hs]) * ATTN_SCALE + cum_q[:, h:h + 1] - cum_t[h:h + 1, :]
        s = jnp.where(causal, s, NEG_INF)
        oc_ref[:, hs] = _softmax_pv(s, vc_all[:, hs]).astype(BF16)


def _sample_mixers(p32, pb, s, ca, cb, cbi, cc, clf, clft, lfn, lfnt, bias_a, t5_tab, triu, tril, layer):
    nb, na = ca.shape[1], ca.shape[2]
    past = cb.shape[2]
    t = lfn.shape[1]
    k_top = min(TOPK_MAX, (past + t) // 4)
    r0 = s // t
    lk = past + LANES
    la = na + LANES
    row = lambda width, off: pl.BlockSpec((t, width), lambda b: (r0 + b, off // width))
    cache = lambda rows, width: pl.BlockSpec((None, None, rows, width), lambda b: (layer, b, 0, 0))
    full = lambda shape: pl.BlockSpec(shape, lambda b: (0,) * len(shape))
    in_specs = [
        row(D_A, OFF_QA), row(D_A, OFF_KA), row(D_A, OFF_VA), row(2 * D_B_KV, OFF_KB),
        row(H_IDX * D_IDX, OFF_QI), row(D_B, OFF_QB), row(D_C, OFF_QC), row(D_C, OFF_KC), row(D_C, OFF_VC),
        row(LANES, OFF_MISC), row(LANES, OFF_MISC),
        cache(na, 2 * D_A), cache(past, 2 * D_B_KV), cache(past, D_IDX), cache(past, 2 * D_C),
        cache(past, 16), cache(16, past),
        pl.BlockSpec((None, t, 16), lambda b: (b, 0, 0)),
        pl.BlockSpec((None, 16, LANES), lambda b: (b, 0, 0)),
        full((H_A, t, la)), full((H_B, t, lk)), full((lk, lk)), full((t, t)),
    ]
    out = lambda width: pl.BlockSpec((t, width), lambda b: (b, 0))
    return pl.pallas_call(
        functools.partial(_sample_kernel, t=t, past=past, na=na, k_top=k_top),
        grid=(nb,),
        in_specs=in_specs,
        out_specs=[out(D_A), out(D_B), out(D_C)],
        out_shape=[jax.ShapeDtypeStruct((nb * t, w), BF16) for w in (D_A, D_B, D_C)],
        compiler_params=_cparams(1),
    )(pb, pb, pb, pb, pb, pb, pb, pb, pb, pb, p32, ca, cb, cbi, cc, clf, clft, lfn, lfnt,
      bias_a, t5_tab, triu, tril)


def _outproj_kernel(x_ref, g_ref, ap, bp, cp, as_, bs, cs, w_ref, o_ref, *, n_prompt_blocks):
    i = pl.program_id(0)

    def run(a, b, c):
        y = (_dot(a[...], w_ref[:D_A, :]) + _dot(b[...], w_ref[D_A:D_A + D_B, :])
             + _dot(c[...], w_ref[D_A + D_B:, :]))
        o_ref[...] = x_ref[...] + g_ref[...] * y

    pl.when(i < n_prompt_blocks)(lambda: run(ap, bp, cp))
    pl.when(i >= n_prompt_blocks)(lambda: run(as_, bs, cs))


def _out_project(x, gate, mix_p, mix_s, w_out_b, layer, n_prompt_blocks):
    m = x.shape[0]
    tm = TOKEN_TILE
    last_p = n_prompt_blocks - 1
    p_map = lambda i: (jnp.minimum(i, last_p), 0)
    s_map = lambda i: (jnp.maximum(i - n_prompt_blocks, 0), 0)
    widths = (D_A, D_B, D_C)
    return pl.pallas_call(
        functools.partial(_outproj_kernel, n_prompt_blocks=n_prompt_blocks),
        grid=(m // tm,),
        in_specs=[
            pl.BlockSpec((tm, D_MODEL), lambda i: (i, 0)),
            pl.BlockSpec((tm, D_MODEL), lambda i: (_mod_block_index(i, n_prompt_blocks), 0)),
            *[pl.BlockSpec((tm, w), p_map) for w in widths],
            *[pl.BlockSpec((tm, w), s_map) for w in widths],
            pl.BlockSpec((None, D_MODEL, D_MODEL), lambda i: (layer, 0, 0)),
        ],
        out_specs=pl.BlockSpec((tm, D_MODEL), lambda i: (i, 0)),
        out_shape=jax.ShapeDtypeStruct((m, D_MODEL), F32),
        compiler_params=_cparams(1),
    )(x, gate, *mix_p, *mix_s, w_out_b)


def _lane_pick(vals, lane, idx):
    return jnp.sum(jnp.where(lane == idx, vals, 0.0), axis=1, keepdims=True)


def _first_argmax(vals, lane):
    m = jnp.max(vals, axis=1, keepdims=True)
    idx = jnp.min(jnp.where(vals == m, lane, float(LANES)), axis=1, keepdims=True)
    return m, idx


def _router_kernel(x_ref, gain_ref, sc_ref, sh_ref, wr_ref, br_ref, tri_ref, h_ref, meta_ref, cnt_ref, carry):
    @pl.when(pl.program_id(0) == 0)
    def _():
        carry[...] = jnp.zeros_like(carry)

    h = _norm_mod(x_ref[...], gain_ref[...], sc_ref[...], sh_ref[...])
    hb = h.astype(BF16)
    h_ref[...] = h
    scores = jax.nn.sigmoid(_dot(hb, wr_ref[...]))
    lane_i = lax.broadcasted_iota(I32, scores.shape, 1)
    lane = lane_i.astype(F32)
    sel = jnp.where(lane_i < N_EXPERTS, scores + br_ref[...], NEG_INF)
    group = (lane_i // (N_EXPERTS // N_GROUPS)).astype(F32)

    best = None
    for g in range(N_GROUPS):
        in_g = jnp.where(group == float(g), sel, NEG_INF)
        m1, i1 = _first_argmax(in_g, lane)
        m2 = jnp.max(jnp.where(lane == i1, NEG_INF, in_g), axis=1, keepdims=True)
        gs = m1 + m2
        if best is None:
            best, gbest = gs, jnp.zeros_like(i1)
        else:
            better = gs > best
            best = jnp.where(better, gs, best)
            gbest = jnp.where(better, float(g), gbest)

    in_best = jnp.where(group == gbest, sel, NEG_INF)
    _, e0 = _first_argmax(in_best, lane)
    _, e1 = _first_argmax(jnp.where(lane == e0, NEG_INF, in_best), lane)
    w0 = _lane_pick(scores, lane, e0)
    w1 = _lane_pick(scores, lane, e1)
    wsum = w0 + w1

    onehot = jnp.where((lane == e0) | (lane == e1), 1.0, 0.0)
    before = _dot(tri_ref[...], onehot.astype(BF16)) + carry[...]
    r0 = _lane_pick(before, lane, e0)
    r1 = _lane_pick(before, lane, e1)
    carry[...] = carry[...] + jnp.sum(onehot, axis=0, keepdims=True)
    cnt_ref[...] = carry[...]

    meta = jnp.zeros(scores.shape, F32)
    for k, v in enumerate((e0, e1, r0, r1, w0 / wsum, w1 / wsum)):
        meta = jnp.where(lane_i == k, v, meta)
    meta_ref[...] = meta


def _router(x, gain, sc, sh, w_router_p, b_router_p, tri, n_prompt_blocks):
    m = x.shape[0]
    tm = TOKEN_TILE
    mod_map = lambda i: (_mod_block_index(i, n_prompt_blocks), 0)
    return pl.pallas_call(
        _router_kernel,
        grid=(m // tm,),
        in_specs=[
            pl.BlockSpec((tm, D_MODEL), lambda i: (i, 0)),
            pl.BlockSpec((1, D_MODEL), lambda i: (0, 0)),
            pl.BlockSpec((tm, D_MODEL), mod_map),
            pl.BlockSpec((tm, D_MODEL), mod_map),
            pl.BlockSpec((D_MODEL, LANES), lambda i: (0, 0)),
            pl.BlockSpec((1, LANES), lambda i: (0, 0)),
            pl.BlockSpec((tm, tm), lambda i: (0, 0)),
        ],
        out_specs=[
            pl.BlockSpec((tm, D_MODEL), lambda i: (i, 0)),
            pl.BlockSpec((tm, LANES), lambda i: (i, 0)),
            pl.BlockSpec((1, LANES), lambda i: (0, 0)),
        ],
        out_shape=[
            jax.ShapeDtypeStruct((m, D_MODEL), F32),
            jax.ShapeDtypeStruct((m, LANES), F32),
            jax.ShapeDtypeStruct((1, LANES), F32),
        ],
        scratch_shapes=[pltpu.VMEM((1, LANES), F32)],
        compiler_params=_cparams(1),
    )(x, gain, sc, sh, w_router_p, b_router_p, tri)


EXPERT_ROWS = 384
EXPERT_TF = 512


def _row_copy(src_hbm, src_row, dst_buf, dst_row, sem):
    return pltpu.make_async_copy(src_hbm.at[pl.ds(src_row, 1), :], dst_buf.at[pl.ds(dst_row, 1), :], sem)


def _expert_kernel(be_ref, nu_ref, st_ref, h_hbm, wg_ref, wu_ref, wd_ref, o_ref, xrows, x16, sems):
    b = pl.program_id(0)
    f = pl.program_id(1)
    n_used = nu_ref[0]
    rows_per_step = EXPERT_ROWS // (D_EXPERT // EXPERT_TF)

    def start_rows(blk, first, count):
        for r in range(count):
            tok = st_ref[blk * EXPERT_ROWS + first + r]
            _row_copy(h_hbm, tok, xrows.at[blk % 2], first + r, sems.at[blk % 2]).start()

    @pl.when((b == 0) & (f == 0))
    def _():
        start_rows(0, 0, EXPERT_ROWS)

    @pl.when((f == 0) & (b < n_used))
    def _():
        for r in range(EXPERT_ROWS):
            _row_copy(h_hbm, 0, xrows.at[b % 2], r, sems.at[b % 2]).wait()
        x16[...] = xrows[b % 2].astype(BF16)

    @pl.when(b + 1 < n_used)
    def _():
        start_rows(b + 1, f * rows_per_step, rows_per_step)

    @pl.when((f == 0) & (b >= n_used))
    def _():
        o_ref[...] = jnp.zeros(o_ref.shape, F32)

    @pl.when(b < n_used)
    def _():
        x = x16[...]
        a = _dot(x, wg_ref[...].astype(BF16))
        u = _dot(x, wu_ref[...].astype(BF16))
        act = (a * jax.nn.sigmoid(a) * u).astype(BF16)
        y = _dot(act, wd_ref[...].astype(BF16))

        @pl.when(f == 0)
        def _():
            o_ref[...] = y

        @pl.when(f > 0)
        def _():
            o_ref[...] += y


def _experts(h2, slot_tok, block_e, n_used, w_g, w_u, w_d, layer):
    nslots = slot_tok.shape[0]
    nb = nslots // EXPERT_ROWS
    nf = D_EXPERT // EXPERT_TF

    def blk(b, nu):
        return jnp.minimum(b, nu[0] - 1)

    def fidx(b, f, nu):
        return jnp.where(b < nu[0], f, nf - 1)

    grid_spec = pltpu.PrefetchScalarGridSpec(
        num_scalar_prefetch=3,
        grid=(nb, nf),
        in_specs=[
            pl.BlockSpec(memory_space=pl.ANY),
            pl.BlockSpec((None, None, D_MODEL, EXPERT_TF),
                         lambda b, f, be, nu, st: (layer, be[blk(b, nu)], 0, fidx(b, f, nu))),
            pl.BlockSpec((None, None, D_MODEL, EXPERT_TF),
                         lambda b, f, be, nu, st: (layer, be[blk(b, nu)], 0, fidx(b, f, nu))),
            pl.BlockSpec((None, None, EXPERT_TF, D_MODEL),
                         lambda b, f, be, nu, st: (layer, be[blk(b, nu)], fidx(b, f, nu), 0)),
        ],
        out_specs=pl.BlockSpec((EXPERT_ROWS, D_MODEL), lambda b, f, be, nu, st: (b, 0)),
        scratch_shapes=[
            pltpu.VMEM((2, EXPERT_ROWS, D_MODEL), F32),
            pltpu.VMEM((EXPERT_ROWS, D_MODEL), BF16),
            pltpu.SemaphoreType.DMA((2,)),
        ],
    )
    return pl.pallas_call(
        _expert_kernel,
        grid_spec=grid_spec,
        out_shape=jax.ShapeDtypeStruct((nslots, D_MODEL), F32),
        compiler_params=_cparams(2),
    )(block_e, n_used, slot_tok, h2, w_g, w_u, w_d)


COMBINE_TILE = 256


def _combine_kernel(d0_ref, d1_ref, x_ref, g_ref, meta_ref, y_hbm, o_ref, ybuf, sems):
    i = pl.program_id(0)
    n = pl.num_programs(0)

    def start_rows(blk):
        slot = blk % 2

        def body(r, _):
            t = blk * COMBINE_TILE + r
            _row_copy(y_hbm, d0_ref[t], ybuf.at[slot, 0], r, sems.at[slot]).start()
            _row_copy(y_hbm, d1_ref[t], ybuf.at[slot, 1], r, sems.at[slot]).start()
            return 0

        lax.fori_loop(0, COMBINE_TILE, body, 0, unroll=8)

    @pl.when(i == 0)
    def _():
        start_rows(i)

    slot = i % 2
    for k in range(2):
        for r in range(COMBINE_TILE):
            _row_copy(y_hbm, 0, ybuf.at[slot, k], r, sems.at[slot]).wait()

    @pl.when(i + 1 < n)
    def _():
        start_rows(i + 1)

    moe = ybuf[slot, 0] * meta_ref[:, 4:5] + ybuf[slot, 1] * meta_ref[:, 5:6]
    o_ref[...] = x_ref[...] + g_ref[...] * moe


def _combine(x, gate, meta, yb, dest0, dest1, n_prompt_blocks):
    m = x.shape[0]
    tm = COMBINE_TILE
    ratio = TOKEN_TILE // tm
    mod_map = lambda i, d0, d1: (jnp.where(i < n_prompt_blocks * ratio, 0, i - (n_prompt_blocks - 1) * ratio), 0)
    row = lambda width: pl.BlockSpec((tm, width), lambda i, d0, d1: (i, 0))
    grid_spec = pltpu.PrefetchScalarGridSpec(
        num_scalar_prefetch=2,
        grid=(m // tm,),
        in_specs=[row(D_MODEL), pl.BlockSpec((tm, D_MODEL), mod_map), row(LANES), pl.BlockSpec(memory_space=pl.ANY)],
        out_specs=row(D_MODEL),
        scratch_shapes=[pltpu.VMEM((2, 2, tm, D_MODEL), F32), pltpu.SemaphoreType.DMA((2,))],
    )
    return pl.pallas_call(
        _combine_kernel,
        grid_spec=grid_spec,
        out_shape=jax.ShapeDtypeStruct((m, D_MODEL), F32),
        compiler_params=_cparams(1),
    )(dest0, dest1, x, gate, meta, yb)


def _t5_bucket(rel):
    nb = T5_BUCKETS // 2
    max_exact = nb // 2
    ret = jnp.where(rel < 0, nb, 0)
    n = jnp.abs(rel)
    nf = jnp.maximum(n, 1).astype(F32)
    large = max_exact + (jnp.log(nf / max_exact) / math.log(T5_MAX_DIST / max_exact) * (nb - max_exact)).astype(I32)
    large = jnp.minimum(large, nb - 1)
    return ret + jnp.where(n < max_exact, n, large)


def _toeplitz(by_rel, q0, k0, nq, nk):
    d = jnp.arange(-(nk - 1), nq)
    rev = by_rel(q0 - k0 + d)[:, ::-1]
    h, length = rev.shape
    flat = jnp.tile(jnp.pad(rev, ((0, 0), (0, 1))), (1, nq))[:, :nq * length]
    return flat.reshape(h, nq, length)[:, :, nq - 1:nq - 1 + nk]


def _band_bias(table, q0, k0, nq, nk, n_valid):
    by_rel = lambda rel: table.astype(F32)[:, jnp.clip(rel, -A_REL_FUTURE, A_REL_PAST) + A_REL_FUTURE]
    qc = (q0 + jnp.arange(nq))[:, None] // CHUNK
    col = jnp.arange(nk)[None, :]
    kc = (k0 + col) // CHUNK
    ok = (col < n_valid) & (kc <= qc) & (kc >= qc - A_LEFT_CHUNKS)
    return jnp.where(ok[None], _toeplitz(by_rel, q0, k0, nq, nk), NEG_INF)


def _t5_bias(t5, q0, k0, nq, nk):
    return _toeplitz(lambda rel: t5.astype(F32)[_t5_bucket(rel)].T, q0, k0, nq, nk)


def _relayout_w_in(w_in_l):
    qa, ka, va, qb, kb, vb, qi, ki, wi, qc, kc, vc, fc = jnp.split(w_in_l, PROJ_SPLITS, axis=1)
    pad = jnp.zeros((D_MODEL, P_COLS - OFF_MISC - D_IDX - H_IDX - H_C), w_in_l.dtype)
    w = jnp.concatenate([qa, ka, va, kb, vb, qi, qb, qc, kc, vc, ki, wi, fc, pad], axis=1).astype(BF16)
    return jnp.swapaxes(w.reshape(D_MODEL, P_COLS // PROJ_TN, PROJ_TN), 0, 1)


def _column_params(qk_gain_l, b_forget_l):
    ones = lambda n: jnp.ones((n,), F32)
    g = qk_gain_l.astype(F32)
    gq = jnp.concatenate([
        jnp.tile(g[0], H_A), jnp.tile(g[1], H_A), ones(D_A),
        jnp.tile(g[3], H_B_KV), ones(D_B_KV), ones(H_IDX * D_IDX),
        jnp.tile(g[2], H_B), jnp.tile(g[4], H_C), jnp.tile(g[5], H_C), ones(D_C), ones(P_COLS - OFF_MISC)])
    bq = jnp.zeros((P_COLS,), F32).at[OFF_MISC + MISC_FC:OFF_MISC + MISC_FC + H_C].set(b_forget_l.astype(F32))
    return gq[None, :], bq[None, :]


def _tri(n, *, strict=False, upper=False):
    r = jnp.arange(n)[:, None]
    c = jnp.arange(n)[None, :]
    m = (r < c if strict else r <= c) if upper else (c < r if strict else c <= r)
    return m.astype(BF16)


def kernel(x_prompt, x_sample, c_prompt, c_sample, cache_a_kv, cache_b_kv, cache_b_kidx, cache_c_kv, cache_c_logf,
           w_ada, b_ada, norm_gain, w_in, b_forget, qk_gain, rel_bias_a, t5_bias, w_out, w_router, b_router,
           w_e_gate, w_e_up, w_e_down):
    depth = w_in.shape[0]
    bp, s, d = x_prompt.shape
    nb, t, _ = x_sample.shape
    ns = nb * t
    tm = TOKEN_TILE
    assert bp == 1 and d == D_MODEL and s % tm == 0 and ns % tm == 0 and s % QB_C == 0
    n_pb = s // tm
    m = s + ns
    na = cache_a_kv.shape[2]
    past = cache_b_kv.shape[2]
    keep = min(A_LEFT_CHUNKS * CHUNK, s)

    x = jnp.concatenate([x_prompt.reshape(s, d), x_sample.reshape(ns, d)], axis=0)
    c_all = jnp.concatenate([c_prompt, c_sample, jnp.zeros((-(bp + nb) % 8, d), F32)], axis=0)

    t5_far = t5_bias.astype(F32)[T5_BUCKETS // 2 - 1]
    t5_tables = jnp.concatenate(
        [_t5_bias(t5_bias, off, 0, QB_B, KB_B) - t5_far[:, None, None] for off in (QB_B, 2 * QB_B, 0)]
        + [jnp.zeros((H_B, QB_B, KB_B), F32)], axis=0)
    t5_s = _t5_bias(t5_bias, past, 0, t, past + LANES)
    bd = ((jnp.arange(PROJ_TN)[:, None] // HEAD_DIM == jnp.arange(PROJ_TN)[None, :] // HEAD_DIM)
          .astype(F32) / HEAD_DIM).astype(BF16)
    tri_cum = _tri(CUM_TILE)
    tri_rank = _tri(tm, strict=True)
    triu_s = _tri(past + LANES, upper=True)
    tril_s = _tri(t)
    w_out_b = w_out.astype(BF16)
    w_router_p = jnp.pad(w_router, ((0, 0), (0, LANES - N_EXPERTS))).astype(BF16)
    b_router_p = jnp.pad(b_router.astype(F32), (0, LANES - N_EXPERTS))[None, :]

    ca = cache_a_kv.reshape(depth, nb, na, 2 * D_A)
    cb = cache_b_kv.reshape(depth, nb, past, 2 * D_B_KV)
    cc = cache_c_kv.reshape(depth, nb, past, 2 * D_C)
    clf = jnp.pad(cache_c_logf.astype(F32), ((0, 0), (0, 0), (0, 0), (0, 16 - H_C)))
    clft = jnp.swapaxes(clf, 2, 3)

    n_asg = 2 * m
    n_eb = -(-n_asg // EXPERT_ROWS) + N_EXPERTS
    tok = jnp.arange(m, dtype=I32)

    states_p, states_s = [], []
    for l in range(depth):
        mod = _ada(c_all, w_ada, b_ada, l)
        mods = []
        for part in jnp.split(mod, 6, axis=1):
            mods.append(jnp.concatenate([jnp.broadcast_to(part[:1], (tm, d)), jnp.repeat(part[bp:bp + nb], t, axis=0)], 0))
        sh1, sc1, g1, sh2, sc2, g2 = mods

        gq, bq = _column_params(qk_gain[l], b_forget[l])
        p32, pb = _project(x, norm_gain[l, 0][None, :], sc1, sh1, _relayout_w_in(w_in[l]), gq, bq, bd, n_pb)

        bias_a_p = _band_bias(rel_bias_a[l], (A_KEY_BLOCKS - 1) * QB_A, 0, QB_A, A_KEY_BLOCKS * QB_A,
                              A_KEY_BLOCKS * QB_A)
        oa_p = _band_prompt(pb, s, bias_a_p)
        ob_p = _dsa_prompt(p32, pb, s, t5_tables)
        cum = _cumsum_rows(p32, s, tri_cum)[:, MISC_FC:MISC_FC + 16]
        oc_p = _fox_prompt(*_fox_prepare(pb, s, cum), _key_norm_max(pb, s, OFF_KC, H_C), s)

        lfn = p32[s:, OFF_MISC + MISC_FC:OFF_MISC + MISC_FC + 16].reshape(nb, t, 16)
        lfnt = jnp.pad(jnp.swapaxes(lfn, 1, 2), ((0, 0), (0, 0), (0, LANES - t)))
        bias_a_s = _band_bias(rel_bias_a[l], past, past - na, t, na + LANES, na + t)
        oa_s, ob_s, oc_s = _sample_mixers(p32, pb, s, ca, cb, cache_b_kidx, cc, clf, clft, lfn, lfnt,
                                          bias_a_s, t5_s, triu_s, tril_s, l)

        x = _out_project(x, g1, (oa_p, ob_p, oc_p), (oa_s, ob_s, oc_s), w_out_b, l, n_pb)

        h2, meta, counts = _router(x, norm_gain[l, 1][None, :], sc2, sh2, w_router_p, b_router_p, tri_rank, n_pb)
        e0 = meta[:, 0].astype(I32)
        e1 = meta[:, 1].astype(I32)
        counts = counts[0, :N_EXPERTS].astype(I32)
        padded = (counts + EXPERT_ROWS - 1) // EXPERT_ROWS * EXPERT_ROWS
        pend = jnp.cumsum(padded)
        pstart = pend - padded
        dest0 = pstart[e0] + meta[:, 2].astype(I32)
        dest1 = pstart[e1] + meta[:, 3].astype(I32)
        slot_tok = jnp.zeros((n_eb * EXPERT_ROWS,), I32).at[dest0].set(tok).at[dest1].set(tok)
        block_e = jnp.minimum(jnp.searchsorted(pend, jnp.arange(n_eb, dtype=I32) * EXPERT_ROWS, side='right'),
                              N_EXPERTS - 1).astype(I32)
        n_used = (pend[-1:] // EXPERT_ROWS).astype(I32)
        yb = _experts(h2, slot_tok, block_e, n_used, w_e_gate, w_e_up, w_e_down, l)
        x = _combine(x, g2, meta, yb, dest0, dest1, n_pb)

        def states(rows, nbatch, a_rows):
            r = p32[rows]
            n = r.shape[0] // nbatch
            kv = lambda off, heads: r[:, off:off + 2 * heads * HEAD_DIM].reshape(nbatch, n, 2, heads, HEAD_DIM)
            return (kv(OFF_KA, H_A)[:, n - a_rows:], kv(OFF_KB, H_B_KV),
                    r[:, OFF_MISC:OFF_MISC + D_IDX].reshape(nbatch, n, D_IDX), kv(OFF_KC, H_C),
                    r[:, OFF_MISC + MISC_FC:OFF_MISC + MISC_FC + H_C].reshape(nbatch, n, H_C))

        states_p.append(states(slice(0, s), bp, keep))
        states_s.append(states(slice(s, m), nb, t))

    stk = lambda sts, i: jnp.stack([st[i] for st in sts], axis=0)
    return (x[:s].reshape(bp, s, d), x[s:].reshape(nb, t, d),
            *[stk(states_p, i) for i in range(5)], *[stk(states_s, i) for i in range(5)])
```

```python
import functools
import math

import jax
import jax.numpy as jnp
from jax import lax
from jax.experimental import pallas as pl
from jax.experimental.pallas import tpu as pltpu

F32 = jnp.float32
BF16 = jnp.bfloat16
I32 = jnp.int32

D_MODEL = 2048
HEAD_DIM = 64
CHUNK = 64
EPS = 1e-6
ATTN_SCALE = HEAD_DIM ** -0.5
H_A = 8
A_LEFT_CHUNKS = 8
A_REL_PAST = 128
A_REL_FUTURE = CHUNK - 1
H_B = 12
H_B_KV = 4
GQA = H_B // H_B_KV
H_IDX = 16
D_IDX = 64
TOPK_MAX = 256
H_C = 12
T5_BUCKETS = 32
T5_MAX_DIST = 128
N_EXPERTS = 32
N_GROUPS = 4
D_EXPERT = 1024

D_A = H_A * HEAD_DIM
D_B = H_B * HEAD_DIM
D_B_KV = H_B_KV * HEAD_DIM
D_C = H_C * HEAD_DIM
PROJ_SIZES = (D_A, D_A, D_A, D_B, D_B_KV, D_B_KV, H_IDX * D_IDX, D_IDX, H_IDX, D_C, D_C, D_C, H_C)
PROJ_SPLITS = tuple(sum(PROJ_SIZES[:i + 1]) for i in range(len(PROJ_SIZES) - 1))

OFF_QA, OFF_KA, OFF_VA = 0, 512, 1024
OFF_KB, OFF_VB = 1536, 1792
OFF_QI = 2048
OFF_QB = 3072
OFF_QC, OFF_KC, OFF_VC = 3840, 4608, 5376
OFF_MISC = 6144
MISC_WI = 64
MISC_FC = 80
P_COLS = 6400
PROJ_TN = 256
NORM_COL_BLOCKS = (0, 1, 2, 3, 6, 12, 13, 14, 15, 16, 17, 18, 19, 20)
MISC_COL_BLOCK = OFF_MISC // PROJ_TN

TOKEN_TILE = 512
LANES = 128
VMEM_LIMIT = 56 * 1024 * 1024

NEG_INF = float("-inf")
SOFTMAX_SUM_FLOOR = 2.0 ** -40
BOUND_SLACK = 1.001
NEG_KEY = (0xFF800000 ^ 0x7FFFFFFF) - (1 << 32)
IDX_BIG = 1 << 30


def _cparams(n_axes):
    return pltpu.CompilerParams(dimension_semantics=("arbitrary",) * n_axes, vmem_limit_bytes=VMEM_LIMIT)


def _split3(x):
    x1 = x.astype(BF16)
    r1 = x - x1.astype(F32)
    x2 = r1.astype(BF16)
    r2 = r1 - x2.astype(F32)
    return x1, x2, r2.astype(BF16)


def _dot(a, b):
    return jnp.dot(a, b, preferred_element_type=F32)


def _dot_t(a, b):
    return lax.dot_general(a, b, (((1,), (1,)), ((), ())), preferred_element_type=F32)


def _sortable(x):
    b = lax.bitcast_convert_type(x, I32)
    return jnp.where(b < 0, b ^ jnp.int32(0x7FFFFFFF), b)


def _ada_kernel(c_ref, w_ref, b_ref, o_ref):
    c = c_ref[...]
    a = (c * jax.nn.sigmoid(c)).astype(BF16)
    o_ref[...] = _dot(a, w_ref[...].astype(BF16)) + b_ref[...]


def _ada(c_all, w_ada, b_ada, layer):
    rows = c_all.shape[0]
    n_out = w_ada.shape[2]
    tn = 1024
    return pl.pallas_call(
        _ada_kernel,
        grid=(n_out // tn,),
        in_specs=[
            pl.BlockSpec((rows, D_MODEL), lambda n: (0, 0)),
            pl.BlockSpec((None, D_MODEL, tn), lambda n: (layer, 0, n)),
            pl.BlockSpec((None, 1, tn), lambda n: (layer, 0, n)),
        ],
        out_specs=pl.BlockSpec((rows, tn), lambda n: (0, n)),
        out_shape=jax.ShapeDtypeStruct((rows, n_out), F32),
        compiler_params=_cparams(1),
    )(c_all, w_ada, b_ada.reshape(b_ada.shape[0], 1, n_out))


def _mod_block_index(i, n_prompt_blocks):
    return jnp.where(i < n_prompt_blocks, 0, i - n_prompt_blocks + 1)


def _norm_mod(x, gain, sc, sh):
    ms = jnp.mean(x * x, axis=-1, keepdims=True)
    return (x * lax.rsqrt(ms + EPS) * gain) * (1.0 + sc) + sh


def _proj_kernel(x_ref, gain_ref, sc_ref, sh_ref, w_ref, gq_ref, bq_ref, bd_ref, o32_ref, o16_ref, h_scr):
    n = pl.program_id(1)

    @pl.when(n == 0)
    def _():
        h_scr[...] = _norm_mod(x_ref[...], gain_ref[...], sc_ref[...], sh_ref[...]).astype(BF16)

    y = _dot(h_scr[...], w_ref[...])

    is_norm = functools.reduce(jnp.logical_or, [n == b for b in NORM_COL_BLOCKS])
    is_misc = n == MISC_COL_BLOCK

    def emit(v):
        o32_ref[...] = v
        o16_ref[...] = v.astype(BF16)

    @pl.when(is_norm)
    def _():
        s1, s2, s3 = _split3(y * y)
        bd = bd_ref[...]
        ms = _dot(s1, bd) + _dot(s2, bd) + _dot(s3, bd)
        emit(y * lax.rsqrt(ms + EPS) * gq_ref[...])

    @pl.when(is_misc)
    def _():
        lane = lax.broadcasted_iota(I32, y.shape, 1)
        z = y + bq_ref[...]
        logsig = jnp.minimum(z, 0.0) - jnp.log(1.0 + jnp.exp(-jnp.abs(z)))
        emit(jnp.where((lane >= MISC_FC) & (lane < MISC_FC + H_C), logsig, y))

    @pl.when(jnp.logical_not(jnp.logical_or(is_norm, is_misc)))
    def _():
        emit(y)


def _project(x, gain, sc, sh, w_r, gq, bq, bd, n_prompt_blocks):
    m = x.shape[0]
    tm = TOKEN_TILE
    mod_map = lambda i, n: (_mod_block_index(i, n_prompt_blocks), 0)
    return pl.pallas_call(
        _proj_kernel,
        grid=(m // tm, P_COLS // PROJ_TN),
        in_specs=[
            pl.BlockSpec((tm, D_MODEL), lambda i, n: (i, 0)),
            pl.BlockSpec((1, D_MODEL), lambda i, n: (0, 0)),
            pl.BlockSpec((tm, D_MODEL), mod_map),
            pl.BlockSpec((tm, D_MODEL), mod_map),
            pl.BlockSpec((None, D_MODEL, PROJ_TN), lambda i, n: (n, 0, 0)),
            pl.BlockSpec((1, PROJ_TN), lambda i, n: (0, n)),
            pl.BlockSpec((1, PROJ_TN), lambda i, n: (0, n)),
            pl.BlockSpec((PROJ_TN, PROJ_TN), lambda i, n: (0, 0)),
        ],
        out_specs=[
            pl.BlockSpec((tm, PROJ_TN), lambda i, n: (i, n)),
            pl.BlockSpec((tm, PROJ_TN), lambda i, n: (i, n)),
        ],
        out_shape=[jax.ShapeDtypeStruct((m, P_COLS), F32), jax.ShapeDtypeStruct((m, P_COLS), BF16)],
        scratch_shapes=[pltpu.VMEM((tm, D_MODEL), BF16)],
        compiler_params=_cparams(2),
    )(x, gain, sc, sh, w_r, gq, bq, bd)


CUM_TILE = 256


def _cumsum_kernel(x_ref, tri_ref, o_ref, carry):
    @pl.when(pl.program_id(0) == 0)
    def _():
        carry[...] = jnp.zeros_like(carry)

    x1, x2, x3 = _split3(x_ref[...])
    tri = tri_ref[...]
    c = _dot(tri, x1) + _dot(tri, x2) + _dot(tri, x3) + carry[...]
    o_ref[...] = c
    carry[...] = c[CUM_TILE - 1:CUM_TILE, :]


def _cumsum_rows(p32, s, tri):
    return pl.pallas_call(
        _cumsum_kernel,
        grid=(s // CUM_TILE,),
        in_specs=[
            pl.BlockSpec((CUM_TILE, LANES), lambda i: (i, OFF_MISC // LANES)),
            pl.BlockSpec((CUM_TILE, CUM_TILE), lambda i: (0, 0)),
        ],
        out_specs=pl.BlockSpec((CUM_TILE, LANES), lambda i: (i, 0)),
        out_shape=jax.ShapeDtypeStruct((s, LANES), F32),
        scratch_shapes=[pltpu.VMEM((1, LANES), F32)],
        compiler_params=_cparams(1),
    )(p32, tri)


QB_A = 128
A_KEY_BLOCKS = 5


def _band_kernel(q_ref, *refs):
    k_refs = refs[:A_KEY_BLOCKS]
    v_refs = refs[A_KEY_BLOCKS:2 * A_KEY_BLOCKS]
    bias_ref, o_ref = refs[2 * A_KEY_BLOCKS:]
    i = pl.program_id(0)
    k_all = jnp.concatenate([r[...] for r in k_refs], axis=0)
    v_all = jnp.concatenate([r[...] for r in v_refs], axis=0)
    key_block = lax.broadcasted_iota(I32, (QB_A, A_KEY_BLOCKS * QB_A), 1) // QB_A
    in_range = i - (A_KEY_BLOCKS - 1) + key_block >= 0
    for h in range(H_A):
        hs = slice(h * HEAD_DIM, (h + 1) * HEAD_DIM)
        s = _dot_t(q_ref[:, hs], k_all[:, hs]) * ATTN_SCALE + bias_ref[h]
        s = jnp.where(in_range, s, NEG_INF)
        o_ref[:, hs] = _softmax_pv(s, v_all[:, hs]).astype(BF16)


def _band_prompt(pb, s, bias_a):
    nq = s // QB_A
    back = A_KEY_BLOCKS - 1

    def kv_spec(j, col):
        return pl.BlockSpec((QB_A, D_A), lambda i: (jnp.maximum(i - back + j, 0), col))

    in_specs = [pl.BlockSpec((QB_A, D_A), lambda i: (i, OFF_QA // D_A))]
    in_specs += [kv_spec(j, OFF_KA // D_A) for j in range(A_KEY_BLOCKS)]
    in_specs += [kv_spec(j, OFF_VA // D_A) for j in range(A_KEY_BLOCKS)]
    in_specs += [pl.BlockSpec((H_A, QB_A, A_KEY_BLOCKS * QB_A), lambda i: (0, 0, 0))]
    return pl.pallas_call(
        _band_kernel,
        grid=(nq,),
        in_specs=in_specs,
        out_specs=pl.BlockSpec((QB_A, D_A), lambda i: (i, 0)),
        out_shape=jax.ShapeDtypeStruct((s, D_A), BF16),
        compiler_params=_cparams(1),
    )(pb, *([pb] * (2 * A_KEY_BLOCKS)), bias_a)


def _kth_largest(count_ge, rows, k):
    def body(carry):
        it, lo, cnt_lo = carry
        cand = lo + lax.shift_left(jnp.int32(1), jnp.int32(31) - it)
        c = count_ge(cand)
        keep = c >= float(k)
        return it + 1, jnp.where(keep, cand, lo), jnp.where(keep, c, cnt_lo)

    def unsettled(carry):
        it, _, cnt_lo = carry
        return (it < 32) & (jnp.max(jnp.abs(cnt_lo - float(k))) > 0.0)

    lo0 = jnp.full((rows, 1), -(1 << 31), I32)
    _, lo, cnt = lax.while_loop(unsettled, body, (jnp.int32(0), lo0, count_ge(lo0)))
    return lo, cnt


def _tie_limit(count_eq_below, need, rows):
    def body(it, j):
        cand = j + lax.shift_left(jnp.int32(1), jnp.int32(14) - it)
        return jnp.where(count_eq_below(cand) <= need, cand, j)

    return lax.fori_loop(0, 15, body, jnp.zeros((rows, 1), I32))


QB_B = 128
KB_B = 256


def _dsa_kernel(bnd_ref, qb_ref, qi_ref, misc_ref, kidx_ref, kb_ref, vb_ref, tb_ref, o_ref,
                key_scr, w_scr, thr_scr, m_scr, acc_scr, *, k_top):
    i = pl.program_id(0)
    n_kb = (i * QB_B) // KB_B + 1
    row = lax.broadcasted_iota(I32, (QB_B, KB_B), 0)
    col = lax.broadcasted_iota(I32, (QB_B, KB_B), 1)

    wi = misc_ref[:, MISC_WI:MISC_WI + H_IDX] * (H_IDX ** -0.5 * D_IDX ** -0.5)
    for h in range(H_IDX):
        w_scr[h] = jnp.broadcast_to(wi[:, h:h + 1], (QB_B, KB_B))

    def score_block(j, _):
        kblk = kidx_ref[pl.ds(pl.multiple_of(j * KB_B, KB_B), KB_B), :][:, :D_IDX]
        acc = jnp.zeros((QB_B, KB_B), F32)
        for h in range(H_IDX):
            isc = _dot_t(qi_ref[:, h * D_IDX:(h + 1) * D_IDX], kblk)
            acc = acc + jnp.maximum(isc, 0.0) * w_scr[h]
        admissible = (j * KB_B + col) // CHUNK <= (i * QB_B + row) // CHUNK
        key_scr[:, pl.ds(pl.multiple_of(j * KB_B, KB_B), KB_B)] = jnp.where(
            admissible, _sortable(acc), jnp.int32(NEG_KEY))
        return 0

    lax.fori_loop(0, n_kb, score_block, 0)

    def count_where(pred):
        def body(j, c):
            blk = key_scr[:, pl.ds(pl.multiple_of(j * KB_B, KB_B), KB_B)]
            return c + jnp.where(pred(blk, j * KB_B + col), 1.0, 0.0)
        c = lax.fori_loop(0, n_kb, body, jnp.zeros((QB_B, KB_B), F32))
        return jnp.sum(c, axis=1, keepdims=True)

    thr, cnt_ge = _kth_largest(lambda cand: count_where(lambda blk, idx: blk >= cand), QB_B, k_top)
    live = thr > jnp.int32(NEG_KEY)
    thr = jnp.maximum(thr, jnp.int32(NEG_KEY))
    thr_scr[0] = jnp.broadcast_to(thr, (QB_B, KB_B))
    thr_scr[1] = jnp.broadcast_to(jnp.where(live, jnp.int32(IDX_BIG), 0), (QB_B, KB_B))
    has_ties = jnp.max(jnp.where(live & (cnt_ge > float(k_top)), 1.0, 0.0)) > 0.0

    @pl.when(has_ties)
    def _():
        cnt_gt = count_where(lambda blk, idx: blk > thr)
        need = float(k_top) - cnt_gt
        jstar = _tie_limit(lambda cand: count_where(lambda blk, idx: (blk == thr) & (idx < cand)), need, QB_B)
        thr_scr[1] = jnp.broadcast_to(jnp.where(live, jstar, 0), (QB_B, KB_B))

    def mask_block(j, _):
        start = pl.multiple_of(j * KB_B, KB_B)
        keys = key_scr[:, pl.ds(start, KB_B)]
        thr_b = thr_scr[0]
        sel = (keys > thr_b) | ((keys == thr_b) & (j * KB_B + col < thr_scr[1]))
        key_scr[:, pl.ds(start, KB_B)] = lax.bitcast_convert_type(jnp.where(sel, 0.0, NEG_INF), I32)
        return 0

    lax.fori_loop(0, n_kb, mask_block, 0)

    odd = i % 2
    table_a = jnp.where(odd == 1, 3, 1)
    table_b = jnp.where(odd == 1, 0, 2)
    n_far = jnp.maximum(n_kb - 2, 0)
    ones = jnp.ones((KB_B, HEAD_DIM), BF16)
    groups = range(H_B_KV)
    heads = [[n * GQA + g for g in range(GQA)] for n in groups]
    q3 = [jnp.concatenate([qb_ref[:, h * HEAD_DIM:(h + 1) * HEAD_DIM] for h in heads[n]], axis=0) * ATTN_SCALE
          for n in groups]

    def scores(n, j, table):
        start = pl.multiple_of(j * KB_B, KB_B)
        negm = lax.bitcast_convert_type(key_scr[:, pl.ds(start, KB_B)], F32)
        k_n = kb_ref[pl.ds(start, KB_B), n * HEAD_DIM:(n + 1) * HEAD_DIM]
        s = _dot_t(q3[n], k_n).reshape(GQA, QB_B, KB_B) + negm[None]
        if table is not None:
            s = s + jnp.stack([tb_ref[table * H_B + h] for h in heads[n]])
        return s

    def over_blocks(block_fn):
        lax.fori_loop(0, n_far, lambda j, c: (block_fn(j, None), c)[1], 0)
        pl.when(n_kb >= 2)(lambda: block_fn(n_kb - 2, table_a))
        block_fn(n_kb - 1, table_b)

    def exp_pv_pass(m_b):
        def pv_block(j, table):
            start = pl.multiple_of(j * KB_B, KB_B)
            for n in groups:
                p = jnp.exp(scores(n, j, table) - m_b[n]).astype(BF16).reshape(GQA * QB_B, KB_B)
                v_n = vb_ref[pl.ds(start, KB_B), n * HEAD_DIM:(n + 1) * HEAD_DIM]
                acc_scr[n] += _dot(p, jnp.concatenate([v_n, ones], axis=1))

        acc_scr[...] = jnp.zeros(acc_scr.shape, F32)
        over_blocks(pv_block)

    def spread(m):
        return jnp.broadcast_to(m, (GQA, QB_B, KB_B))

    def bound(n):
        qn = jnp.sqrt(jnp.sum(jnp.square(q3[n].astype(F32)), axis=1, keepdims=True)) * bnd_ref[n]
        return spread(jnp.stack([qn[g * QB_B:(g + 1) * QB_B] + bnd_ref[H_B_KV + h] for g, h in enumerate(heads[n])]))

    exp_pv_pass([bound(n) for n in groups])
    smallest = functools.reduce(jnp.minimum, [jnp.min(acc_scr[n][:, HEAD_DIM:HEAD_DIM + 1]) for n in groups])

    @pl.when(jnp.logical_not(smallest >= SOFTMAX_SUM_FLOOR))
    def _():
        def max_block(j, table):
            for n in groups:
                s = scores(n, j, table)
                m_scr[n] = jnp.maximum(m_scr[n], jnp.maximum(s[:, :, :LANES], s[:, :, LANES:]))

        m_scr[...] = jnp.full(m_scr.shape, -1e30, F32)
        over_blocks(max_block)
        exp_pv_pass([spread(jnp.max(m_scr[n], axis=2, keepdims=True)) for n in groups])

    for n in groups:
        for g, h in enumerate(heads[n]):
            a = acc_scr[n, g * QB_B:(g + 1) * QB_B, :]
            o_ref[:, h * HEAD_DIM:(h + 1) * HEAD_DIM] = (a[:, :HEAD_DIM] / a[:, HEAD_DIM:]).astype(BF16)


def _key_norm_max(pb, s, off, heads):
    k = pb[:s, off:off + heads * HEAD_DIM].astype(F32).reshape(s, heads, HEAD_DIM)
    return jnp.sqrt(jnp.max(jnp.sum(k * k, axis=2), axis=0)) * BOUND_SLACK


def _dsa_prompt(p32, pb, s, t5_tables):
    nq = s // QB_B
    k_top = min(TOPK_MAX, s // 4)
    once = pl.Buffered(1)
    bias_max = jnp.maximum(jnp.max(t5_tables.reshape(4, H_B, -1), axis=(0, 2)), 0.0)
    bounds = jnp.concatenate([_key_norm_max(pb, s, OFF_KB, H_B_KV), bias_max])
    return pl.pallas_call(
        functools.partial(_dsa_kernel, k_top=k_top),
        grid=(nq,),
        in_specs=[
            pl.BlockSpec(memory_space=pltpu.SMEM),
            pl.BlockSpec((QB_B, D_B), lambda i: (i, OFF_QB // D_B)),
            pl.BlockSpec((QB_B, H_IDX * D_IDX), lambda i: (i, OFF_QI // (H_IDX * D_IDX))),
            pl.BlockSpec((QB_B, LANES), lambda i: (i, OFF_MISC // LANES)),
            pl.BlockSpec((s, LANES), lambda i: (0, OFF_MISC // LANES), pipeline_mode=once),
            pl.BlockSpec((s, D_B_KV), lambda i: (0, OFF_KB // D_B_KV), pipeline_mode=once),
            pl.BlockSpec((s, D_B_KV), lambda i: (0, OFF_VB // D_B_KV), pipeline_mode=once),
            pl.BlockSpec((4 * H_B, QB_B, KB_B), lambda i: (0, 0, 0), pipeline_mode=once),
        ],
        out_specs=pl.BlockSpec((QB_B, D_B), lambda i: (i, 0)),
        out_shape=jax.ShapeDtypeStruct((s, D_B), BF16),
        scratch_shapes=[
            pltpu.VMEM((QB_B, s), I32),
            pltpu.VMEM((H_IDX, QB_B, KB_B), F32),
            pltpu.VMEM((2, QB_B, KB_B), I32),
            pltpu.VMEM((H_B_KV, GQA, QB_B, LANES), F32),
            pltpu.VMEM((H_B_KV, GQA * QB_B, LANES), F32),
        ],
        compiler_params=_cparams(1),
    )(bounds, pb, pb, p32, pb, pb, pb, t5_tables)


QB_C = 512
HEADS_PER_STEP_C = LANES // HEAD_DIM


PREP_TILE_C = 512


def _fox_prep_kernel(q_ref, k_ref, v_ref, cum_ref, qa_ref, ka_ref, va_ref):
    rows = q_ref.shape[0]
    lane = lax.broadcasted_iota(I32, (rows, HEAD_DIM), 1)
    ones = jnp.ones((rows, HEAD_DIM), BF16)
    for h in range(H_C):
        hs = slice(h * HEAD_DIM, (h + 1) * HEAD_DIM)
        c1, c2, c3 = [c.astype(F32) for c in _split3(cum_ref[:, h:h + 1])]
        unit = jnp.where(lane < 6, 1.0, 0.0)
        q_extra = jnp.where(lane < 3, jnp.where(lane == 0, c1, jnp.where(lane == 1, c2, c3)), unit)
        k_extra = jnp.where(lane < 3, unit, -jnp.where(lane == 3, c1, jnp.where(lane == 4, c2, c3)) * unit)
        qa_ref[h] = jnp.concatenate([q_ref[:, hs] * ATTN_SCALE, q_extra.astype(BF16)], axis=1)
        ka_ref[h] = jnp.concatenate([k_ref[:, hs], k_extra.astype(BF16)], axis=1)
        va_ref[h] = jnp.concatenate([v_ref[:, hs], ones], axis=1)


def _fox_prepare(pb, s, cum):
    tm = PREP_TILE_C
    row = lambda off: pl.BlockSpec((tm, D_C), lambda i: (i, off // D_C))
    out = pl.BlockSpec((H_C, tm, LANES), lambda i: (0, i, 0))
    return pl.pallas_call(
        _fox_prep_kernel,
        grid=(s // tm,),
        in_specs=[row(OFF_QC), row(OFF_KC), row(OFF_VC), pl.BlockSpec((tm, 16), lambda i: (i, 0))],
        out_specs=[out, out, out],
        out_shape=[jax.ShapeDtypeStruct((H_C, s, LANES), BF16)] * 3,
        compiler_params=_cparams(1),
    )(pb, pb, pb, cum)


def _fox_kernel(kmax_ref, q_ref, k_ref, v_ref, o_ref):
    g = pl.program_id(0)
    i = pl.program_id(1)
    row = lax.broadcasted_iota(I32, (QB_C, QB_C), 0)
    col = lax.broadcasted_iota(I32, (QB_C, QB_C), 1)
    heads = range(HEADS_PER_STEP_C)
    q = [q_ref[hh] for hh in heads]

    def scores(hh, j):
        return _dot_t(q[hh], k_ref[hh, pl.ds(pl.multiple_of(j * QB_C, QB_C), QB_C), :])

    def halves_max(s):
        return functools.reduce(jnp.maximum, [s[:, c:c + LANES] for c in range(0, QB_C, LANES)])

    def spread(m):
        return jnp.broadcast_to(m, (QB_C, QB_C))

    diag = [jnp.where(col <= row, scores(hh, i), NEG_INF) for hh in heads]

    def exp_pv_pass(m_b):
        def pv(hh, s, j):
            p = jnp.exp(s - m_b[hh]).astype(BF16)
            return _dot(p, v_ref[hh, pl.ds(pl.multiple_of(j * QB_C, QB_C), QB_C), :])

        def pv_body(j, accs):
            return tuple(accs[hh] + pv(hh, scores(hh, j), j) for hh in heads)

        return lax.fori_loop(0, i, pv_body, tuple(pv(hh, diag[hh], i) for hh in heads))

    lane = lax.broadcasted_iota(I32, (QB_C, LANES), 1)

    def bound(hh):
        qf = jnp.where(lane < HEAD_DIM, q[hh].astype(F32), 0.0)
        return spread(jnp.sqrt(jnp.sum(qf * qf, axis=1, keepdims=True)) * kmax_ref[g * HEADS_PER_STEP_C + hh])

    accs = exp_pv_pass([bound(hh) for hh in heads])
    smallest = functools.reduce(jnp.minimum, [jnp.min(accs[hh][:, HEAD_DIM:HEAD_DIM + 1]) for hh in heads])

    def exact():
        def max_body(j, ms):
            return tuple(jnp.maximum(ms[hh], halves_max(scores(hh, j))) for hh in heads)

        ms = lax.fori_loop(0, i, max_body, tuple(halves_max(diag[hh]) for hh in heads))
        return exp_pv_pass([spread(jnp.max(ms[hh], axis=1, keepdims=True)) for hh in heads])

    accs = lax.cond(smallest >= SOFTMAX_SUM_FLOOR, lambda: accs, exact)
    o_ref[...] = jnp.concatenate(
        [accs[hh][:, :HEAD_DIM] / accs[hh][:, HEAD_DIM:] for hh in heads], axis=1).astype(BF16)


def _fox_prompt(qa, ka, va, kmax, s):
    nq = s // QB_C
    ng = H_C // HEADS_PER_STEP_C
    hp = HEADS_PER_STEP_C
    return pl.pallas_call(
        _fox_kernel,
        grid=(ng, nq),
        in_specs=[
            pl.BlockSpec(memory_space=pltpu.SMEM),
            pl.BlockSpec((hp, QB_C, LANES), lambda g, i: (g, i, 0)),
            pl.BlockSpec((hp, s, LANES), lambda g, i: (g, 0, 0)),
            pl.BlockSpec((hp, s, LANES), lambda g, i: (g, 0, 0)),
        ],
        out_specs=pl.BlockSpec((QB_C, LANES), lambda g, i: (i, g)),
        out_shape=jax.ShapeDtypeStruct((s, D_C), BF16),
        compiler_params=_cparams(2),
    )(kmax, qa, ka, va)


def _softmax_pv(s, v_all):
    m = jnp.max(s, axis=1, keepdims=True)
    p = jnp.exp(s - m)
    l = jnp.sum(p, axis=1, keepdims=True)
    return _dot(p.astype(BF16), v_all) / l


def _with_new_rows(cache, new, pad_rows):
    parts = [cache.astype(BF16), new]
    if pad_rows:
        parts.append(jnp.zeros((pad_rows, new.shape[1]), BF16))
    return jnp.concatenate(parts, axis=0)


def _sample_kernel(qa_ref, ka_ref, va_ref, kvb_ref, qi_ref, qb_ref, qc_ref, kc_ref, vc_ref, misc16_ref,
                   misc32_ref, ca_ref, cb_ref, cbi_ref, cc_ref, clf_ref, clft_ref, lfn_ref, lfnt_ref,
                   bias_a_ref, t5_ref, triu_ref, tril_ref,
                   oa_ref, ob_ref, oc_ref, *, t, past, na, k_top):
    la = na + LANES
    lk = past + LANES
    pad = LANES - t

    ca = ca_ref[...]
    ka_all = _with_new_rows(ca[:, :D_A], ka_ref[...], pad)
    va_all = _with_new_rows(ca[:, D_A:], va_ref[...], pad)
    for h in range(H_A):
        hs = slice(h * HEAD_DIM, (h + 1) * HEAD_DIM)
        s = _dot_t(qa_ref[:, hs], ka_all[:, hs]) * ATTN_SCALE + bias_a_ref[h]
        oa_ref[:, hs] = _softmax_pv(s, va_all[:, hs]).astype(BF16)

    col = lax.broadcasted_iota(I32, (t, lk), 1)
    ki_all = _with_new_rows(cbi_ref[...], misc16_ref[:, :D_IDX], pad)
    wi = misc32_ref[:, MISC_WI:MISC_WI + H_IDX] * (H_IDX ** -0.5 * D_IDX ** -0.5)
    acc = jnp.zeros((t, lk), F32)
    for h in range(H_IDX):
        isc = _dot_t(qi_ref[:, h * D_IDX:(h + 1) * D_IDX], ki_all)
        acc = acc + jnp.maximum(isc, 0.0) * wi[:, h:h + 1]
    keys = jnp.where(col < past + t, _sortable(acc), jnp.int32(NEG_KEY))

    def count(pred):
        return jnp.sum(jnp.where(pred, 1.0, 0.0), axis=1, keepdims=True)

    thr, _ = _kth_largest(lambda cand: count(keys >= cand), t, k_top)
    live = thr > jnp.int32(NEG_KEY)
    thr = jnp.maximum(thr, jnp.int32(NEG_KEY))
    need = float(k_top) - count(keys > thr)
    jstar = _tie_limit(lambda cand: count((keys == thr) & (col < cand)), need, t)
    sel = (keys > thr) | ((keys == thr) & (col < jnp.where(live, jstar, 0)))

    cb = cb_ref[...]
    kb_all = _with_new_rows(cb[:, :D_B_KV], kvb_ref[:, :D_B_KV], pad)
    vb_all = _with_new_rows(cb[:, D_B_KV:], kvb_ref[:, D_B_KV:], pad)
    for n in range(H_B_KV):
        ns = slice(n * HEAD_DIM, (n + 1) * HEAD_DIM)
        heads = [n * GQA + g for g in range(GQA)]
        q3 = jnp.concatenate([qb_ref[:, h * HEAD_DIM:(h + 1) * HEAD_DIM] for h in heads], axis=0)
        s = (_dot_t(q3, kb_all[:, ns]) * ATTN_SCALE).reshape(GQA, t, lk) + jnp.stack([t5_ref[h] for h in heads])
        s = jnp.where(sel[None], s, NEG_INF).reshape(GQA * t, lk)
        o3 = _softmax_pv(s, vb_all[:, ns]).astype(BF16)
        for g, h in enumerate(heads):
            ob_ref[:, h * HEAD_DIM:(h + 1) * HEAD_DIM] = o3[g * t:(g + 1) * t, :]

    cc = cc_ref[...]
    kc_all = _with_new_rows(cc[:, :D_C], kc_ref[...], pad)
    vc_all = _with_new_rows(cc[:, D_C:], vc_ref[...], pad)
    lft = jnp.concatenate([clft_ref[...], lfnt_ref[...]], axis=1)
    t1, t2, t3 = _split3(lft)
    triu = triu_ref[...]
    cum_t = _dot(t1, triu) + _dot(t2, triu) + _dot(t3, triu)
    total = jnp.sum(clf_ref[...], axis=0, keepdims=True)
    n1, n2, n3 = _split3(lfn_ref[...])
    tril = tril_ref[...]
    cum_q = total + _dot(tril, n1) + _dot(tril, n2) + _dot(tril, n3)
    row = lax.broadcasted_iota(I32, (t, lk), 0)
    causal = col <= past + row
    for h in range(H_C):
        hs = slice(h * HEAD_DIM, (h + 1) * HEAD_DIM)
        s = _dot_t(qc_ref[:, hs], kc_all[:, hs]) * ATTN_SCALE + cum_q[:, h:h + 1] - cum_t[h:h + 1, :]
        s = jnp.where(causal, s, NEG_INF)
        oc_ref[:, hs] = _softmax_pv(s, vc_all[:, hs]).astype(BF16)


def _sample_mixers(p32, pb, s, ca, cb, cbi, cc, clf, clft, lfn, lfnt, bias_a, t5_tab, triu, tril, layer):
    nb, na = ca.shape[1], ca.shape[2]
    past = cb.shape[2]
    t = lfn.shape[1]
    k_top = min(TOPK_MAX, (past + t) // 4)
    r0 = s // t
    lk = past + LANES
    la = na + LANES
    row = lambda width, off: pl.BlockSpec((t, width), lambda b: (r0 + b, off // width))
    cache = lambda rows, width: pl.BlockSpec((None, None, rows, width), lambda b: (layer, b, 0, 0))
    full = lambda shape: pl.BlockSpec(shape, lambda b: (0,) * len(shape))
    in_specs = [
        row(D_A, OFF_QA), row(D_A, OFF_KA), row(D_A, OFF_VA), row(2 * D_B_KV, OFF_KB),
        row(H_IDX * D_IDX, OFF_QI), row(D_B, OFF_QB), row(D_C, OFF_QC), row(D_C, OFF_KC), row(D_C, OFF_VC),
        row(LANES, OFF_MISC), row(LANES, OFF_MISC),
        cache(na, 2 * D_A), cache(past, 2 * D_B_KV), cache(past, D_IDX), cache(past, 2 * D_C),
        cache(past, 16), cache(16, past),
        pl.BlockSpec((None, t, 16), lambda b: (b, 0, 0)),
        pl.BlockSpec((None, 16, LANES), lambda b: (b, 0, 0)),
        full((H_A, t, la)), full((H_B, t, lk)), full((lk, lk)), full((t, t)),
    ]
    out = lambda width: pl.BlockSpec((t, width), lambda b: (b, 0))
    return pl.pallas_call(
        functools.partial(_sample_kernel, t=t, past=past, na=na, k_top=k_top),
        grid=(nb,),
        in_specs=in_specs,
        out_specs=[out(D_A), out(D_B), out(D_C)],
        out_shape=[jax.ShapeDtypeStruct((nb * t, w), BF16) for w in (D_A, D_B, D_C)],
        compiler_params=_cparams(1),
    )(pb, pb, pb, pb, pb, pb, pb, pb, pb, pb, p32, ca, cb, cbi, cc, clf, clft, lfn, lfnt,
      bias_a, t5_tab, triu, tril)


def _outproj_kernel(x_ref, g_ref, ap, bp, cp, as_, bs, cs, w_ref, o_ref, *, n_prompt_blocks):
    i = pl.program_id(0)

    def run(a, b, c):
        y = (_dot(a[...], w_ref[:D_A, :]) + _dot(b[...], w_ref[D_A:D_A + D_B, :])
             + _dot(c[...], w_ref[D_A + D_B:, :]))
        o_ref[...] = x_ref[...] + g_ref[...] * y

    pl.when(i < n_prompt_blocks)(lambda: run(ap, bp, cp))
    pl.when(i >= n_prompt_blocks)(lambda: run(as_, bs, cs))


def _out_project(x, gate, mix_p, mix_s, w_out_b, layer, n_prompt_blocks):
    m = x.shape[0]
    tm = TOKEN_TILE
    last_p = n_prompt_blocks - 1
    p_map = lambda i: (jnp.minimum(i, last_p), 0)
    s_map = lambda i: (jnp.maximum(i - n_prompt_blocks, 0), 0)
    widths = (D_A, D_B, D_C)
    return pl.pallas_call(
        functools.partial(_outproj_kernel, n_prompt_blocks=n_prompt_blocks),
        grid=(m // tm,),
        in_specs=[
            pl.BlockSpec((tm, D_MODEL), lambda i: (i, 0)),
            pl.BlockSpec((tm, D_MODEL), lambda i: (_mod_block_index(i, n_prompt_blocks), 0)),
            *[pl.BlockSpec((tm, w), p_map) for w in widths],
            *[pl.BlockSpec((tm, w), s_map) for w in widths],
            pl.BlockSpec((None, D_MODEL, D_MODEL), lambda i: (layer, 0, 0)),
        ],
        out_specs=pl.BlockSpec((tm, D_MODEL), lambda i: (i, 0)),
        out_shape=jax.ShapeDtypeStruct((m, D_MODEL), F32),
        compiler_params=_cparams(1),
    )(x, gate, *mix_p, *mix_s, w_out_b)


def _lane_pick(vals, lane, idx):
    return jnp.sum(jnp.where(lane == idx, vals, 0.0), axis=1, keepdims=True)


def _first_argmax(vals, lane):
    m = jnp.max(vals, axis=1, keepdims=True)
    idx = jnp.min(jnp.where(vals == m, lane, float(LANES)), axis=1, keepdims=True)
    return m, idx


def _router_kernel(x_ref, gain_ref, sc_ref, sh_ref, wr_ref, br_ref, tri_ref, h_ref, meta_ref, cnt_ref, carry):
    @pl.when(pl.program_id(0) == 0)
    def _():
        carry[...] = jnp.zeros_like(carry)

    h = _norm_mod(x_ref[...], gain_ref[...], sc_ref[...], sh_ref[...])
    hb = h.astype(BF16)
    h_ref[...] = h
    scores = jax.nn.sigmoid(_dot(hb, wr_ref[...]))
    lane_i = lax.broadcasted_iota(I32, scores.shape, 1)
    lane = lane_i.astype(F32)
    sel = jnp.where(lane_i < N_EXPERTS, scores + br_ref[...], NEG_INF)
    group = (lane_i // (N_EXPERTS // N_GROUPS)).astype(F32)

    best = None
    for g in range(N_GROUPS):
        in_g = jnp.where(group == float(g), sel, NEG_INF)
        m1, i1 = _first_argmax(in_g, lane)
        m2 = jnp.max(jnp.where(lane == i1, NEG_INF, in_g), axis=1, keepdims=True)
        gs = m1 + m2
        if best is None:
            best, gbest = gs, jnp.zeros_like(i1)
        else:
            better = gs > best
            best = jnp.where(better, gs, best)
            gbest = jnp.where(better, float(g), gbest)

    in_best = jnp.where(group == gbest, sel, NEG_INF)
    _, e0 = _first_argmax(in_best, lane)
    _, e1 = _first_argmax(jnp.where(lane == e0, NEG_INF, in_best), lane)
    w0 = _lane_pick(scores, lane, e0)
    w1 = _lane_pick(scores, lane, e1)
    wsum = w0 + w1

    onehot = jnp.where((lane == e0) | (lane == e1), 1.0, 0.0)
    before = _dot(tri_ref[...], onehot.astype(BF16)) + carry[...]
    r0 = _lane_pick(before, lane, e0)
    r1 = _lane_pick(before, lane, e1)
    carry[...] = carry[...] + jnp.sum(onehot, axis=0, keepdims=True)
    cnt_ref[...] = carry[...]

    meta = jnp.zeros(scores.shape, F32)
    for k, v in enumerate((e0, e1, r0, r1, w0 / wsum, w1 / wsum)):
        meta = jnp.where(lane_i == k, v, meta)
    meta_ref[...] = meta


def _router(x, gain, sc, sh, w_router_p, b_router_p, tri, n_prompt_blocks):
    m = x.shape[0]
    tm = TOKEN_TILE
    mod_map = lambda i: (_mod_block_index(i, n_prompt_blocks), 0)
    return pl.pallas_call(
        _router_kernel,
        grid=(m // tm,),
        in_specs=[
            pl.BlockSpec((tm, D_MODEL), lambda i: (i, 0)),
            pl.BlockSpec((1, D_MODEL), lambda i: (0, 0)),
            pl.BlockSpec((tm, D_MODEL), mod_map),
            pl.BlockSpec((tm, D_MODEL), mod_map),
            pl.BlockSpec((D_MODEL, LANES), lambda i: (0, 0)),
            pl.BlockSpec((1, LANES), lambda i: (0, 0)),
            pl.BlockSpec((tm, tm), lambda i: (0, 0)),
        ],
        out_specs=[
            pl.BlockSpec((tm, D_MODEL), lambda i: (i, 0)),
            pl.BlockSpec((tm, LANES), lambda i: (i, 0)),
            pl.BlockSpec((1, LANES), lambda i: (0, 0)),
        ],
        out_shape=[
            jax.ShapeDtypeStruct((m, D_MODEL), F32),
            jax.ShapeDtypeStruct((m, LANES), F32),
            jax.ShapeDtypeStruct((1, LANES), F32),
        ],
        scratch_shapes=[pltpu.VMEM((1, LANES), F32)],
        compiler_params=_cparams(1),
    )(x, gain, sc, sh, w_router_p, b_router_p, tri)


EXPERT_ROWS = 384
EXPERT_TF = 512


def _row_copy(src_hbm, src_row, dst_buf, dst_row, sem):
    return pltpu.make_async_copy(src_hbm.at[pl.ds(src_row, 1), :], dst_buf.at[pl.ds(dst_row, 1), :], sem)


def _expert_kernel(be_ref, nu_ref, st_ref, h_hbm, wg_ref, wu_ref, wd_ref, o_ref, xrows, x16, sems):
    b = pl.program_id(0)
    f = pl.program_id(1)
    n_used = nu_ref[0]
    rows_per_step = EXPERT_ROWS // (D_EXPERT // EXPERT_TF)

    def start_rows(blk, first, count):
        for r in range(count):
            tok = st_ref[blk * EXPERT_ROWS + first + r]
            _row_copy(h_hbm, tok, xrows.at[blk % 2], first + r, sems.at[blk % 2]).start()

    @pl.when((b == 0) & (f == 0))
    def _():
        start_rows(0, 0, EXPERT_ROWS)

    @pl.when((f == 0) & (b < n_used))
    def _():
        for r in range(EXPERT_ROWS):
            _row_copy(h_hbm, 0, xrows.at[b % 2], r, sems.at[b % 2]).wait()
        x16[...] = xrows[b % 2].astype(BF16)

    @pl.when(b + 1 < n_used)
    def _():
        start_rows(b + 1, f * rows_per_step, rows_per_step)

    @pl.when((f == 0) & (b >= n_used))
    def _():
        o_ref[...] = jnp.zeros(o_ref.shape, F32)

    @pl.when(b < n_used)
    def _():
        x = x16[...]
        a = _dot(x, wg_ref[...].astype(BF16))
        u = _dot(x, wu_ref[...].astype(BF16))
        act = (a * jax.nn.sigmoid(a) * u).astype(BF16)
        y = _dot(act, wd_ref[...].astype(BF16))

        @pl.when(f == 0)
        def _():
            o_ref[...] = y

        @pl.when(f > 0)
        def _():
            o_ref[...] += y


def _experts(h2, slot_tok, block_e, n_used, w_g, w_u, w_d, layer):
    nslots = slot_tok.shape[0]
    nb = nslots // EXPERT_ROWS
    nf = D_EXPERT // EXPERT_TF

    def blk(b, nu):
        return jnp.minimum(b, nu[0] - 1)

    def fidx(b, f, nu):
        return jnp.where(b < nu[0], f, nf - 1)

    grid_spec = pltpu.PrefetchScalarGridSpec(
        num_scalar_prefetch=3,
        grid=(nb, nf),
        in_specs=[
            pl.BlockSpec(memory_space=pl.ANY),
            pl.BlockSpec((None, None, D_MODEL, EXPERT_TF),
                         lambda b, f, be, nu, st: (layer, be[blk(b, nu)], 0, fidx(b, f, nu))),
            pl.BlockSpec((None, None, D_MODEL, EXPERT_TF),
                         lambda b, f, be, nu, st: (layer, be[blk(b, nu)], 0, fidx(b, f, nu))),
            pl.BlockSpec((None, None, EXPERT_TF, D_MODEL),
                         lambda b, f, be, nu, st: (layer, be[blk(b, nu)], fidx(b, f, nu), 0)),
        ],
        out_specs=pl.BlockSpec((EXPERT_ROWS, D_MODEL), lambda b, f, be, nu, st: (b, 0)),
        scratch_shapes=[
            pltpu.VMEM((2, EXPERT_ROWS, D_MODEL), F32),
            pltpu.VMEM((EXPERT_ROWS, D_MODEL), BF16),
            pltpu.SemaphoreType.DMA((2,)),
        ],
    )
    return pl.pallas_call(
        _expert_kernel,
        grid_spec=grid_spec,
        out_shape=jax.ShapeDtypeStruct((nslots, D_MODEL), F32),
        compiler_params=_cparams(2),
    )(block_e, n_used, slot_tok, h2, w_g, w_u, w_d)


COMBINE_TILE = 256


def _combine_kernel(d0_ref, d1_ref, x_ref, g_ref, meta_ref, y_hbm, o_ref, ybuf, sems):
    i = pl.program_id(0)
    n = pl.num_programs(0)

    def start_rows(blk):
        slot = blk % 2

        def body(r, _):
            t = blk * COMBINE_TILE + r
            _row_copy(y_hbm, d0_ref[t], ybuf.at[slot, 0], r, sems.at[slot]).start()
            _row_copy(y_hbm, d1_ref[t], ybuf.at[slot, 1], r, sems.at[slot]).start()
            return 0

        lax.fori_loop(0, COMBINE_TILE, body, 0, unroll=8)

    @pl.when(i == 0)
    def _():
        start_rows(i)

    slot = i % 2
    for k in range(2):
        for r in range(COMBINE_TILE):
            _row_copy(y_hbm, 0, ybuf.at[slot, k], r, sems.at[slot]).wait()

    @pl.when(i + 1 < n)
    def _():
        start_rows(i + 1)

    moe = ybuf[slot, 0] * meta_ref[:, 4:5] + ybuf[slot, 1] * meta_ref[:, 5:6]
    o_ref[...] = x_ref[...] + g_ref[...] * moe


def _combine(x, gate, meta, yb, dest0, dest1, n_prompt_blocks):
    m = x.shape[0]
    tm = COMBINE_TILE
    ratio = TOKEN_TILE // tm
    mod_map = lambda i, d0, d1: (jnp.where(i < n_prompt_blocks * ratio, 0, i - (n_prompt_blocks - 1) * ratio), 0)
    row = lambda width: pl.BlockSpec((tm, width), lambda i, d0, d1: (i, 0))
    grid_spec = pltpu.PrefetchScalarGridSpec(
        num_scalar_prefetch=2,
        grid=(m // tm,),
        in_specs=[row(D_MODEL), pl.BlockSpec((tm, D_MODEL), mod_map), row(LANES), pl.BlockSpec(memory_space=pl.ANY)],
        out_specs=row(D_MODEL),
        scratch_shapes=[pltpu.VMEM((2, 2, tm, D_MODEL), F32), pltpu.SemaphoreType.DMA((2,))],
    )
    return pl.pallas_call(
        _combine_kernel,
        grid_spec=grid_spec,
        out_shape=jax.ShapeDtypeStruct((m, D_MODEL), F32),
        compiler_params=_cparams(1),
    )(dest0, dest1, x, gate, meta, yb)


def _t5_bucket(rel):
    nb = T5_BUCKETS // 2
    max_exact = nb // 2
    ret = jnp.where(rel < 0, nb, 0)
    n = jnp.abs(rel)
    nf = jnp.maximum(n, 1).astype(F32)
    large = max_exact + (jnp.log(nf / max_exact) / math.log(T5_MAX_DIST / max_exact) * (nb - max_exact)).astype(I32)
    large = jnp.minimum(large, nb - 1)
    return ret + jnp.where(n < max_exact, n, large)


def _toeplitz(by_rel, q0, k0, nq, nk):
    d = jnp.arange(-(nk - 1), nq)
    rev = by_rel(q0 - k0 + d)[:, ::-1]
    h, length = rev.shape
    flat = jnp.tile(jnp.pad(rev, ((0, 0), (0, 1))), (1, nq))[:, :nq * length]
    return flat.reshape(h, nq, length)[:, :, nq - 1:nq - 1 + nk]


def _band_bias(table, q0, k0, nq, nk, n_valid):
    by_rel = lambda rel: table.astype(F32)[:, jnp.clip(rel, -A_REL_FUTURE, A_REL_PAST) + A_REL_FUTURE]
    qc = (q0 + jnp.arange(nq))[:, None] // CHUNK
    col = jnp.arange(nk)[None, :]
    kc = (k0 + col) // CHUNK
    ok = (col < n_valid) & (kc <= qc) & (kc >= qc - A_LEFT_CHUNKS)
    return jnp.where(ok[None], _toeplitz(by_rel, q0, k0, nq, nk), NEG_INF)


def _t5_bias(t5, q0, k0, nq, nk):
    return _toeplitz(lambda rel: t5.astype(F32)[_t5_bucket(rel)].T, q0, k0, nq, nk)


def _relayout_w_in(w_in_l):
    qa, ka, va, qb, kb, vb, qi, ki, wi, qc, kc, vc, fc = jnp.split(w_in_l, PROJ_SPLITS, axis=1)
    pad = jnp.zeros((D_MODEL, P_COLS - OFF_MISC - D_IDX - H_IDX - H_C), w_in_l.dtype)
    w = jnp.concatenate([qa, ka, va, kb, vb, qi, qb, qc, kc, vc, ki, wi, fc, pad], axis=1).astype(BF16)
    return jnp.swapaxes(w.reshape(D_MODEL, P_COLS // PROJ_TN, PROJ_TN), 0, 1)


def _column_params(qk_gain_l, b_forget_l):
    ones = lambda n: jnp.ones((n,), F32)
    g = qk_gain_l.astype(F32)
    gq = jnp.concatenate([
        jnp.tile(g[0], H_A), jnp.tile(g[1], H_A), ones(D_A),
        jnp.tile(g[3], H_B_KV), ones(D_B_KV), ones(H_IDX * D_IDX),
        jnp.tile(g[2], H_B), jnp.tile(g[4], H_C), jnp.tile(g[5], H_C), ones(D_C), ones(P_COLS - OFF_MISC)])
    bq = jnp.zeros((P_COLS,), F32).at[OFF_MISC + MISC_FC:OFF_MISC + MISC_FC + H_C].set(b_forget_l.astype(F32))
    return gq[None, :], bq[None, :]


def _tri(n, *, strict=False, upper=False):
    r = jnp.arange(n)[:, None]
    c = jnp.arange(n)[None, :]
    m = (r < c if strict else r <= c) if upper else (c < r if strict else c <= r)
    return m.astype(BF16)


def kernel(x_prompt, x_sample, c_prompt, c_sample, cache_a_kv, cache_b_kv, cache_b_kidx, cache_c_kv, cache_c_logf,
           w_ada, b_ada, norm_gain, w_in, b_forget, qk_gain, rel_bias_a, t5_bias, w_out, w_router, b_router,
           w_e_gate, w_e_up, w_e_down):
    depth = w_in.shape[0]
    bp, s, d = x_prompt.shape
    nb, t, _ = x_sample.shape
    ns = nb * t
    tm = TOKEN_TILE
    assert bp == 1 and d == D_MODEL and s % tm == 0 and ns % tm == 0 and s % QB_C == 0
    n_pb = s // tm
    m = s + ns
    na = cache_a_kv.shape[2]
    past = cache_b_kv.shape[2]
    keep = min(A_LEFT_CHUNKS * CHUNK, s)

    x = jnp.concatenate([x_prompt.reshape(s, d), x_sample.reshape(ns, d)], axis=0)
    c_all = jnp.concatenate([c_prompt, c_sample, jnp.zeros((-(bp + nb) % 8, d), F32)], axis=0)

    t5_far = t5_bias.astype(F32)[T5_BUCKETS // 2 - 1]
    t5_tables = jnp.concatenate(
        [_t5_bias(t5_bias, off, 0, QB_B, KB_B) - t5_far[:, None, None] for off in (QB_B, 2 * QB_B, 0)]
        + [jnp.zeros((H_B, QB_B, KB_B), F32)], axis=0)
    t5_s = _t5_bias(t5_bias, past, 0, t, past + LANES)
    bd = ((jnp.arange(PROJ_TN)[:, None] // HEAD_DIM == jnp.arange(PROJ_TN)[None, :] // HEAD_DIM)
          .astype(F32) / HEAD_DIM).astype(BF16)
    tri_cum = _tri(CUM_TILE)
    tri_rank = _tri(tm, strict=True)
    triu_s = _tri(past + LANES, upper=True)
    tril_s = _tri(t)
    w_out_b = w_out.astype(BF16)
    w_router_p = jnp.pad(w_router, ((0, 0), (0, LANES - N_EXPERTS))).astype(BF16)
    b_router_p = jnp.pad(b_router.astype(F32), (0, LANES - N_EXPERTS))[None, :]

    ca = cache_a_kv.reshape(depth, nb, na, 2 * D_A)
    cb = cache_b_kv.reshape(depth, nb, past, 2 * D_B_KV)
    cc = cache_c_kv.reshape(depth, nb, past, 2 * D_C)
    clf = jnp.pad(cache_c_logf.astype(F32), ((0, 0), (0, 0), (0, 0), (0, 16 - H_C)))
    clft = jnp.swapaxes(clf, 2, 3)

    n_asg = 2 * m
    n_eb = -(-n_asg // EXPERT_ROWS) + N_EXPERTS
    tok = jnp.arange(m, dtype=I32)

    states_p, states_s = [], []
    for l in range(depth):
        mod = _ada(c_all, w_ada, b_ada, l)
        mods = []
        for part in jnp.split(mod, 6, axis=1):
            mods.append(jnp.concatenate([jnp.broadcast_to(part[:1], (tm, d)), jnp.repeat(part[bp:bp + nb], t, axis=0)], 0))
        sh1, sc1, g1, sh2, sc2, g2 = mods

        gq, bq = _column_params(qk_gain[l], b_forget[l])
        p32, pb = _project(x, norm_gain[l, 0][None, :], sc1, sh1, _relayout_w_in(w_in[l]), gq, bq, bd, n_pb)

        bias_a_p = _band_bias(rel_bias_a[l], (A_KEY_BLOCKS - 1) * QB_A, 0, QB_A, A_KEY_BLOCKS * QB_A,
                              A_KEY_BLOCKS * QB_A)
        oa_p = _band_prompt(pb, s, bias_a_p)
        ob_p = _dsa_prompt(p32, pb, s, t5_tables)
        cum = _cumsum_rows(p32, s, tri_cum)[:, MISC_FC:MISC_FC + 16]
        oc_p = _fox_prompt(*_fox_prepare(pb, s, cum), _key_norm_max(pb, s, OFF_KC, H_C), s)

        lfn = p32[s:, OFF_MISC + MISC_FC:OFF_MISC + MISC_FC + 16].reshape(nb, t, 16)
        lfnt = jnp.pad(jnp.swapaxes(lfn, 1, 2), ((0, 0), (0, 0), (0, LANES - t)))
        bias_a_s = _band_bias(rel_bias_a[l], past, past - na, t, na + LANES, na + t)
        oa_s, ob_s, oc_s = _sample_mixers(p32, pb, s, ca, cb, cache_b_kidx, cc, clf, clft, lfn, lfnt,
                                          bias_a_s, t5_s, triu_s, tril_s, l)

        x = _out_project(x, g1, (oa_p, ob_p, oc_p), (oa_s, ob_s, oc_s), w_out_b, l, n_pb)

        h2, meta, counts = _router(x, norm_gain[l, 1][None, :], sc2, sh2, w_router_p, b_router_p, tri_rank, n_pb)
        e0 = meta[:, 0].astype(I32)
        e1 = meta[:, 1].astype(I32)
        counts = counts[0, :N_EXPERTS].astype(I32)
        padded = (counts + EXPERT_ROWS - 1) // EXPERT_ROWS * EXPERT_ROWS
        pend = jnp.cumsum(padded)
        pstart = pend - padded
        dest0 = pstart[e0] + meta[:, 2].astype(I32)
        dest1 = pstart[e1] + meta[:, 3].astype(I32)
        slot_tok = jnp.zeros((n_eb * EXPERT_ROWS,), I32).at[dest0].set(tok).at[dest1].set(tok)
        block_e = jnp.minimum(jnp.searchsorted(pend, jnp.arange(n_eb, dtype=I32) * EXPERT_ROWS, side='right'),
                              N_EXPERTS - 1).astype(I32)
        n_used = (pend[-1:] // EXPERT_ROWS).astype(I32)
        yb = _experts(h2, slot_tok, block_e, n_used, w_e_gate, w_e_up, w_e_down, l)
        x = _combine(x, g2, meta, yb, dest0, dest1, n_pb)

        def states(rows, nbatch, a_rows):
            r = p32[rows]
            n = r.shape[0] // nbatch
            kv = lambda off, heads: r[:, off:off + 2 * heads * HEAD_DIM].reshape(nbatch, n, 2, heads, HEAD_DIM)
            return (kv(OFF_KA, H_A)[:, n - a_rows:], kv(OFF_KB, H_B_KV),
                    r[:, OFF_MISC:OFF_MISC + D_IDX].reshape(nbatch, n, D_IDX), kv(OFF_KC, H_C),
                    r[:, OFF_MISC + MISC_FC:OFF_MISC + MISC_FC + H_C].reshape(nbatch, n, H_C))

        states_p.append(states(slice(0, s), bp, keep))
        states_s.append(states(slice(s, m), nb, t))

    stk = lambda sts, i: jnp.stack([st[i] for st in sts], axis=0)
    return (x[:s].reshape(bp, s, d), x[s:].reshape(nb, t, d),
            *[stk(states_p, i) for i in range(5)], *[stk(states_s, i) for i in range(5)])
```
